```python
import jax, jax.numpy as jnp
from jax import lax
import numpy as np

D_MODEL = 1024
BATCH = 8
SEQ = 2048
DEPTH = 2
DEC_BATCH = 4
DEC_SEQ = 4096
PAST_LEN = 128

D_MIX = D_MODEL
HG_WIDTH = D_MIX // 2
HG_HEADS = 4
HG_KEY_DIM = HG_WIDTH // HG_HEADS
HG_VAL_DIM = HG_WIDTH // HG_HEADS
HG_CHUNK = 32
LB_FLOOR = 1e-30
ATTN_WIDTH = D_MIX - HG_WIDTH
HEAD_DIM = 64
N_Q_HEADS = ATTN_WIDTH // HEAD_DIM
N_KV_HEADS = 2
GQA_GROUP = N_Q_HEADS // N_KV_HEADS
KV_WIDTH = N_KV_HEADS * HEAD_DIM
Q_BLOCK = 128
GRID_W = 64
ROPE_AXIS_DIM = HEAD_DIM // 2
ROPE_THETA = 10000.0
N_GROUPS = 4
EXPERTS_PER_GROUP = 4
N_EXPERTS = N_GROUPS * EXPERTS_PER_GROUP
TOP_K_IN_GROUP = 2
D_EXPERT = D_MODEL // 2
N_MOD = 6
NORM_EPS = 1e-6
IN_PROJ_WIDTH = 5 * HG_WIDTH + ATTN_WIDTH + 2 * KV_WIDTH
IN_SPLITS = (HG_WIDTH, 2 * HG_WIDTH, 3 * HG_WIDTH, 4 * HG_WIDTH, 5 * HG_WIDTH,
             5 * HG_WIDTH + ATTN_WIDTH, 5 * HG_WIDTH + ATTN_WIDTH + KV_WIDTH)

kernel_name = "hymba_hgrn2_axialgqa_hiermoe_encoder"


def rms_norm(x, gain):
    x32 = x.astype(jnp.float32)
    y = x32 * lax.rsqrt(jnp.mean(x32 * x32, axis=-1, keepdims=True) + NORM_EPS)
    return (y * gain.astype(jnp.float32)).astype(x.dtype)


def hgrn_lower_bounds(hg_lb):
    sm = jax.nn.softmax(hg_lb.astype(jnp.float32), axis=0)
    return jnp.cumsum(sm, axis=0) - sm[0:1]


def chunk_recurrence(q, k, v, log_f):
    N, L, H, K = q.shape
    V = v.shape[-1]
    nc = L // HG_CHUNK

    def to_chunks(a):
        return a.reshape(N, nc, HG_CHUNK, H, a.shape[-1]).transpose(1, 0, 3, 2, 4)

    lower = jnp.tril(jnp.ones((HG_CHUNK, HG_CHUNK), dtype=bool))[:, :, None]

    def step(S, inp):
        qc, kc, vc, gc = inp
        b = jnp.cumsum(gc, axis=2)
        diff = b[:, :, :, None, :] - b[:, :, None, :, :]
        decay = jnp.where(lower, jnp.exp(jnp.minimum(diff, 0.0)), 0.0)
        A = jnp.einsum('nhtk,nhtsk,nhsk->nhts', qc, decay, kc)
        o = (jnp.einsum('nhts,nhsv->nhtv', A, vc)
             + jnp.einsum('nhtk,nhkv->nhtv', qc * jnp.exp(b), S))
        b_end = b[:, :, -1:, :]
        S = (jnp.exp(b_end[:, :, 0, :])[..., None] * S
             + jnp.einsum('nhsk,nhsv->nhkv', kc * jnp.exp(b_end - b), vc))
        return S, o

    S0 = jnp.zeros((N, H, K, V), jnp.float32)
    _, o = lax.scan(step, S0, (to_chunks(q), to_chunks(k), to_chunks(v), to_chunks(log_f)))
    return o.transpose(1, 0, 3, 2, 4).reshape(N, L, H, V)


def hgrn2_mixer(hq, hf_fwd, hf_bwd, hi, hg, lb, out_gain):
    B, L, _ = hq.shape
    shp = (B, L, HG_HEADS, HG_KEY_DIM)
    q = jax.nn.silu(hq.astype(jnp.float32)).reshape(shp)
    lb = lb.reshape(HG_HEADS, HG_KEY_DIM)
    log_lb = jnp.log(jnp.maximum(lb, LB_FLOOR))
    log_1m_lb = jnp.log1p(-lb)

    def gates(hf):
        fx = hf.astype(jnp.float32).reshape(shp)
        log_f = jnp.logaddexp(log_lb, log_1m_lb + jax.nn.log_sigmoid(fx))
        k = (1.0 - lb) * jax.nn.sigmoid(-fx)
        return log_f, k

    lf_f, k_f = gates(hf_fwd)
    lf_b, k_b = gates(hf_bwd)
    v = hi.astype(jnp.float32).reshape(B, L, HG_HEADS, HG_VAL_DIM)
    flip = lambda a: a[:, ::-1]
    o = chunk_recurrence(jnp.concatenate([q, flip(q)], axis=0),
                         jnp.concatenate([k_f, flip(k_b)], axis=0),
                         jnp.concatenate([v, flip(v)], axis=0),
                         jnp.concatenate([lf_f, flip(lf_b)], axis=0))
    o = o[:B] + flip(o[B:])
    o = rms_norm(o, out_gain)
    o = o * jax.nn.silu(hg.astype(jnp.float32)).reshape(B, L, HG_HEADS, HG_VAL_DIM)
    return o.reshape(B, L, HG_WIDTH).astype(hq.dtype)


def axial_rope_tables(L):
    rows = L // GRID_W
    t_row = jnp.repeat(jnp.arange(rows, dtype=jnp.float32), GRID_W)
    t_col = jnp.tile(jnp.arange(GRID_W, dtype=jnp.float32), rows)
    inv_freq = ROPE_THETA ** (-jnp.arange(0, ROPE_AXIS_DIM, 2, dtype=jnp.float32) / ROPE_AXIS_DIM)
    ang_r = t_row[:, None] * inv_freq
    ang_c = t_col[:, None] * inv_freq
    return (jnp.cos(ang_r), jnp.sin(ang_r), jnp.cos(ang_c), jnp.sin(ang_c))


def rotate(x, cos, sin):
    half = x.shape[-1] // 2
    x1, x2 = x[..., :half], x[..., half:]
    cos = cos[None, :, None, :]
    sin = sin[None, :, None, :]
    return jnp.concatenate([x1 * cos - x2 * sin, x1 * sin + x2 * cos], axis=-1)


def axial_rope(x, rope):
    cos_r, sin_r, cos_c, sin_c = rope
    x32 = x.astype(jnp.float32)
    out = jnp.concatenate([rotate(x32[..., :ROPE_AXIS_DIM], cos_r, sin_r),
                           rotate(x32[..., ROPE_AXIS_DIM:], cos_c, sin_c)], axis=-1)
    return out.astype(x.dtype)


def block_attention(q, k, v):
    B, L = q.shape[:2]
    nb = L // Q_BLOCK
    qb = q.reshape(B, nb, Q_BLOCK, N_KV_HEADS, GQA_GROUP, HEAD_DIM).transpose(1, 0, 2, 3, 4, 5)
    scale = HEAD_DIM ** -0.5

    def one_block(qi):
        s = jnp.einsum('bqhgd,bkhd->bhgqk', qi, k, preferred_element_type=jnp.float32) * scale
        p = jax.nn.softmax(s, axis=-1).astype(v.dtype)
        return jnp.einsum('bhgqk,bkhd->bqhgd', p, v)

    ob = lax.map(one_block, qb)
    return ob.transpose(1, 0, 2, 3, 4, 5).reshape(B, L, N_Q_HEADS * HEAD_DIM)


def attention_mixer(aq, ak, av, q_gain, k_gain, rope):
    B, L, _ = aq.shape
    q = rms_norm(aq.reshape(B, L, N_Q_HEADS, HEAD_DIM), q_gain)
    k = rms_norm(ak.reshape(B, L, N_KV_HEADS, HEAD_DIM), k_gain)
    v = av.reshape(B, L, N_KV_HEADS, HEAD_DIM)
    return block_attention(axial_rope(q, rope), axial_rope(k, rope), v)


def token_mixer(h, w_in, w_out, lb, hg_gain, q_gain, k_gain, rope):
    proj = h @ w_in
    hq, hf_f, hf_b, hi, hg, aq, ak, av = jnp.split(proj, IN_SPLITS, axis=-1)
    o_hg = hgrn2_mixer(hq, hf_f, hf_b, hi, hg, lb, hg_gain)
    o_at = attention_mixer(aq, ak, av, q_gain, k_gain, rope)
    return jnp.concatenate([o_hg, o_at], axis=-1) @ w_out


def hier_moe(h, w_group, b_group, w_router, b_router, w_g, w_u, w_d):
    B, L, D = h.shape
    t = h.reshape(B * L, D)
    p_group = jax.nn.softmax((t @ w_group).astype(jnp.float32) + b_group.astype(jnp.float32), axis=-1)
    pg_top, g_idx = lax.top_k(p_group, 1)
    le = ((t @ w_router).astype(jnp.float32) + b_router.astype(jnp.float32)).reshape(
        -1, N_GROUPS, EXPERTS_PER_GROUP)
    le_sel = jnp.take_along_axis(le, g_idx[:, :, None], axis=1)[:, 0]
    pe_top, e_idx = lax.top_k(jax.nn.softmax(le_sel, axis=-1), TOP_K_IN_GROUP)
    w = pg_top * pe_top / jnp.sum(pe_top, axis=-1, keepdims=True)
    expert_id = g_idx * EXPERTS_PER_GROUP + e_idx
    gate = jnp.sum(jax.nn.one_hot(expert_id, N_EXPERTS, dtype=jnp.float32) * w[..., None], axis=1)
    gate = gate.astype(t.dtype)
    out = jnp.zeros_like(t)
    for e in range(N_EXPERTS):
        hid = jax.nn.silu(t @ w_g[e]) * (t @ w_u[e])
        out = out + gate[:, e:e + 1] * (hid @ w_d[e])
    return out.reshape(B, L, D)


def trunk(x, c, w_in, w_out, hg_lb, hg_out_norm, q_norm, k_norm, norm_mix, norm_ffn,
          w_ada, b_ada, w_group, b_group, w_router, b_router, w_e_gate, w_e_up, w_e_down,
          final_norm):
    L = x.shape[1]
    rope = axial_rope_tables(L)
    lbs = hgrn_lower_bounds(hg_lb)
    cs = jax.nn.silu(c)
    for l in range(DEPTH):
        mod = (cs @ w_ada[l] + b_ada[l])[:, None, :]
        sh1, sc1, g1, sh2, sc2, g2 = jnp.split(mod, N_MOD, axis=-1)
        h = rms_norm(x, norm_mix[l]) * (1.0 + sc1) + sh1
        x = x + g1 * token_mixer(h, w_in[l], w_out[l], lbs[l], hg_out_norm[l],
                                 q_norm[l], k_norm[l], rope)
        h = rms_norm(x, norm_ffn[l]) * (1.0 + sc2) + sh2
        x = x + g2 * hier_moe(h, w_group[l], b_group[l], w_router[l], b_router[l],
                              w_e_gate[l], w_e_up[l], w_e_down[l])
    return rms_norm(x, final_norm)


def setup_inputs(seed: int = 0) -> dict:
    key = jax.random.key(seed)
    ks = jax.random.split(key, 24)
    f32 = jnp.float32
    nrm = lambda k, shape, s: jax.random.normal(k, shape, f32) * s
    gain = lambda k, shape: 1.0 + 0.02 * jax.random.normal(k, shape, f32)
    return {
        "x_prompt": nrm(ks[0], (BATCH, SEQ, D_MODEL), 1.0),
        "x_sample": nrm(ks[1], (DEC_BATCH, DEC_SEQ, D_MODEL), 1.0),
        "c_prompt": nrm(ks[2], (BATCH, D_MODEL), 1.0),
        "c_sample": nrm(ks[3], (DEC_BATCH, D_MODEL), 1.0),
        "w_in": nrm(ks[4], (DEPTH, D_MODEL, IN_PROJ_WIDTH), D_MODEL ** -0.5),
        "w_out": nrm(ks[5], (DEPTH, D_MIX, D_MODEL), D_MIX ** -0.5),
        "hg_lb": nrm(ks[6], (DEPTH, HG_WIDTH), 1.0),
        "hg_out_norm": gain(ks[7], (DEPTH, HG_VAL_DIM)),
        "q_norm": gain(ks[8], (DEPTH, HEAD_DIM)),
        "k_norm": gain(ks[9], (DEPTH, HEAD_DIM)),
        "norm_mix": gain(ks[10], (DEPTH, D_MODEL)),
        "norm_ffn": gain(ks[11], (DEPTH, D_MODEL)),
        "w_ada": nrm(ks[12], (DEPTH, D_MODEL, N_MOD * D_MODEL), D_MODEL ** -0.5),
        "b_ada": nrm(ks[13], (DEPTH, N_MOD * D_MODEL), 0.02),
        "w_group": nrm(ks[14], (DEPTH, D_MODEL, N_GROUPS), D_MODEL ** -0.5),
        "b_group": nrm(ks[15], (DEPTH, N_GROUPS), 0.01),
        "w_router": nrm(ks[16], (DEPTH, D_MODEL, N_EXPERTS), D_MODEL ** -0.5),
        "b_router": nrm(ks[17], (DEPTH, N_EXPERTS), 0.01),
        "w_e_gate": nrm(ks[18], (DEPTH, N_EXPERTS, D_MODEL, D_EXPERT), D_MODEL ** -0.5),
        "w_e_up": nrm(ks[19], (DEPTH, N_EXPERTS, D_MODEL, D_EXPERT), D_MODEL ** -0.5),
        "w_e_down": nrm(ks[20], (DEPTH, N_EXPERTS, D_EXPERT, D_MODEL), D_EXPERT ** -0.5),
        "final_norm": gain(ks[21], (D_MODEL,)),
    }


def reference(x_prompt, x_sample, c_prompt, c_sample, w_in, w_out, hg_lb, hg_out_norm,
              q_norm, k_norm, norm_mix, norm_ffn, w_ada, b_ada, w_group, b_group,
              w_router, b_router, w_e_gate, w_e_up, w_e_down, final_norm):
    y_prompt = trunk(x_prompt, c_prompt, w_in, w_out, hg_lb, hg_out_norm, q_norm, k_norm,
                     norm_mix, norm_ffn, w_ada, b_ada, w_group, b_group, w_router, b_router,
                     w_e_gate, w_e_up, w_e_down, final_norm)
    y_sample = trunk(x_sample, c_sample, w_in, w_out, hg_lb, hg_out_norm, q_norm, k_norm,
                     norm_mix, norm_ffn, w_ada, b_ada, w_group, b_group, w_router, b_router,
                     w_e_gate, w_e_up, w_e_down, final_norm)
    return (y_prompt, y_sample)
```

```python
import functools

import numpy as np
import jax
import jax.numpy as jnp
from jax import lax
from jax.experimental import pallas as pl
from jax.experimental.pallas import tpu as pltpu

F32 = jnp.float32
BF16 = jnp.bfloat16
U32 = jnp.uint32

D_MODEL = 1024
HG_WIDTH = 512
HG_HEADS = 4
HG_DIM = 128
LB_FLOOR = 1e-30
ATTN_WIDTH = 512
HEAD_DIM = 64
N_Q_HEADS = 8
N_KV_HEADS = 2
GQA_GROUP = 4
KV_WIDTH = 128
GRID_W = 64
ROPE_AXIS_DIM = 32
ROPE_THETA = 10000.0
N_GROUPS = 4
EXPERTS_PER_GROUP = 4
N_EXPERTS = 16
D_EXPERT = 512
N_MOD = 6
NORM_EPS = 1e-6
IN_PROJ_WIDTH = 5 * HG_WIDTH + ATTN_WIDTH + 2 * KV_WIDTH
COL_HQ, COL_HF_FWD, COL_HF_BWD, COL_HI, COL_HG, COL_AQ = 0, 1, 2, 3, 4, 5
COL_AK, COL_AV = 24, 25

LANES = 128
SUBLANES = 8
VMEM_LIMIT = 56 * 1024 * 1024
ROW_TILE = 512
HG_CHUNK = 128
KV_BLOCK = 512
Q_TILE = 128
EXPERT_TILE = 128
MOE_SEGMENT = 4096
ROUTE_LANES = 128


def _cparams(*sem):
    return pltpu.CompilerParams(dimension_semantics=sem, vmem_limit_bytes=VMEM_LIMIT)


def _dot(a, b):
    return jnp.dot(a, b, preferred_element_type=F32)


def _dot_nt(a, b):
    return lax.dot_general(a, b, (((1,), (1,)), ((), ())), preferred_element_type=F32)


def _split_bf16(x):
    hi = x.astype(BF16)
    lo = (x - hi.astype(F32)).astype(BF16)
    return hi, lo


def _sigmoid(x):
    return 1.0 / (1.0 + jnp.exp(-x))


def _mod_body(c_ref, w_ref, b_ref, o_ref):
    c = c_ref[...]
    cs = c * _sigmoid(c)
    o_ref[0] = _dot(cs.astype(BF16), w_ref[0].astype(BF16)) + b_ref[0]


def _modulation(c_all, w_ada, b_ada):
    nb = c_all.shape[0]
    depth, _, n = w_ada.shape
    tn = 1024
    return pl.pallas_call(
        _mod_body,
        grid=(depth, n // tn),
        in_specs=[pl.BlockSpec((nb, D_MODEL), lambda l, j: (0, 0)),
                  pl.BlockSpec((1, D_MODEL, tn), lambda l, j: (l, 0, j)),
                  pl.BlockSpec((1, 1, tn), lambda l, j: (l, 0, j))],
        out_specs=pl.BlockSpec((1, nb, tn), lambda l, j: (l, 0, j)),
        out_shape=jax.ShapeDtypeStruct((depth, nb, n), F32),
        compiler_params=_cparams("arbitrary", "arbitrary"),
        name="adaln_modulation",
    )(c_all, w_ada, b_ada.reshape(depth, 1, n))


def _lb_body(p_ref, o_ref):
    depth = p_ref.shape[0]
    rows = [p_ref[l:l + 1, :] for l in range(depth)]
    m = functools.reduce(jnp.maximum, rows)
    es = [jnp.exp(r - m) for r in rows]
    tot = functools.reduce(lambda a, b: a + b, es)
    sm = [e / tot for e in es]
    acc = jnp.zeros_like(sm[0])
    for l in range(depth):
        acc = acc + sm[l]
        o_ref[l:l + 1, :] = acc - sm[0]


def _lower_bounds(hg_lb):
    return pl.pallas_call(
        _lb_body,
        out_shape=jax.ShapeDtypeStruct(hg_lb.shape, F32),
        name="hgrn_lower_bounds",
    )(hg_lb)


def _inproj_body(x_ref, mod_ref, gain_ref, w_ref, p_ref):
    x = x_ref[0]
    ms = jnp.mean(x * x, axis=-1, keepdims=True)
    y = x * lax.rsqrt(ms + NORM_EPS) * gain_ref[0]
    h = y * (1.0 + mod_ref[0, 1:2, :]) + mod_ref[0, 0:1, :]
    p_ref[0] = _dot(h.astype(BF16), w_ref[0]).astype(BF16)


def _in_proj(x, modr, mod_row0, gain, w_in_bf, layer):
    b, l, _ = x.shape
    tm = min(ROW_TILE, l)
    return pl.pallas_call(
        _inproj_body,
        grid=(b, l // tm),
        in_specs=[pl.BlockSpec((1, tm, D_MODEL), lambda i, j: (i, j, 0)),
                  pl.BlockSpec((1, N_MOD, D_MODEL), lambda i, j: (mod_row0 + i, 0, 0)),
                  pl.BlockSpec((1, 1, D_MODEL), lambda i, j: (layer, 0, 0)),
                  pl.BlockSpec((1, D_MODEL, IN_PROJ_WIDTH), lambda i, j: (layer, 0, 0))],
        out_specs=pl.BlockSpec((1, tm, IN_PROJ_WIDTH), lambda i, j: (i, j, 0)),
        out_shape=jax.ShapeDtypeStruct((b, l, IN_PROJ_WIDTH), BF16),
        compiler_params=_cparams("arbitrary", "arbitrary"),
        name="in_proj",
    )(x, modr, gain, w_in_bf)


HG_LEVELS = (64, 32, 16, 8, 4, 2, 1)


def _hg_constants():
    c = HG_CHUNK
    t = np.arange(c)[:, None]
    s = np.arange(c)[None, :]
    masks = np.zeros((2, len(HG_LEVELS) + 1, c, c), np.float32)
    for li, h in enumerate(HG_LEVELS):
        same = (t // (2 * h)) == (s // (2 * h))
        masks[0, li] = same & ((t % (2 * h)) >= h) & ((s % (2 * h)) < h)
        masks[1, li] = same & ((t % (2 * h)) < h) & ((s % (2 * h)) >= h)
    masks[:, len(HG_LEVELS)] = (t == s)
    cum = np.stack([(s <= t), (s >= t)]).astype(np.float32)
    return jnp.asarray(masks), jnp.asarray(cum, dtype=BF16)


def _row_bcast(a, rows, blk):
    parts = [jnp.broadcast_to(a[r:r + 1, :], (blk, a.shape[1])) for r in rows]
    return parts[0] if len(parts) == 1 else jnp.concatenate(parts, axis=0)


def _level_reference(cum, h, reverse, sub):
    c = cum.shape[0]
    if 2 * h >= SUBLANES:
        rows = [p0 + (h if reverse else h - 1) for p0 in range(0, c, 2 * h)]
        return _row_bcast(cum, rows, 2 * h)
    out = None
    for p0 in reversed(range(0, SUBLANES, 2 * h)):
        r = p0 + (h if reverse else h - 1)
        piece = _row_bcast(cum, [v0 + r for v0 in range(0, c, SUBLANES)], SUBLANES)
        out = piece if out is None else jnp.where(sub < p0 + 2 * h, piece, out)
    return out


def _hg_direction(hq, hf, hi, lb, cum_mat, masks_ref, d, st_ref, head, reverse):
    c = HG_CHUNK
    q = hq * _sigmoid(hq)
    sig = _sigmoid(hf)
    one_m_lb = 1.0 - lb
    g = jnp.log(jnp.maximum(lb, LB_FLOOR) + one_m_lb * sig)
    kk = one_m_lb * _sigmoid(-hf)
    g_hi, g_lo = _split_bf16(g)
    cum = _dot(cum_mat, g_hi) + _dot(cum_mat, g_lo)
    sub = lax.broadcasted_iota(jnp.int32, (c, HG_DIM), 0) % SUBLANES

    a = jnp.zeros((c, c), F32)
    for li, h in enumerate(HG_LEVELS):
        ref = _level_reference(cum, h, reverse, sub)
        e = jnp.exp(-jnp.abs(cum - ref))
        al = _dot_nt((q * e).astype(BF16), (kk * e).astype(BF16))
        a = a + jnp.where(masks_ref[d, li] != 0.0, al, 0.0)
    diag = jnp.sum(q * kk, axis=-1, keepdims=True)
    a = a + jnp.where(masks_ref[d, len(HG_LEVELS)] != 0.0, diag, 0.0)

    st = st_ref[d, head]
    tot = cum[0:1, :] if reverse else cum[c - 1:c, :]
    o = _dot(a.astype(BF16), hi.astype(BF16))
    o = o + _dot_nt((q * jnp.exp(cum)).astype(BF16), st.astype(BF16))
    k_end = (kk * jnp.exp(tot - cum)).astype(BF16)
    st_ref[d, head] = jnp.exp(tot) * st + _dot(hi.T.astype(BF16), k_end)
    return o


def _hgrn_body(qf_ref, ff_ref, if_ref, qb_ref, fb_ref, ib_ref, lb_ref, cum_ref, masks_ref,
               of_ref, ob_ref, st_ref):
    @pl.when(pl.program_id(1) == 0)
    def _():
        st_ref[...] = jnp.zeros_like(st_ref)

    for head in range(HG_HEADS):
        sl = slice(head * HG_DIM, (head + 1) * HG_DIM)
        lb = lb_ref[0, :, sl]
        for d, (q_ref, f_ref, i_ref, o_ref) in enumerate(((qf_ref, ff_ref, if_ref, of_ref),
                                                          (qb_ref, fb_ref, ib_ref, ob_ref))):
            o = _hg_direction(q_ref[0, :, sl].astype(F32), f_ref[0, :, sl].astype(F32),
                              i_ref[0, :, sl].astype(F32), lb, cum_ref[d], masks_ref, d, st_ref,
                              head, reverse=(d == 1))
            o_ref[0, :, sl] = o.astype(BF16)


def _hgrn(proj, lbs, layer):
    b, l, _ = proj.shape
    c = HG_CHUNK
    nc = l // c
    masks, cum = _hg_constants()
    fwd = lambda col: pl.BlockSpec((1, c, HG_WIDTH), lambda i, j: (i, j, col))
    bwd = lambda col: pl.BlockSpec((1, c, HG_WIDTH), lambda i, j: (i, nc - 1 - j, col))
    out_sd = jax.ShapeDtypeStruct((b, l, HG_WIDTH), BF16)
    return pl.pallas_call(
        _hgrn_body,
        grid=(b, nc),
        in_specs=[fwd(COL_HQ), fwd(COL_HF_FWD), fwd(COL_HI),
                  bwd(COL_HQ), bwd(COL_HF_BWD), bwd(COL_HI),
                  pl.BlockSpec((1, 1, HG_WIDTH), lambda i, j: (layer, 0, 0)),
                  pl.BlockSpec(cum.shape, lambda i, j: (0, 0, 0)),
                  pl.BlockSpec(masks.shape, lambda i, j: (0, 0, 0, 0))],
        out_specs=[pl.BlockSpec((1, c, HG_WIDTH), lambda i, j: (i, j, 0)),
                   pl.BlockSpec((1, c, HG_WIDTH), lambda i, j: (i, nc - 1 - j, 0))],
        out_shape=[out_sd, out_sd],
        scratch_shapes=[pltpu.VMEM((2, HG_HEADS, HG_DIM, HG_DIM), F32)],
        compiler_params=_cparams("arbitrary", "arbitrary"),
        name="hgrn2_recurrence",
    )(proj, proj, proj, proj, proj, proj, lbs, cum, masks)


def _rope_tables(l):
    lane = np.arange(LANES)
    dd = lane % HEAD_DIM
    axis = dd // ROPE_AXIS_DIM
    first_half = (dd % ROPE_AXIS_DIM) < (ROPE_AXIS_DIM // 2)
    freq_idx = dd % (ROPE_AXIS_DIM // 2)
    inv_freq = ROPE_THETA ** (-jnp.arange(0, ROPE_AXIS_DIM, 2, dtype=F32) / ROPE_AXIS_DIM)
    t = jnp.arange(l)
    pos = jnp.where(jnp.asarray(axis)[None, :] == 0, (t // GRID_W)[:, None], (t % GRID_W)[:, None])
    ang = pos.astype(F32) * inv_freq[jnp.asarray(freq_idx)][None, :]
    sign = jnp.where(jnp.asarray(first_half), -1.0, 1.0)[None, :]
    return jnp.cos(ang), jnp.sin(ang) * sign


def _norm_rope(x, gain, cos_t, sin_t, bd, first_half):
    x2_hi, x2_lo = _split_bf16(x * x)
    ss = _dot(x2_hi, bd) + _dot(x2_lo, bd)
    xn = x * lax.rsqrt(ss * (1.0 / HEAD_DIM) + NORM_EPS) * gain
    half = ROPE_AXIS_DIM // 2
    partner = jnp.where(first_half, pltpu.roll(xn, LANES - half, 1), pltpu.roll(xn, half, 1))
    return xn * cos_t + partner * sin_t


def _qkprep_body(q_ref, k_ref, qg_ref, kg_ref, cos_ref, sin_ref, bd_ref, qo_ref, kt_ref):
    cos_t = cos_ref[...]
    sin_t = sin_ref[...]
    bd = bd_ref[...]
    lane = lax.broadcasted_iota(jnp.int32, cos_t.shape, 1)
    first_half = (lane % ROPE_AXIS_DIM) < (ROPE_AXIS_DIM // 2)
    for j in range(ATTN_WIDTH // LANES):
        sl = slice(j * LANES, (j + 1) * LANES)
        qr = _norm_rope(q_ref[0, :, sl].astype(F32), qg_ref[0], cos_t, sin_t, bd, first_half)
        qo_ref[0, :, sl] = (qr * (HEAD_DIM ** -0.5)).astype(BF16)
    kr = _norm_rope(k_ref[0].astype(F32), kg_ref[0], cos_t, sin_t, bd, first_half)
    krt = kr.T
    for h in range(N_KV_HEADS):
        kt_ref[0, h, 0] = krt[h * HEAD_DIM:(h + 1) * HEAD_DIM, :].astype(BF16)


def _qk_prep(proj, q_gain2, k_gain2, layer):
    b, l, _ = proj.shape
    tk = min(KV_BLOCK, l)
    cos_t, sin_t = _rope_tables(l)
    blk = np.arange(LANES) // HEAD_DIM
    bd = jnp.asarray((blk[:, None] == blk[None, :]).astype(np.float32), dtype=BF16)
    return pl.pallas_call(
        _qkprep_body,
        grid=(b, l // tk),
        in_specs=[pl.BlockSpec((1, tk, ATTN_WIDTH), lambda i, j: (i, j, COL_AQ)),
                  pl.BlockSpec((1, tk, KV_WIDTH), lambda i, j: (i, j, COL_AK)),
                  pl.BlockSpec((1, 1, LANES), lambda i, j: (layer, 0, 0)),
                  pl.BlockSpec((1, 1, LANES), lambda i, j: (layer, 0, 0)),
                  pl.BlockSpec((tk, LANES), lambda i, j: (j, 0)),
                  pl.BlockSpec((tk, LANES), lambda i, j: (j, 0)),
                  pl.BlockSpec((LANES, LANES), lambda i, j: (0, 0))],
        out_specs=[pl.BlockSpec((1, tk, ATTN_WIDTH), lambda i, j: (i, j, 0)),
                   pl.BlockSpec((1, N_KV_HEADS, 1, HEAD_DIM, tk), lambda i, j: (i, 0, j, 0, 0))],
        out_shape=[jax.ShapeDtypeStruct((b, l, ATTN_WIDTH), BF16),
                   jax.ShapeDtypeStruct((b, N_KV_HEADS, l // tk, HEAD_DIM, tk), BF16)],
        compiler_params=_cparams("arbitrary", "arbitrary"),
        name="qk_norm_rope",
    )(proj, proj, q_gain2, k_gain2, cos_t, sin_t, bd)


def _attn_body(q_ref, kt_ref, v_ref, o_ref, m_scr, l_scr, acc_scr):
    tq = q_ref.shape[1]
    nblk = kt_ref.shape[2]
    tk = kt_ref.shape[4]
    q_all = q_ref[0].astype(F32)
    outs = []
    for h in range(N_KV_HEADS):
        heads = [q_all[:, (h * GQA_GROUP + g) * HEAD_DIM:(h * GQA_GROUP + g + 1) * HEAD_DIM]
                 for g in range(GQA_GROUP)]
        qs = jnp.concatenate(heads, axis=0).astype(BF16)
        m_scr[...] = jnp.full_like(m_scr, -jnp.inf)
        l_scr[...] = jnp.zeros_like(l_scr)
        acc_scr[...] = jnp.zeros_like(acc_scr)

        def step(i, carry):
            s = _dot(qs, kt_ref[0, h, i])
            m_prev = m_scr[...]
            m_new = jnp.maximum(m_prev, jnp.max(s, axis=-1, keepdims=True))
            alpha = jnp.exp(m_prev - m_new)
            p = jnp.exp(s - m_new[:, 0:1])
            l_scr[...] = alpha * l_scr[...] + jnp.sum(p, axis=-1, keepdims=True)
            v = v_ref[0, pl.ds(pl.multiple_of(i * tk, tk), tk), :]
            acc_scr[...] = alpha * acc_scr[...] + _dot(p.astype(BF16), v)
            m_scr[...] = m_new
            return carry

        lax.fori_loop(0, nblk, step, 0)
        o = acc_scr[...] / l_scr[...]
        oh = o[:, h * HEAD_DIM:(h + 1) * HEAD_DIM]
        outs += [oh[g * tq:(g + 1) * tq, :] for g in range(GQA_GROUP)]
    o_ref[0] = jnp.concatenate(outs, axis=1).astype(BF16)


def _attention(q_rot, kt, proj):
    b, l, _ = q_rot.shape
    tq = min(Q_TILE, l)
    nblk, tk = kt.shape[2], kt.shape[4]
    rows = GQA_GROUP * tq
    return pl.pallas_call(
        _attn_body,
        grid=(b, l // tq),
        in_specs=[pl.BlockSpec((1, tq, ATTN_WIDTH), lambda i, j: (i, j, 0)),
                  pl.BlockSpec((1, N_KV_HEADS, nblk, HEAD_DIM, tk), lambda i, j: (i, 0, 0, 0, 0)),
                  pl.BlockSpec((1, l, KV_WIDTH), lambda i, j: (i, 0, COL_AV))],
        out_specs=pl.BlockSpec((1, tq, ATTN_WIDTH), lambda i, j: (i, j, 0)),
        out_shape=jax.ShapeDtypeStruct((b, l, ATTN_WIDTH), BF16),
        scratch_shapes=[pltpu.VMEM((rows, LANES), F32), pltpu.VMEM((rows, LANES), F32),
                        pltpu.VMEM((rows, KV_WIDTH), F32)],
        compiler_params=_cparams("arbitrary", "arbitrary"),
        name="gqa_attention",
    )(q_rot, kt, proj)


def _pack_bf16_pairs(x):
    n = x.shape[1] // 2
    bits = lax.bitcast_convert_type(x.astype(BF16).astype(F32), U32)
    return (bits[:, :n] >> 16) | bits[:, n:]


def _unpack_bf16_pairs(w):
    lo = lax.bitcast_convert_type(w << 16, F32)
    hi = lax.bitcast_convert_type(w & jnp.uint32(0xFFFF0000), F32)
    return lo, hi


def _route(logits):
    lane = lax.broadcasted_iota(jnp.int32, logits.shape, 1).astype(F32)
    neg = jnp.float32(-jnp.inf)
    lanemin = lambda cond: jnp.min(jnp.where(cond, lane, float(ROUTE_LANES)), axis=-1, keepdims=True)
    gl = jnp.where(lane < N_GROUPS, logits, neg)
    gmax = jnp.max(gl, axis=-1, keepdims=True)
    g_idx = lanemin(gl == gmax)
    pg_top = 1.0 / jnp.sum(jnp.exp(gl - gmax), axis=-1, keepdims=True)
    first = N_GROUPS + g_idx * EXPERTS_PER_GROUP
    in_group = (lane >= first) & (lane < first + EXPERTS_PER_GROUP)
    el = jnp.where(in_group, logits, neg)
    a_max = jnp.max(el, axis=-1, keepdims=True)
    a_idx = lanemin(el == a_max)
    el2 = jnp.where(lane == a_idx, neg, el)
    b_max = jnp.max(el2, axis=-1, keepdims=True)
    b_idx = lanemin(el2 == b_max)
    r = jnp.exp(b_max - a_max)
    w_a = pg_top / (1.0 + r)
    w_b = pg_top * r / (1.0 + r)
    out = jnp.where(lane == 0, a_idx - N_GROUPS,
                    jnp.where(lane == 1, b_idx - N_GROUPS,
                              jnp.where(lane == 2, w_a, jnp.where(lane == 3, w_b, 0.0))))
    return out


def _outproj_body(of_ref, ob_ref, hg_ref, at_ref, x_ref, mod_ref, hgain_ref, wout_ref, fgain_ref,
                  wr_ref, br_ref, xo_ref, hp_ref, rt_ref):
    parts = []
    for head in range(HG_HEADS):
        sl = slice(head * HG_DIM, (head + 1) * HG_DIM)
        o = of_ref[0, :, sl].astype(F32) + ob_ref[0, :, sl].astype(F32)
        o = o * lax.rsqrt(jnp.mean(o * o, axis=-1, keepdims=True) + NORM_EPS) * hgain_ref[0]
        gt = hg_ref[0, :, sl].astype(F32)
        parts.append((o * (gt * _sigmoid(gt))).astype(BF16))
    o_hg = jnp.concatenate(parts, axis=1)
    mix = _dot(o_hg, wout_ref[0, :HG_WIDTH, :]) + _dot(at_ref[0], wout_ref[0, HG_WIDTH:, :])
    x = x_ref[0] + mod_ref[0, 2:3, :] * mix
    xo_ref[0] = x
    y = x * lax.rsqrt(jnp.mean(x * x, axis=-1, keepdims=True) + NORM_EPS) * fgain_ref[0]
    h = y * (1.0 + mod_ref[0, 4:5, :]) + mod_ref[0, 3:4, :]
    hp_ref[0] = _pack_bf16_pairs(h)
    h_hi, h_lo = _split_bf16(h)
    w_hi, w_lo = _split_bf16(wr_ref[0])
    logits = _dot(h_hi, w_hi) + _dot(h_hi, w_lo) + _dot(h_lo, w_hi) + br_ref[0]
    rt_ref[0] = _route(logits)


def _out_proj(o_f, o_b, proj, o_at, x, modr, mod_row0, hgain, w_out_bf, fgain, w_route, b_route, layer):
    b, l, _ = x.shape
    tm = min(ROW_TILE, l)
    tok = lambda w, col=0: pl.BlockSpec((1, tm, w), lambda i, j: (i, j, col))
    lay = lambda *s: pl.BlockSpec((1,) + s, lambda i, j: (layer,) + (0,) * len(s))
    return pl.pallas_call(
        _outproj_body,
        grid=(b, l // tm),
        in_specs=[tok(HG_WIDTH), tok(HG_WIDTH), tok(HG_WIDTH, COL_HG), tok(ATTN_WIDTH), tok(D_MODEL),
                  pl.BlockSpec((1, N_MOD, D_MODEL), lambda i, j: (mod_row0 + i, 0, 0)),
                  lay(1, HG_DIM), lay(D_MODEL, D_MODEL), lay(1, D_MODEL),
                  lay(D_MODEL, ROUTE_LANES), lay(1, ROUTE_LANES)],
        out_specs=[tok(D_MODEL), tok(D_MODEL // 2), tok(ROUTE_LANES)],
        out_shape=[jax.ShapeDtypeStruct((b, l, D_MODEL), F32),
                   jax.ShapeDtypeStruct((b, l, D_MODEL // 2), U32),
                   jax.ShapeDtypeStruct((b, l, ROUTE_LANES), F32)],
        compiler_params=_cparams("arbitrary", "arbitrary"),
        name="out_proj_router",
    )(o_f, o_b, proj, o_at, x, modr, hgain, w_out_bf, fgain, w_route, b_route)


def _moe_body(te_ref, tv_ref, idx_ref, h_ref, wg_ref, wu_ref, wd_ref, o_ref, xs_scr):
    i = pl.program_id(0)
    tr = o_ref.shape[0]

    @pl.when(tv_ref[i] != 0)
    def _():
        def gather(r, carry):
            xs_scr[pl.ds(r, 1), :] = h_ref[0, pl.ds(idx_ref[0, 0, r], 1), :]
            return carry

        lax.fori_loop(0, tr, gather, 0, unroll=8)
        lo, hi = _unpack_bf16_pairs(xs_scr[...])
        lo = lo.astype(BF16)
        hi = hi.astype(BF16)
        half = D_MODEL // 2
        gt = _dot(lo, wg_ref[0, :half, :]) + _dot(hi, wg_ref[0, half:, :])
        up = _dot(lo, wu_ref[0, :half, :]) + _dot(hi, wu_ref[0, half:, :])
        hid = (gt * _sigmoid(gt)) * up
        o_ref[...] = _pack_bf16_pairs(_dot(hid.astype(BF16), wd_ref[0]))

    @pl.when(tv_ref[i] == 0)
    def _():
        o_ref[...] = jnp.zeros_like(o_ref)


def _moe_ffn(hp_seg, row_token, tile_expert, tile_valid, wg, wu, wd, layer):
    nseg, seg, half = hp_seg.shape
    n_tiles = tile_expert.shape[0]
    tr = EXPERT_TILE
    tiles_per_seg = n_tiles // nseg
    wspec = lambda k, n: pl.BlockSpec((1, k, n), lambda i, te, tv: (layer * N_EXPERTS + te[i], 0, 0))
    return pl.pallas_call(
        _moe_body,
        grid_spec=pltpu.PrefetchScalarGridSpec(
            num_scalar_prefetch=2,
            grid=(n_tiles,),
            in_specs=[pl.BlockSpec((1, 1, tr), lambda i, te, tv: (i, 0, 0), memory_space=pltpu.SMEM),
                      pl.BlockSpec((1, seg, half), lambda i, te, tv: (i // tiles_per_seg, 0, 0)),
                      wspec(D_MODEL, D_EXPERT), wspec(D_MODEL, D_EXPERT), wspec(D_EXPERT, D_MODEL)],
            out_specs=pl.BlockSpec((tr, half), lambda i, te, tv: (i, 0)),
            scratch_shapes=[pltpu.VMEM((tr, half), U32)]),
        out_shape=jax.ShapeDtypeStruct((n_tiles * tr, half), U32),
        compiler_params=_cparams("arbitrary"),
        name="routed_expert_ffn",
    )(tile_expert, tile_valid, row_token.reshape(n_tiles, 1, tr), hp_seg, wg, wu, wd)


def _combine_body(pa_ref, pb_ref, rows_ref, rt_ref, x_ref, mod_ref, fin_ref, o_ref, a_scr, b_scr, *,
                  final):
    tm = o_ref.shape[1]

    def gather(r, carry):
        a_scr[pl.ds(r, 1), :] = rows_ref[0, pl.ds(pa_ref[0, 0, r], 1), :]
        b_scr[pl.ds(r, 1), :] = rows_ref[0, pl.ds(pb_ref[0, 0, r], 1), :]
        return carry

    lax.fori_loop(0, tm, gather, 0, unroll=8)
    a_lo, a_hi = _unpack_bf16_pairs(a_scr[...])
    b_lo, b_hi = _unpack_bf16_pairs(b_scr[...])
    rt = rt_ref[0]
    w_a = rt[:, 2:3]
    w_b = rt[:, 3:4]
    moe = jnp.concatenate([w_a * a_lo + w_b * b_lo, w_a * a_hi + w_b * b_hi], axis=1)
    x = x_ref[0] + mod_ref[0, 5:6, :] * moe
    if final:
        x = x * lax.rsqrt(jnp.mean(x * x, axis=-1, keepdims=True) + NORM_EPS) * fin_ref[...]
    o_ref[0] = x


def _combine(pos_a, pos_b, rows_seg, route, x, modr, mod_row0, final_gain, final):
    b, l, _ = x.shape
    nseg, rseg, half = rows_seg.shape
    tm = min(ROW_TILE, l)
    tiles_l = l // tm
    seg_tiles = (b * tiles_l) // nseg
    flat = lambda i, j: i * tiles_l + j
    tok = lambda w: pl.BlockSpec((1, tm, w), lambda i, j: (i, j, 0))
    smem = pl.BlockSpec((1, 1, tm), lambda i, j: (flat(i, j), 0, 0), memory_space=pltpu.SMEM)
    return pl.pallas_call(
        functools.partial(_combine_body, final=final),
        grid=(b, tiles_l),
        in_specs=[smem, smem,
                  pl.BlockSpec((1, rseg, half), lambda i, j: (flat(i, j) // seg_tiles, 0, 0)),
                  tok(ROUTE_LANES), tok(D_MODEL),
                  pl.BlockSpec((1, N_MOD, D_MODEL), lambda i, j: (mod_row0 + i, 0, 0)),
                  pl.BlockSpec((1, D_MODEL), lambda i, j: (0, 0))],
        out_specs=tok(D_MODEL),
        out_shape=jax.ShapeDtypeStruct((b, l, D_MODEL), F32),
        scratch_shapes=[pltpu.VMEM((tm, half), U32), pltpu.VMEM((tm, half), U32)],
        compiler_params=_cparams("arbitrary", "arbitrary"),
        name="moe_combine",
    )(pos_a.reshape(-1, 1, tm), pos_b.reshape(-1, 1, tm), rows_seg, route, x, modr, final_gain)


def _routing_tables(route, nseg, seg):
    t = route.shape[0]
    tr = EXPERT_TILE
    rseg = 2 * seg + N_EXPERTS * tr
    experts = route[:, 0:2].astype(jnp.int32)
    seg_of = (jnp.arange(t, dtype=jnp.int32) // seg)[:, None]
    key = (seg_of * N_EXPERTS + experts).reshape(-1)
    ngrp = nseg * N_EXPERTS
    counts = jnp.zeros((ngrp,), jnp.int32).at[key].add(1)
    padded = ((counts + tr - 1) // tr) * tr
    pad2 = padded.reshape(nseg, N_EXPERTS)
    end_local = jnp.cumsum(pad2, axis=1)
    start_pad = (end_local - pad2 + (jnp.arange(nseg, dtype=jnp.int32) * rseg)[:, None]).reshape(-1)
    start_unp = jnp.cumsum(counts) - counts
    order = jnp.argsort(key, stable=True).astype(jnp.int32)
    key_sorted = key[order]
    dest_sorted = start_pad[key_sorted] + jnp.arange(2 * t, dtype=jnp.int32) - start_unp[key_sorted]
    row_token = jnp.zeros((nseg * rseg,), jnp.int32).at[dest_sorted].set((order // 2) % seg)
    pos = jnp.zeros((2 * t,), jnp.int32).at[order].set(dest_sorted - (key_sorted // N_EXPERTS) * rseg)
    pos = pos.reshape(t, 2)
    tiles_per_seg = rseg // tr
    local_row = jnp.arange(tiles_per_seg, dtype=jnp.int32) * tr
    tile_grp = jnp.sum(local_row[None, :, None] >= end_local[:, None, :], axis=-1)
    tile_valid = (tile_grp < N_EXPERTS).astype(jnp.int32).reshape(-1)
    tile_expert = jnp.minimum(tile_grp, N_EXPERTS - 1).astype(jnp.int32).reshape(-1)
    return row_token, pos[:, 0], pos[:, 1], tile_expert, tile_valid, rseg


def _trunk(x, modr, mod_row0, nb, params):
    b, l, _ = x.shape
    t = b * l
    seg = min(MOE_SEGMENT, t)
    nseg = t // seg
    depth = params["w_in"].shape[0]
    for layer in range(depth):
        row0 = layer * nb + mod_row0
        proj = _in_proj(x, modr, row0, params["norm_mix"], params["w_in"], layer)
        o_f, o_b = _hgrn(proj, params["lbs"], layer)
        q_rot, kt = _qk_prep(proj, params["q_gain2"], params["k_gain2"], layer)
        o_at = _attention(q_rot, kt, proj)
        x_mid, hp, route = _out_proj(o_f, o_b, proj, o_at, x, modr, row0, params["hg_gain"],
                                     params["w_out"], params["norm_ffn"], params["w_route"],
                                     params["b_route"], layer)
        route2 = route.reshape(t, ROUTE_LANES)
        row_token, pos_a, pos_b, tile_e, tile_v, rseg = _routing_tables(route2, nseg, seg)
        rows = _moe_ffn(hp.reshape(nseg, seg, D_MODEL // 2), row_token, tile_e, tile_v,
                        params["w_e_gate"], params["w_e_up"], params["w_e_down"], layer)
        x = _combine(pos_a, pos_b, rows.reshape(nseg, rseg, D_MODEL // 2), route, x_mid, modr, row0,
                     params["final_norm"], final=(layer == depth - 1))
    return x


def kernel(x_prompt, x_sample, c_prompt, c_sample, w_in, w_out, hg_lb, hg_out_norm, q_norm, k_norm,
           norm_mix, norm_ffn, w_ada, b_ada, w_group, b_group, w_router, b_router, w_e_gate, w_e_up,
           w_e_down, final_norm):
    depth = w_in.shape[0]
    bp, bs = c_prompt.shape[0], c_sample.shape[0]
    nb = -(-(bp + bs) // SUBLANES) * SUBLANES
    c_all = jnp.zeros((nb, D_MODEL), F32).at[:bp].set(c_prompt).at[bp:bp + bs].set(c_sample)
    mod = _modulation(c_all, w_ada, b_ada)
    modr = mod.reshape(depth * nb, N_MOD, D_MODEL)
    pad = ROUTE_LANES - N_GROUPS - N_EXPERTS
    params = {
        "w_in": w_in.astype(BF16),
        "w_out": w_out.astype(BF16),
        "lbs": _lower_bounds(hg_lb).reshape(depth, 1, HG_WIDTH),
        "hg_gain": hg_out_norm.reshape(depth, 1, HG_DIM),
        "q_gain2": jnp.tile(q_norm, (1, LANES // HEAD_DIM)).reshape(depth, 1, LANES),
        "k_gain2": jnp.tile(k_norm, (1, LANES // HEAD_DIM)).reshape(depth, 1, LANES),
        "norm_mix": norm_mix.reshape(depth, 1, D_MODEL),
        "norm_ffn": norm_ffn.reshape(depth, 1, D_MODEL),
        "w_route": jnp.pad(jnp.concatenate([w_group, w_router], axis=-1), ((0, 0), (0, 0), (0, pad))),
        "b_route": jnp.pad(jnp.concatenate([b_group, b_router], axis=-1),
                           ((0, 0), (0, pad))).reshape(depth, 1, ROUTE_LANES),
        "w_e_gate": w_e_gate.astype(BF16).reshape(depth * N_EXPERTS, D_MODEL, D_EXPERT),
        "w_e_up": w_e_up.astype(BF16).reshape(depth * N_EXPERTS, D_MODEL, D_EXPERT),
        "w_e_down": w_e_down.astype(BF16).reshape(depth * N_EXPERTS, D_EXPERT, D_MODEL),
        "final_norm": final_norm.reshape(1, D_MODEL),
    }
    y_prompt = _trunk(x_prompt, modr, 0, nb, params)
    y_sample = _trunk(x_sample, modr, bp, nb, params)
    return (y_prompt, y_sample)
```

```python
import functools

import numpy as np
import jax
import jax.numpy as jnp
from jax import lax
from jax.experimental import pallas as pl
from jax.experimental.pallas import tpu as pltpu

F32 = jnp.float32
BF16 = jnp.bfloat16

D_MODEL = 1024
HG_WIDTH = 512
HG_HEADS = 4
HG_DIM = 128
LB_FLOOR = 1e-30
ATTN_WIDTH = 512
HEAD_DIM = 64
N_Q_HEADS = 8
N_KV_HEADS = 2
GQA_GROUP = 4
KV_WIDTH = 128
GRID_W = 64
ROPE_AXIS_DIM = 32
ROPE_THETA = 10000.0
N_GROUPS = 4
EXPERTS_PER_GROUP = 4
N_EXPERTS = 16
D_EXPERT = 512
N_MOD = 6
NORM_EPS = 1e-6
IN_PROJ_WIDTH = 5 * HG_WIDTH + ATTN_WIDTH + 2 * KV_WIDTH
COL_HQ, COL_HF_FWD, COL_HF_BWD, COL_HI, COL_HG, COL_AQ = 0, 1, 2, 3, 4, 5
COL_AK, COL_AV = 24, 25

LANES = 128
SUBLANES = 8
VMEM_LIMIT = 56 * 1024 * 1024
ROW_TILE = 512
HG_CHUNK = 128
KV_BLOCK = 512
Q_TILE = 128
EXPERT_TILE = 128
COMBINE_TILE = 256
MOE_SEGMENT = 4096
ROUTE_LANES = 128
LOG2E = 1.4426950408889634
SAFE_SCORE_BOUND = 40.0


def _cparams(*sem):
    return pltpu.CompilerParams(dimension_semantics=sem, vmem_limit_bytes=VMEM_LIMIT)


def _dot(a, b):
    return jnp.dot(a, b, preferred_element_type=F32)


def _dot_nt(a, b):
    return lax.dot_general(a, b, (((1,), (1,)), ((), ())), preferred_element_type=F32)


def _split_bf16(x):
    hi = x.astype(BF16)
    lo = (x - hi.astype(F32)).astype(BF16)
    return hi, lo


def _sigmoid(x):
    return 1.0 / (1.0 + jnp.exp(-x))


def _mod_body(c_ref, w_ref, b_ref, o_ref):
    c = c_ref[...]
    cs = c * _sigmoid(c)
    o_ref[0] = _dot(cs.astype(BF16), w_ref[0].astype(BF16)) + b_ref[0]


def _modulation(c_all, w_ada, b_ada):
    nb = c_all.shape[0]
    depth, _, n = w_ada.shape
    tn = 1024
    return pl.pallas_call(
        _mod_body,
        grid=(depth, n // tn),
        in_specs=[pl.BlockSpec((nb, D_MODEL), lambda l, j: (0, 0)),
                  pl.BlockSpec((1, D_MODEL, tn), lambda l, j: (l, 0, j)),
                  pl.BlockSpec((1, 1, tn), lambda l, j: (l, 0, j))],
        out_specs=pl.BlockSpec((1, nb, tn), lambda l, j: (l, 0, j)),
        out_shape=jax.ShapeDtypeStruct((depth, nb, n), F32),
        compiler_params=_cparams("arbitrary", "arbitrary"),
        name="adaln_modulation",
    )(c_all, w_ada, b_ada.reshape(depth, 1, n))


def _lb_body(p_ref, o_ref):
    depth = p_ref.shape[0]
    rows = [p_ref[l:l + 1, :] for l in range(depth)]
    m = functools.reduce(jnp.maximum, rows)
    es = [jnp.exp(r - m) for r in rows]
    tot = functools.reduce(lambda a, b: a + b, es)
    sm = [e / tot for e in es]
    acc = jnp.zeros_like(sm[0])
    for l in range(depth):
        acc = acc + sm[l]
        o_ref[l:l + 1, :] = acc - sm[0]


def _lower_bounds(hg_lb):
    return pl.pallas_call(
        _lb_body,
        out_shape=jax.ShapeDtypeStruct(hg_lb.shape, F32),
        name="hgrn_lower_bounds",
    )(hg_lb)


def _inproj_body(x_ref, mod_ref, gain_ref, w_ref, p_ref):
    x = x_ref[0]
    ms = jnp.mean(x * x, axis=-1, keepdims=True)
    y = x * lax.rsqrt(ms + NORM_EPS) * gain_ref[0]
    h = y * (1.0 + mod_ref[0, 1:2, :]) + mod_ref[0, 0:1, :]
    p_ref[0] = _dot(h.astype(BF16), w_ref[0]).astype(BF16)


def _in_proj(x, modr, mod_row0, gain, w_in_bf, layer):
    b, l, _ = x.shape
    tm = min(ROW_TILE, l)
    return pl.pallas_call(
        _inproj_body,
        grid=(b, l // tm),
        in_specs=[pl.BlockSpec((1, tm, D_MODEL), lambda i, j: (i, j, 0)),
                  pl.BlockSpec((1, N_MOD, D_MODEL), lambda i, j: (mod_row0 + i, 0, 0)),
                  pl.BlockSpec((1, 1, D_MODEL), lambda i, j: (layer, 0, 0)),
                  pl.BlockSpec((1, D_MODEL, IN_PROJ_WIDTH), lambda i, j: (layer, 0, 0))],
        out_specs=pl.BlockSpec((1, tm, IN_PROJ_WIDTH), lambda i, j: (i, j, 0)),
        out_shape=jax.ShapeDtypeStruct((b, l, IN_PROJ_WIDTH), BF16),
        compiler_params=_cparams("arbitrary", "arbitrary"),
        name="in_proj",
    )(x, modr, gain, w_in_bf)


HG_LEVELS = (64, 32, 16, 8, 4, 2, 1)


def _hg_constants():
    c = HG_CHUNK
    t = np.arange(c)[:, None]
    s = np.arange(c)[None, :]
    masks = np.zeros((2, len(HG_LEVELS) + 1, c, c), np.float32)
    for li, h in enumerate(HG_LEVELS):
        same = (t // (2 * h)) == (s // (2 * h))
        masks[0, li] = same & ((t % (2 * h)) >= h) & ((s % (2 * h)) < h)
        masks[1, li] = same & ((t % (2 * h)) < h) & ((s % (2 * h)) >= h)
    masks[:, len(HG_LEVELS)] = (t == s)
    cum = np.stack([(s <= t), (s >= t)]).astype(np.float32)
    return jnp.asarray(masks), jnp.asarray(cum, dtype=BF16)


def _row_bcast(a, rows, blk):
    parts = [jnp.broadcast_to(a[r:r + 1, :], (blk, a.shape[1])) for r in rows]
    return parts[0] if len(parts) == 1 else jnp.concatenate(parts, axis=0)


def _level_reference(cum, h, reverse, sub):
    c = cum.shape[0]
    if 2 * h >= SUBLANES:
        rows = [p0 + (h if reverse else h - 1) for p0 in range(0, c, 2 * h)]
        return _row_bcast(cum, rows, 2 * h)
    out = None
    for p0 in reversed(range(0, SUBLANES, 2 * h)):
        r = p0 + (h if reverse else h - 1)
        piece = _row_bcast(cum, [v0 + r for v0 in range(0, c, SUBLANES)], SUBLANES)
        out = piece if out is None else jnp.where(sub < p0 + 2 * h, piece, out)
    return out


def _hg_direction(hq, hf, hi, lb, cum_mat, masks_ref, d, st_ref, head, reverse):
    c = HG_CHUNK
    q = hq * _sigmoid(hq)
    sig = _sigmoid(hf)
    one_m_lb = 1.0 - lb
    g = jnp.log(jnp.maximum(lb, LB_FLOOR) + one_m_lb * sig)
    kk = one_m_lb * (1.0 - sig)
    g_hi, g_lo = _split_bf16(g)
    cum = _dot(cum_mat, g_hi) + _dot(cum_mat, g_lo)
    sub = lax.broadcasted_iota(jnp.int32, (c, HG_DIM), 0) % SUBLANES

    a = jnp.zeros((c, c), F32)
    for li, h in enumerate(HG_LEVELS):
        ref = _level_reference(cum, h, reverse, sub)
        e = jnp.exp2(jnp.abs(cum - ref) * (-LOG2E))
        al = _dot_nt((q * e).astype(BF16), (kk * e).astype(BF16))
        a = a + al * masks_ref[d, li]
    diag = jnp.sum(q * kk, axis=-1, keepdims=True)
    a = a + diag * masks_ref[d, len(HG_LEVELS)]

    st = st_ref[d, head]
    tot = cum[0:1, :] if reverse else cum[c - 1:c, :]
    o = _dot(a.astype(BF16), hi.astype(BF16))
    o = o + _dot_nt((q * jnp.exp(cum)).astype(BF16), st.astype(BF16))
    k_end = (kk * jnp.exp(tot - cum)).astype(BF16)
    st_ref[d, head] = jnp.exp(tot) * st + _dot(hi.T.astype(BF16), k_end)
    return o


def _hgrn_body(qf_ref, ff_ref, if_ref, qb_ref, fb_ref, ib_ref, lb_ref, cum_ref, masks_ref,
               of_ref, ob_ref, st_ref):
    @pl.when(pl.program_id(1) == 0)
    def _():
        st_ref[...] = jnp.zeros_like(st_ref)

    for head in range(HG_HEADS):
        sl = slice(head * HG_DIM, (head + 1) * HG_DIM)
        lb = lb_ref[0, :, sl]
        for d, (q_ref, f_ref, i_ref, o_ref) in enumerate(((qf_ref, ff_ref, if_ref, of_ref),
                                                          (qb_ref, fb_ref, ib_ref, ob_ref))):
            o = _hg_direction(q_ref[0, :, sl].astype(F32), f_ref[0, :, sl].astype(F32),
                              i_ref[0, :, sl].astype(F32), lb, cum_ref[d], masks_ref, d, st_ref,
                              head, reverse=(d == 1))
            o_ref[0, :, sl] = o.astype(BF16)


def _hgrn(proj, lbs, layer):
    b, l, _ = proj.shape
    c = HG_CHUNK
    nc = l // c
    masks, cum = _hg_constants()
    fwd = lambda col: pl.BlockSpec((1, c, HG_WIDTH), lambda i, j: (i, j, col))
    bwd = lambda col: pl.BlockSpec((1, c, HG_WIDTH), lambda i, j: (i, nc - 1 - j, col))
    out_sd = jax.ShapeDtypeStruct((b, l, HG_WIDTH), BF16)
    return pl.pallas_call(
        _hgrn_body,
        grid=(b, nc),
        in_specs=[fwd(COL_HQ), fwd(COL_HF_FWD), fwd(COL_HI),
                  bwd(COL_HQ), bwd(COL_HF_BWD), bwd(COL_HI),
                  pl.BlockSpec((1, 1, HG_WIDTH), lambda i, j: (layer, 0, 0)),
                  pl.BlockSpec(cum.shape, lambda i, j: (0, 0, 0)),
                  pl.BlockSpec(masks.shape, lambda i, j: (0, 0, 0, 0))],
        out_specs=[pl.BlockSpec((1, c, HG_WIDTH), lambda i, j: (i, j, 0)),
                   pl.BlockSpec((1, c, HG_WIDTH), lambda i, j: (i, nc - 1 - j, 0))],
        out_shape=[out_sd, out_sd],
        scratch_shapes=[pltpu.VMEM((2, HG_HEADS, HG_DIM, HG_DIM), F32)],
        compiler_params=_cparams("arbitrary", "arbitrary"),
        name="hgrn2_recurrence",
    )(proj, proj, proj, proj, proj, proj, lbs, cum, masks)


def _rope_tables(l):
    lane = np.arange(LANES)
    dd = lane % HEAD_DIM
    axis = dd // ROPE_AXIS_DIM
    first_half = (dd % ROPE_AXIS_DIM) < (ROPE_AXIS_DIM // 2)
    freq_idx = dd % (ROPE_AXIS_DIM // 2)
    inv_freq = ROPE_THETA ** (-jnp.arange(0, ROPE_AXIS_DIM, 2, dtype=F32) / ROPE_AXIS_DIM)
    t = jnp.arange(l)
    pos = jnp.where(jnp.asarray(axis)[None, :] == 0, (t // GRID_W)[:, None], (t % GRID_W)[:, None])
    ang = pos.astype(F32) * inv_freq[jnp.asarray(freq_idx)][None, :]
    sign = jnp.where(jnp.asarray(first_half), -1.0, 1.0)[None, :]
    return jnp.cos(ang), jnp.sin(ang) * sign


def _norm_rope(x, gain, cos_t, sin_t, bd, first_half):
    x2_hi, x2_lo = _split_bf16(x * x)
    ss = _dot(x2_hi, bd) + _dot(x2_lo, bd)
    xn = x * lax.rsqrt(ss * (1.0 / HEAD_DIM) + NORM_EPS) * gain
    half = ROPE_AXIS_DIM // 2
    partner = jnp.where(first_half, pltpu.roll(xn, LANES - half, 1), pltpu.roll(xn, half, 1))
    return xn * cos_t + partner * sin_t


def _qkprep_body(q_ref, k_ref, v_ref, qg_ref, kg_ref, cos_ref, sin_ref, bd_ref, qt_ref, ko_ref, vt_ref):
    tq = qt_ref.shape[4] // GQA_GROUP
    cos_t = cos_ref[...]
    sin_t = sin_ref[...]
    bd = bd_ref[...]
    lane = lax.broadcasted_iota(jnp.int32, cos_t.shape, 1)
    first_half = (lane % ROPE_AXIS_DIM) < (ROPE_AXIS_DIM // 2)
    zeros = jnp.zeros((HEAD_DIM, tq), BF16)
    for j in range(ATTN_WIDTH // LANES):
        sl = slice(j * LANES, (j + 1) * LANES)
        qr = _norm_rope(q_ref[0, :, sl].astype(F32), qg_ref[0], cos_t, sin_t, bd, first_half)
        qr = qr * (HEAD_DIM ** -0.5 * LOG2E)
        h = (2 * j) // GQA_GROUP
        for u in range(qt_ref.shape[2]):
            t = qr[u * tq:(u + 1) * tq, :].T.astype(BF16)
            for e in range(2):
                piece = t[e * HEAD_DIM:(e + 1) * HEAD_DIM, :]
                blockcol = jnp.concatenate([piece, zeros] if h == 0 else [zeros, piece], axis=0)
                g = (2 * j + e) % GQA_GROUP
                qt_ref[0, h, u, :, g * tq:(g + 1) * tq] = blockcol
    kr = _norm_rope(k_ref[0].astype(F32), kg_ref[0], cos_t, sin_t, bd, first_half)
    ko_ref[0] = kr.astype(BF16)
    vt_ref[0, 0] = v_ref[0].astype(F32).T.astype(BF16)


def _qk_prep(proj, q_gain2, k_gain2, layer):
    b, l, _ = proj.shape
    tk = min(KV_BLOCK, l)
    tq = min(Q_TILE, l)
    cos_t, sin_t = _rope_tables(l)
    blk = np.arange(LANES) // HEAD_DIM
    bd = jnp.asarray((blk[:, None] == blk[None, :]).astype(np.float32), dtype=BF16)
    return pl.pallas_call(
        _qkprep_body,
        grid=(b, l // tk),
        in_specs=[pl.BlockSpec((1, tk, ATTN_WIDTH), lambda i, j: (i, j, COL_AQ)),
                  pl.BlockSpec((1, tk, KV_WIDTH), lambda i, j: (i, j, COL_AK)),
                  pl.BlockSpec((1, tk, KV_WIDTH), lambda i, j: (i, j, COL_AV)),
                  pl.BlockSpec((1, 1, LANES), lambda i, j: (layer, 0, 0)),
                  pl.BlockSpec((1, 1, LANES), lambda i, j: (layer, 0, 0)),
                  pl.BlockSpec((tk, LANES), lambda i, j: (j, 0)),
                  pl.BlockSpec((tk, LANES), lambda i, j: (j, 0)),
                  pl.BlockSpec((LANES, LANES), lambda i, j: (0, 0))],
        out_specs=[pl.BlockSpec((1, N_KV_HEADS, tk // tq, KV_WIDTH, GQA_GROUP * tq),
                                lambda i, j: (i, 0, j, 0, 0)),
                   pl.BlockSpec((1, tk, KV_WIDTH), lambda i, j: (i, j, 0)),
                   pl.BlockSpec((1, 1, KV_WIDTH, tk), lambda i, j: (i, j, 0, 0))],
        out_shape=[jax.ShapeDtypeStruct((b, N_KV_HEADS, l // tq, KV_WIDTH, GQA_GROUP * tq), BF16),
                   jax.ShapeDtypeStruct((b, l, KV_WIDTH), BF16),
                   jax.ShapeDtypeStruct((b, l // tk, KV_WIDTH, tk), BF16)],
        compiler_params=_cparams("arbitrary", "arbitrary"),
        name="qk_norm_rope",
    )(proj, proj, proj, q_gain2, k_gain2, cos_t, sin_t, bd)


def _attn_finish(acc_scr, l_fin, o_ref):
    tq = o_ref.shape[1]
    outs = []
    for h in range(N_KV_HEADS):
        o = (acc_scr[h] / l_fin[h]).T
        oh = o[:, h * HEAD_DIM:(h + 1) * HEAD_DIM]
        outs += [oh[g * tq:(g + 1) * tq, :] for g in range(GQA_GROUP)]
    o_ref[0] = jnp.concatenate(outs, axis=1).astype(BF16)


def _attn_bounded_body(bound_ref, qt_ref, k_ref, vt_ref, o_ref, acc_scr):
    nblk = vt_ref.shape[1]
    tk = vt_ref.shape[3]
    cols = qt_ref.shape[4]
    shift = bound_ref[0, 0]
    acc_scr[...] = jnp.zeros_like(acc_scr)

    def step(i, sums):
        kb = k_ref[0, pl.ds(pl.multiple_of(i * tk, tk), tk), :]
        vt = vt_ref[0, i]
        scores = [_dot(kb, qt_ref[0, h, 0]) for h in range(N_KV_HEADS)]
        new = []
        for h in range(N_KV_HEADS):
            p = jnp.exp2(scores[h] - shift)
            new.append(sums[h] + jnp.sum(p, axis=0, keepdims=True))
            acc_scr[h] = acc_scr[h] + _dot(vt, p.astype(BF16))
        return tuple(new)

    init = tuple(jnp.zeros((1, cols), F32) for _ in range(N_KV_HEADS))
    _attn_finish(acc_scr, lax.fori_loop(0, nblk, step, init, unroll=4), o_ref)


def _attn_online_body(bound_ref, qt_ref, k_ref, vt_ref, o_ref, acc_scr):
    nblk = vt_ref.shape[1]
    tk = vt_ref.shape[3]
    cols = qt_ref.shape[4]
    acc_scr[...] = jnp.zeros_like(acc_scr)
    l_fin = []
    for h in range(N_KV_HEADS):
        qt = qt_ref[0, h, 0]

        def step(i, carry):
            m_prev, l_prev = carry
            kb = k_ref[0, pl.ds(pl.multiple_of(i * tk, tk), tk), :]
            s = _dot(kb, qt)
            m_new = jnp.maximum(m_prev, jnp.max(s, axis=0, keepdims=True))
            alpha = jnp.exp2(m_prev - m_new)
            p = jnp.exp2(s - m_new)
            l_new = alpha * l_prev + jnp.sum(p, axis=0, keepdims=True)
            acc_scr[h] = alpha * acc_scr[h] + _dot(vt_ref[0, i], p.astype(BF16))
            return m_new, l_new

        init = (jnp.full((1, cols), -jnp.inf, F32), jnp.zeros((1, cols), F32))
        l_fin.append(lax.fori_loop(0, nblk, step, init)[1])
    _attn_finish(acc_scr, l_fin, o_ref)


def _attention(qt, k_rot, vt, bound):
    b, l, _ = k_rot.shape
    tq = qt.shape[4] // GQA_GROUP
    nblk, tk = vt.shape[1], vt.shape[3]

    def call(body):
        return pl.pallas_call(
            body,
            grid=(b, l // tq),
            in_specs=[pl.BlockSpec(memory_space=pltpu.SMEM),
                      pl.BlockSpec((1, N_KV_HEADS, 1, KV_WIDTH, GQA_GROUP * tq), lambda i, j: (i, 0, j, 0, 0)),
                      pl.BlockSpec((1, l, KV_WIDTH), lambda i, j: (i, 0, 0)),
                      pl.BlockSpec((1, nblk, KV_WIDTH, tk), lambda i, j: (i, 0, 0, 0))],
            out_specs=pl.BlockSpec((1, tq, ATTN_WIDTH), lambda i, j: (i, j, 0)),
            out_shape=jax.ShapeDtypeStruct((b, l, ATTN_WIDTH), BF16),
            scratch_shapes=[pltpu.VMEM((N_KV_HEADS, KV_WIDTH, GQA_GROUP * tq), F32)],
            compiler_params=_cparams("arbitrary", "arbitrary"),
            name="gqa_attention",
        )(bound, qt, k_rot, vt)

    return lax.cond(bound[0, 0] <= SAFE_SCORE_BOUND,
                    lambda: call(_attn_bounded_body), lambda: call(_attn_online_body))


def _route(logits):
    lane = lax.broadcasted_iota(jnp.int32, logits.shape, 1).astype(F32)
    neg = jnp.float32(-jnp.inf)
    lanemin = lambda cond: jnp.min(jnp.where(cond, lane, float(ROUTE_LANES)), axis=-1, keepdims=True)
    gl = jnp.where(lane < N_GROUPS, logits, neg)
    gmax = jnp.max(gl, axis=-1, keepdims=True)
    g_idx = lanemin(gl == gmax)
    pg_top = 1.0 / jnp.sum(jnp.exp(gl - gmax), axis=-1, keepdims=True)
    first = N_GROUPS + g_idx * EXPERTS_PER_GROUP
    in_group = (lane >= first) & (lane < first + EXPERTS_PER_GROUP)
    el = jnp.where(in_group, logits, neg)
    a_max = jnp.max(el, axis=-1, keepdims=True)
    a_idx = lanemin(el == a_max)
    el2 = jnp.where(lane == a_idx, neg, el)
    b_max = jnp.max(el2, axis=-1, keepdims=True)
    b_idx = lanemin(el2 == b_max)
    r = jnp.exp(b_max - a_max)
    w_a = pg_top / (1.0 + r)
    w_b = pg_top * r / (1.0 + r)
    return a_idx - N_GROUPS, b_idx - N_GROUPS, w_a, w_b


def _outproj_body(of_ref, ob_ref, hg_ref, at_ref, x_ref, mod_ref, hgain_ref, wout_ref, fgain_ref,
                  wr_ref, br_ref, tri_ref, xo_ref, h_ref, rt_ref, cnt_ref, cnt_scr, *, seg):
    tm = x_ref.shape[1]
    tile = pl.program_id(0) * pl.num_programs(1) + pl.program_id(1)

    @pl.when(tile == 0)
    def _():
        cnt_scr[...] = jnp.zeros_like(cnt_scr)

    parts = []
    for head in range(HG_HEADS):
        sl = slice(head * HG_DIM, (head + 1) * HG_DIM)
        o = of_ref[0, :, sl].astype(F32) + ob_ref[0, :, sl].astype(F32)
        o = o * lax.rsqrt(jnp.mean(o * o, axis=-1, keepdims=True) + NORM_EPS) * hgain_ref[0]
        gt = hg_ref[0, :, sl].astype(F32)
        parts.append((o * (gt * _sigmoid(gt))).astype(BF16))
    o_hg = jnp.concatenate(parts, axis=1)
    mix = _dot(o_hg, wout_ref[0, :HG_WIDTH, :]) + _dot(at_ref[0], wout_ref[0, HG_WIDTH:, :])
    x = x_ref[0] + mod_ref[0, 2:3, :] * mix
    xo_ref[0] = x
    y = x * lax.rsqrt(jnp.mean(x * x, axis=-1, keepdims=True) + NORM_EPS) * fgain_ref[0]
    h = y * (1.0 + mod_ref[0, 4:5, :]) + mod_ref[0, 3:4, :]
    h_ref[0] = h
    h_hi, h_lo = _split_bf16(h)
    w_hi, w_lo = _split_bf16(wr_ref[0])
    logits = _dot(h_hi, w_hi) + _dot(h_hi, w_lo) + _dot(h_lo, w_hi) + br_ref[0]
    e_a, e_b, w_a, w_b = _route(logits)

    lane = lax.broadcasted_iota(jnp.int32, logits.shape, 1).astype(F32)
    group0 = (((tile * tm) // seg) * N_EXPERTS).astype(F32)
    hot_a = jnp.where(lane == e_a + group0, 1.0, 0.0)
    hot_b = jnp.where(lane == e_b + group0, 1.0, 0.0)
    hot = hot_a + hot_b
    before = cnt_scr[...] + _dot(tri_ref[...], hot.astype(BF16))
    rank_a = jnp.sum(hot_a * before, axis=-1, keepdims=True)
    rank_b = jnp.sum(hot_b * before, axis=-1, keepdims=True)
    cnt_scr[...] = cnt_scr[...] + jnp.sum(hot, axis=0, keepdims=True)
    cnt_ref[...] = cnt_scr[...]
    cols = (e_a, e_b, w_a, w_b, rank_a, rank_b)
    out = jnp.zeros_like(logits)
    for c, v in enumerate(cols):
        out = jnp.where(lane == float(c), v, out)
    rt_ref[0] = out


def _out_proj(o_f, o_b, proj, o_at, x, modr, mod_row0, hgain, w_out_bf, fgain, w_route, b_route, layer,
              seg):
    b, l, _ = x.shape
    tm = min(ROW_TILE, l)
    tok = lambda w, col=0: pl.BlockSpec((1, tm, w), lambda i, j: (i, j, col))
    lay = lambda *s: pl.BlockSpec((1,) + s, lambda i, j: (layer,) + (0,) * len(s))
    tri = jnp.asarray(np.tril(np.ones((tm, tm), np.float32), -1), dtype=BF16)
    return pl.pallas_call(
        functools.partial(_outproj_body, seg=seg),
        grid=(b, l // tm),
        in_specs=[tok(HG_WIDTH), tok(HG_WIDTH), tok(HG_WIDTH, COL_HG), tok(ATTN_WIDTH), tok(D_MODEL),
                  pl.BlockSpec((1, N_MOD, D_MODEL), lambda i, j: (mod_row0 + i, 0, 0)),
                  lay(1, HG_DIM), lay(D_MODEL, D_MODEL), lay(1, D_MODEL),
                  lay(D_MODEL, ROUTE_LANES), lay(1, ROUTE_LANES),
                  pl.BlockSpec((tm, tm), lambda i, j: (0, 0))],
        out_specs=[tok(D_MODEL), tok(D_MODEL), tok(ROUTE_LANES),
                   pl.BlockSpec((1, ROUTE_LANES), lambda i, j: (0, 0))],
        out_shape=[jax.ShapeDtypeStruct((b, l, D_MODEL), F32),
                   jax.ShapeDtypeStruct((b, l, D_MODEL), F32),
                   jax.ShapeDtypeStruct((b, l, ROUTE_LANES), F32),
                   jax.ShapeDtypeStruct((1, ROUTE_LANES), F32)],
        scratch_shapes=[pltpu.VMEM((1, ROUTE_LANES), F32)],
        compiler_params=_cparams("arbitrary", "arbitrary"),
        name="out_proj_router",
    )(o_f, o_b, proj, o_at, x, modr, hgain, w_out_bf, fgain, w_route, b_route, tri)


def _moe_body(te_ref, tv_ref, tu_ref, tok_ref, h_ref, wg_ref, wu_ref, wd_ref, o_ref, xs_scr):
    i = pl.program_id(0)
    tr = o_ref.shape[0]

    @pl.when(tv_ref[i] != 0)
    def _():
        first = tu_ref[i]

        def gather(r, carry):
            xs_scr[pl.ds(r, 1), :] = h_ref[0, pl.ds(tok_ref[first + r], 1), :]
            return carry

        lax.fori_loop(0, tr, gather, 0, unroll=8)
        xs = xs_scr[...].astype(BF16)
        gt = _dot(xs, wg_ref[0])
        up = _dot(xs, wu_ref[0])
        hid = (gt * _sigmoid(gt)) * up
        o_ref[...] = _dot(hid.astype(BF16), wd_ref[0])

    @pl.when(tv_ref[i] == 0)
    def _():
        o_ref[...] = jnp.zeros_like(o_ref)


def _moe_ffn(h_seg, sorted_tok, tile_expert, tile_valid, tile_first, wg, wu, wd, layer):
    nseg, seg, d = h_seg.shape
    n_tiles = tile_expert.shape[0]
    tr = EXPERT_TILE
    tiles_per_seg = n_tiles // nseg
    wspec = lambda k, n: pl.BlockSpec((1, k, n), lambda i, te, *_: (layer * N_EXPERTS + te[i], 0, 0))
    return pl.pallas_call(
        _moe_body,
        grid_spec=pltpu.PrefetchScalarGridSpec(
            num_scalar_prefetch=4,
            grid=(n_tiles,),
            in_specs=[pl.BlockSpec((1, seg, d), lambda i, *_: (i // tiles_per_seg, 0, 0)),
                      wspec(D_MODEL, D_EXPERT), wspec(D_MODEL, D_EXPERT), wspec(D_EXPERT, D_MODEL)],
            out_specs=pl.BlockSpec((tr, d), lambda i, *_: (i, 0)),
            scratch_shapes=[pltpu.VMEM((tr, d), F32)]),
        out_shape=jax.ShapeDtypeStruct((n_tiles * tr, d), F32),
        compiler_params=_cparams("arbitrary"),
        name="routed_expert_ffn",
    )(tile_expert, tile_valid, tile_first, sorted_tok, h_seg, wg, wu, wd)


def _combine_body(pa_ref, pb_ref, rows_ref, rt_ref, x_ref, mod_ref, fin_ref, o_ref, a_scr, b_scr, *,
                  final):
    tm = o_ref.shape[1]

    def gather(r, carry):
        a_scr[pl.ds(r, 1), :] = rows_ref[0, pl.ds(pa_ref[0, 0, r], 1), :]
        b_scr[pl.ds(r, 1), :] = rows_ref[0, pl.ds(pb_ref[0, 0, r], 1), :]
        return carry

    lax.fori_loop(0, tm, gather, 0, unroll=8)
    rt = rt_ref[0]
    moe = rt[:, 2:3] * a_scr[...] + rt[:, 3:4] * b_scr[...]
    x = x_ref[0] + mod_ref[0, 5:6, :] * moe
    if final:
        x = x * lax.rsqrt(jnp.mean(x * x, axis=-1, keepdims=True) + NORM_EPS) * fin_ref[...]
    o_ref[0] = x


def _combine(pos_a, pos_b, rows_seg, route, x, modr, mod_row0, final_gain, final):
    b, l, _ = x.shape
    nseg, rseg, d = rows_seg.shape
    tm = min(COMBINE_TILE, l)
    tiles_l = l // tm
    seg_tiles = (b * tiles_l) // nseg
    flat = lambda i, j: i * tiles_l + j
    tok = lambda w: pl.BlockSpec((1, tm, w), lambda i, j: (i, j, 0))
    smem = pl.BlockSpec((1, 1, tm), lambda i, j: (flat(i, j), 0, 0), memory_space=pltpu.SMEM)
    return pl.pallas_call(
        functools.partial(_combine_body, final=final),
        grid=(b, tiles_l),
        in_specs=[smem, smem,
                  pl.BlockSpec((1, rseg, d), lambda i, j: (flat(i, j) // seg_tiles, 0, 0),
                               pipeline_mode=pl.Buffered(1)),
                  tok(ROUTE_LANES), tok(D_MODEL),
                  pl.BlockSpec((1, N_MOD, D_MODEL), lambda i, j: (mod_row0 + i, 0, 0)),
                  pl.BlockSpec((1, D_MODEL), lambda i, j: (0, 0))],
        out_specs=tok(D_MODEL),
        out_shape=jax.ShapeDtypeStruct((b, l, D_MODEL), F32),
        scratch_shapes=[pltpu.VMEM((tm, d), F32), pltpu.VMEM((tm, d), F32)],
        compiler_params=_cparams("arbitrary", "arbitrary"),
        name="moe_combine",
    )(pos_a.reshape(-1, 1, tm), pos_b.reshape(-1, 1, tm), rows_seg, route, x, modr, final_gain)


def _routing_tables(route, counts, nseg, seg):
    t = route.shape[0]
    tr = EXPERT_TILE
    rseg = 2 * seg + N_EXPERTS * tr
    ngrp = nseg * N_EXPERTS
    cnt = counts[0, :ngrp].astype(jnp.int32)
    pad2 = (((cnt + tr - 1) // tr) * tr).reshape(nseg, N_EXPERTS)
    end_local = jnp.cumsum(pad2, axis=1)
    start_local = (end_local - pad2).reshape(-1)
    start_unp = jnp.cumsum(cnt) - cnt
    ids = route[:, 0:2].astype(jnp.int32)
    rank = route[:, 4:6].astype(jnp.int32)
    key = (jnp.arange(t, dtype=jnp.int32) // seg)[:, None] * N_EXPERTS + ids
    hot = key[:, :, None] == jnp.arange(ngrp, dtype=jnp.int32)[None, None, :]
    lookup = lambda table: jnp.sum(jnp.where(hot, table[None, None, :], 0), axis=-1)
    pos = lookup(start_local) + rank
    order = lookup(start_unp) + rank
    tok = jnp.broadcast_to((jnp.arange(t, dtype=jnp.int32) % seg)[:, None], (t, 2))
    _, sorted_tok = lax.sort((order.reshape(-1), tok.reshape(-1)), num_keys=1)
    sorted_tok = jnp.concatenate([sorted_tok, jnp.zeros((tr,), jnp.int32)])
    tiles_per_seg = rseg // tr
    local_row = jnp.arange(tiles_per_seg, dtype=jnp.int32) * tr
    tile_grp = jnp.sum(local_row[None, :, None] >= end_local[:, None, :], axis=-1)
    tile_valid = (tile_grp < N_EXPERTS).astype(jnp.int32)
    tile_expert = jnp.minimum(tile_grp, N_EXPERTS - 1).astype(jnp.int32)
    grp = tile_expert + (jnp.arange(nseg, dtype=jnp.int32) * N_EXPERTS)[:, None]
    hot_t = grp[:, :, None] == jnp.arange(ngrp, dtype=jnp.int32)[None, None, :]
    skipped_pad = jnp.sum(jnp.where(hot_t, (start_local - start_unp)[None, None, :], 0), axis=-1)
    tile_first = (local_row[None, :] - skipped_pad) * tile_valid
    return (sorted_tok, pos[:, 0], pos[:, 1], tile_expert.reshape(-1), tile_valid.reshape(-1),
            tile_first.reshape(-1), rseg)


def _trunk(x, modr, mod_row0, nb, params):
    b, l, _ = x.shape
    t = b * l
    seg = min(MOE_SEGMENT, t)
    nseg = t // seg
    depth = params["w_in"].shape[0]
    for layer in range(depth):
        row0 = layer * nb + mod_row0
        proj = _in_proj(x, modr, row0, params["norm_mix"], params["w_in"], layer)
        o_f, o_b = _hgrn(proj, params["lbs"], layer)
        qt, k_rot, vt = _qk_prep(proj, params["q_gain2"], params["k_gain2"], layer)
        o_at = _attention(qt, k_rot, vt, params["score_bound"][layer])
        x_mid, h, route, counts = _out_proj(o_f, o_b, proj, o_at, x, modr, row0, params["hg_gain"],
                                            params["w_out"], params["norm_ffn"], params["w_route"],
                                            params["b_route"], layer, seg)
        sorted_tok, pos_a, pos_b, tile_e, tile_v, tile_f, rseg = _routing_tables(
            route.reshape(t, ROUTE_LANES), counts, nseg, seg)
        rows = _moe_ffn(h.reshape(nseg, seg, D_MODEL), sorted_tok, tile_e, tile_v, tile_f,
                        params["w_e_gate"], params["w_e_up"], params["w_e_down"], layer)
        x = _combine(pos_a, pos_b, rows.reshape(nseg, rseg, D_MODEL), route, x_mid, modr, row0,
                     params["final_norm"], final=(layer == depth - 1))
    return x


def kernel(x_prompt, x_sample, c_prompt, c_sample, w_in, w_out, hg_lb, hg_out_norm, q_norm, k_norm,
           norm_mix, norm_ffn, w_ada, b_ada, w_group, b_group, w_router, b_router, w_e_gate, w_e_up,
           w_e_down, final_norm):
    depth = w_in.shape[0]
    bp, bs = c_prompt.shape[0], c_sample.shape[0]
    nb = -(-(bp + bs) // SUBLANES) * SUBLANES
    c_all = jnp.zeros((nb, D_MODEL), F32).at[:bp].set(c_prompt).at[bp:bp + bs].set(c_sample)
    mod = _modulation(c_all, w_ada, b_ada)
    modr = mod.reshape(depth * nb, N_MOD, D_MODEL)
    pad = ROUTE_LANES - N_GROUPS - N_EXPERTS
    params = {
        "w_in": w_in.astype(BF16),
        "w_out": w_out.astype(BF16),
        "lbs": _lower_bounds(hg_lb).reshape(depth, 1, HG_WIDTH),
        "hg_gain": hg_out_norm.reshape(depth, 1, HG_DIM),
        "q_gain2": jnp.tile(q_norm, (1, LANES // HEAD_DIM)).reshape(depth, 1, LANES),
        "k_gain2": jnp.tile(k_norm, (1, LANES // HEAD_DIM)).reshape(depth, 1, LANES),
        "score_bound": (1.02 * LOG2E * HEAD_DIM ** 0.5 * jnp.max(jnp.abs(q_norm), axis=1)
                        * jnp.max(jnp.abs(k_norm), axis=1)).reshape(depth, 1, 1),
        "norm_mix": norm_mix.reshape(depth, 1, D_MODEL),
        "norm_ffn": norm_ffn.reshape(depth, 1, D_MODEL),
        "w_route": jnp.pad(jnp.concatenate([w_group, w_router], axis=-1), ((0, 0), (0, 0), (0, pad))),
        "b_route": jnp.pad(jnp.concatenate([b_group, b_router], axis=-1),
                           ((0, 0), (0, pad))).reshape(depth, 1, ROUTE_LANES),
        "w_e_gate": w_e_gate.astype(BF16).reshape(depth * N_EXPERTS, D_MODEL, D_EXPERT),
        "w_e_up": w_e_up.astype(BF16).reshape(depth * N_EXPERTS, D_MODEL, D_EXPERT),
        "w_e_down": w_e_down.astype(BF16).reshape(depth * N_EXPERTS, D_EXPERT, D_MODEL),
        "final_norm": final_norm.reshape(1, D_MODEL),
    }
    y_prompt = _trunk(x_prompt, modr, 0, nb, params)
    y_sample = _trunk(x_sample, modr, bp, nb, params)
    return (y_prompt, y_sample)
```

```python
import functools

import numpy as np
import jax
import jax.numpy as jnp
from jax import lax
from jax.experimental import pallas as pl
from jax.experimental.pallas import tpu as pltpu

F32 = jnp.float32
BF16 = jnp.bfloat16

D_MODEL = 1024
HG_WIDTH = 512
HG_HEADS = 4
HG_DIM = 128
LB_FLOOR = 1e-30
ATTN_WIDTH = 512
HEAD_DIM = 64
N_Q_HEADS = 8
N_KV_HEADS = 2
GQA_GROUP = 4
KV_WIDTH = 128
GRID_W = 64
ROPE_AXIS_DIM = 32
ROPE_THETA = 10000.0
N_GROUPS = 4
EXPERTS_PER_GROUP = 4
N_EXPERTS = 16
D_EXPERT = 512
N_MOD = 6
NORM_EPS = 1e-6
IN_PROJ_WIDTH = 5 * HG_WIDTH + ATTN_WIDTH + 2 * KV_WIDTH
COL_HQ, COL_HF_FWD, COL_HF_BWD, COL_HI, COL_HG, COL_AQ = 0, 1, 2, 3, 4, 5
COL_AK, COL_AV = 24, 25

LANES = 128
SUBLANES = 8
VMEM_LIMIT = 56 * 1024 * 1024
ROW_TILE = 512
HG_CHUNK = 128
KV_BLOCK = 512
Q_TILE = 128
EXPERT_TILE = 128
COMBINE_TILE = 256
MOE_SEGMENT = 4096
ROUTE_LANES = 128
LOG2E = 1.4426950408889634
SAFE_SCORE_BOUND = 40.0


def _cparams(*sem):
    return pltpu.CompilerParams(dimension_semantics=sem, vmem_limit_bytes=VMEM_LIMIT)


def _dot(a, b):
    return jnp.dot(a, b, preferred_element_type=F32)


def _dot_nt(a, b):
    return lax.dot_general(a, b, (((1,), (1,)), ((), ())), preferred_element_type=F32)


def _split_bf16(x):
    hi = x.astype(BF16)
    lo = (x - hi.astype(F32)).astype(BF16)
    return hi, lo


def _sigmoid(x):
    return 1.0 / (1.0 + jnp.exp(-x))


def _mod_body(c_ref, w_ref, b_ref, o_ref):
    c = c_ref[...]
    cs = c * _sigmoid(c)
    o_ref[0] = _dot(cs.astype(BF16), w_ref[0].astype(BF16)) + b_ref[0]


def _modulation(c_all, w_ada, b_ada):
    nb = c_all.shape[0]
    depth, _, n = w_ada.shape
    tn = 1024
    return pl.pallas_call(
        _mod_body,
        grid=(depth, n // tn),
        in_specs=[pl.BlockSpec((nb, D_MODEL), lambda l, j: (0, 0)),
                  pl.BlockSpec((1, D_MODEL, tn), lambda l, j: (l, 0, j)),
                  pl.BlockSpec((1, 1, tn), lambda l, j: (l, 0, j))],
        out_specs=pl.BlockSpec((1, nb, tn), lambda l, j: (l, 0, j)),
        out_shape=jax.ShapeDtypeStruct((depth, nb, n), F32),
        compiler_params=_cparams("arbitrary", "arbitrary"),
        name="adaln_modulation",
    )(c_all, w_ada, b_ada.reshape(depth, 1, n))


def _lb_body(p_ref, o_ref):
    depth = p_ref.shape[0]
    rows = [p_ref[l:l + 1, :] for l in range(depth)]
    m = functools.reduce(jnp.maximum, rows)
    es = [jnp.exp(r - m) for r in rows]
    tot = functools.reduce(lambda a, b: a + b, es)
    sm = [e / tot for e in es]
    acc = jnp.zeros_like(sm[0])
    for l in range(depth):
        acc = acc + sm[l]
        o_ref[l:l + 1, :] = acc - sm[0]


def _lower_bounds(hg_lb):
    return pl.pallas_call(
        _lb_body,
        out_shape=jax.ShapeDtypeStruct(hg_lb.shape, F32),
        name="hgrn_lower_bounds",
    )(hg_lb)


def _inproj_body(x_ref, mod_ref, gain_ref, w_ref, p_ref, fmin_ref):
    x = x_ref[0]
    ms = jnp.mean(x * x, axis=-1, keepdims=True)
    y = x * lax.rsqrt(ms + NORM_EPS) * gain_ref[0]
    h = y * (1.0 + mod_ref[0, 1:2, :]) + mod_ref[0, 0:1, :]
    proj = _dot(h.astype(BF16), w_ref[0]).astype(BF16)
    p_ref[0] = proj
    lo, hi = COL_HF_FWD * HG_WIDTH, (COL_HF_BWD + 1) * HG_WIDTH
    cur = jnp.min(proj[:, lo:hi].astype(F32), axis=0, keepdims=True)
    first = jnp.logical_and(pl.program_id(0) == 0, pl.program_id(1) == 0)

    @pl.when(first)
    def _():
        fmin_ref[...] = cur

    @pl.when(jnp.logical_not(first))
    def _():
        fmin_ref[...] = jnp.minimum(fmin_ref[...], cur)


def _in_proj(x, modr, mod_row0, gain, w_in_bf, layer):
    b, l, _ = x.shape
    tm = min(ROW_TILE, l)
    return pl.pallas_call(
        _inproj_body,
        grid=(b, l // tm),
        in_specs=[pl.BlockSpec((1, tm, D_MODEL), lambda i, j: (i, j, 0)),
                  pl.BlockSpec((1, N_MOD, D_MODEL), lambda i, j: (mod_row0 + i, 0, 0)),
                  pl.BlockSpec((1, 1, D_MODEL), lambda i, j: (layer, 0, 0)),
                  pl.BlockSpec((1, D_MODEL, IN_PROJ_WIDTH), lambda i, j: (layer, 0, 0))],
        out_specs=[pl.BlockSpec((1, tm, IN_PROJ_WIDTH), lambda i, j: (i, j, 0)),
                   pl.BlockSpec((1, 2 * HG_WIDTH), lambda i, j: (0, 0))],
        out_shape=[jax.ShapeDtypeStruct((b, l, IN_PROJ_WIDTH), BF16),
                   jax.ShapeDtypeStruct((1, 2 * HG_WIDTH), F32)],
        compiler_params=_cparams("arbitrary", "arbitrary"),
        name="in_proj",
    )(x, modr, gain, w_in_bf)


HG_LEVELS = (64, 32, 16, 8, 4, 2, 1)
HG_INBLOCK = 8
HG_INBLOCK_MAX_EXPONENT = 80.0
HG_HEAD_GROUPS = ((0, 1, 2, 3),)
MASK_DIAG = len(HG_LEVELS)
MASK_INBLOCK = len(HG_LEVELS) + 1


def _hg_constants():
    c = HG_CHUNK
    t = np.arange(c)[:, None]
    s = np.arange(c)[None, :]
    masks = np.zeros((2, len(HG_LEVELS) + 2, c, c), np.float32)
    for li, h in enumerate(HG_LEVELS):
        same = (t // (2 * h)) == (s // (2 * h))
        masks[0, li] = same & ((t % (2 * h)) >= h) & ((s % (2 * h)) < h)
        masks[1, li] = same & ((t % (2 * h)) < h) & ((s % (2 * h)) >= h)
    masks[:, MASK_DIAG] = (t == s)
    same = (t // HG_INBLOCK) == (s // HG_INBLOCK)
    masks[0, MASK_INBLOCK] = same & (s <= t)
    masks[1, MASK_INBLOCK] = same & (s >= t)
    cum = np.stack([(s <= t), (s >= t)]).astype(np.float32)
    return jnp.asarray(masks), jnp.asarray(cum, dtype=BF16)


def _row_bcast(a, rows, blk):
    parts = [jnp.broadcast_to(a[r:r + 1, :], (blk, a.shape[1])) for r in rows]
    return parts[0] if len(parts) == 1 else jnp.concatenate(parts, axis=0)


def _level_reference(cum, h, reverse, sub):
    c = cum.shape[0]
    if 2 * h >= SUBLANES:
        rows = [p0 + (h if reverse else h - 1) for p0 in range(0, c, 2 * h)]
        return _row_bcast(cum, rows, 2 * h)
    out = None
    for p0 in reversed(range(0, SUBLANES, 2 * h)):
        r = p0 + (h if reverse else h - 1)
        piece = _row_bcast(cum, [v0 + r for v0 in range(0, c, SUBLANES)], SUBLANES)
        out = piece if out is None else jnp.where(sub < p0 + 2 * h, piece, out)
    return out


def _hg_chunks(chains, lb_ref, cum_ref, masks_ref, st_ref, inblock):
    c = HG_CHUNK
    sub = lax.broadcasted_iota(jnp.int32, (c, HG_DIM), 0) % SUBLANES
    work = []
    for head, d, q_ref, f_ref, i_ref, o_ref in chains:
        sl = slice(head * HG_DIM, (head + 1) * HG_DIM)
        lb = lb_ref[0, :, sl]
        hq = q_ref[0, :, sl].astype(F32)
        sig = _sigmoid(f_ref[0, :, sl].astype(F32))
        one_m_lb = 1.0 - lb
        w = dict(head=head, d=d, sl=sl, o_ref=o_ref, hi=i_ref[0, :, sl], q=hq * _sigmoid(hq),
                 g=jnp.log(jnp.maximum(lb, LB_FLOOR) + one_m_lb * sig),
                 kk=one_m_lb * (1.0 - sig))
        work.append(w)
    for w in work:
        g_hi, g_lo = _split_bf16(w["g"])
        w["cum"] = _dot(cum_ref[w["d"]], g_hi) + _dot(cum_ref[w["d"]], g_lo)
        w["q_bf"] = w["q"].astype(BF16)
        w["kk_bf"] = w["kk"].astype(BF16)
        w["a"] = jnp.zeros((c, c), F32)

    for li, h in enumerate(HG_LEVELS):
        if inblock and 2 * h <= HG_INBLOCK:
            continue
        for w in work:
            ref = _level_reference(w["cum"], h, w["d"] == 1, sub)
            e = jnp.exp2((jnp.abs(w["cum"] - ref) * (-LOG2E)).astype(BF16))
            w["a"] = w["a"] + _dot_nt(w["q_bf"] * e, w["kk_bf"] * e) * masks_ref[w["d"], li]
    for w in work:
        q, kk, cum, d = w["q"], w["kk"], w["cum"], w["d"]
        if inblock:
            first = HG_INBLOCK - 1 if d == 1 else 0
            ref = _row_bcast(cum, [p0 + first for p0 in range(0, c, HG_INBLOCK)], HG_INBLOCK)
            z = (cum - ref) * LOG2E
            al = _dot_nt((q * jnp.exp2(z)).astype(BF16), (kk * jnp.exp2(-z)).astype(BF16))
            w["a"] = w["a"] + al * masks_ref[d, MASK_INBLOCK]
        else:
            w["a"] = w["a"] + jnp.sum(q * kk, axis=-1, keepdims=True) * masks_ref[d, MASK_DIAG]
    for w in work:
        q, kk, cum, d, head = w["q"], w["kk"], w["cum"], w["d"], w["head"]
        hi = w["hi"]
        st = st_ref[d, head]
        tot = cum[0:1, :] if d == 1 else cum[c - 1:c, :]
        o = _dot(w["a"].astype(BF16), hi)
        o = o + _dot_nt((q * jnp.exp(cum)).astype(BF16), st.astype(BF16))
        k_end = (kk * jnp.exp(tot - cum)).astype(BF16)
        st_ref[d, head] = jnp.exp(tot) * st + _dot(hi.astype(F32).T.astype(BF16), k_end)
        w["o_ref"][0, :, w["sl"]] = o.astype(BF16)


def _hgrn_body(qf_ref, ff_ref, if_ref, qb_ref, fb_ref, ib_ref, lb_ref, cum_ref, masks_ref,
               of_ref, ob_ref, st_ref, *, inblock):
    @pl.when(pl.program_id(1) == 0)
    def _():
        st_ref[...] = jnp.zeros_like(st_ref)

    for heads in HG_HEAD_GROUPS:
        chains = [(head, d) + refs for head in heads
                  for d, refs in enumerate(((qf_ref, ff_ref, if_ref, of_ref),
                                            (qb_ref, fb_ref, ib_ref, ob_ref)))]
        _hg_chunks(chains, lb_ref, cum_ref, masks_ref, st_ref, inblock)


def _hgrn(proj, lbs, layer, fx_min):
    b, l, _ = proj.shape
    c = HG_CHUNK
    nc = l // c
    masks, cum = _hg_constants()
    fwd = lambda col: pl.BlockSpec((1, c, HG_WIDTH), lambda i, j: (i, j, col))
    bwd = lambda col: pl.BlockSpec((1, c, HG_WIDTH), lambda i, j: (i, nc - 1 - j, col))
    out_sd = jax.ShapeDtypeStruct((b, l, HG_WIDTH), BF16)

    def call(inblock):
        return pl.pallas_call(
            functools.partial(_hgrn_body, inblock=inblock),
            grid=(b, nc),
            in_specs=[fwd(COL_HQ), fwd(COL_HF_FWD), fwd(COL_HI),
                      bwd(COL_HQ), bwd(COL_HF_BWD), bwd(COL_HI),
                      pl.BlockSpec((1, 1, HG_WIDTH), lambda i, j: (layer, 0, 0)),
                      pl.BlockSpec(cum.shape, lambda i, j: (0, 0, 0)),
                      pl.BlockSpec(masks.shape, lambda i, j: (0, 0, 0, 0))],
            out_specs=[pl.BlockSpec((1, c, HG_WIDTH), lambda i, j: (i, j, 0)),
                       pl.BlockSpec((1, c, HG_WIDTH), lambda i, j: (i, nc - 1 - j, 0))],
            out_shape=[out_sd, out_sd],
            scratch_shapes=[pltpu.VMEM((2, HG_HEADS, HG_DIM, HG_DIM), F32)],
            compiler_params=_cparams("arbitrary", "arbitrary"),
            name="hgrn2_recurrence",
        )(proj, proj, proj, proj, proj, proj, lbs, cum, masks)

    lb = jnp.tile(lbs[layer], (1, 2))
    worst = jnp.log(jnp.maximum(lb, LB_FLOOR) + (1.0 - lb) * _sigmoid(fx_min))
    safe = (HG_INBLOCK - 1) * jnp.max(-worst) <= HG_INBLOCK_MAX_EXPONENT
    return lax.cond(safe, lambda: call(True), lambda: call(False))


def _rope_tables(l):
    lane = np.arange(LANES)
    dd = lane % HEAD_DIM
    axis = dd // ROPE_AXIS_DIM
    first_half = (dd % ROPE_AXIS_DIM) < (ROPE_AXIS_DIM // 2)
    freq_idx = dd % (ROPE_AXIS_DIM // 2)
    inv_freq = ROPE_THETA ** (-jnp.arange(0, ROPE_AXIS_DIM, 2, dtype=F32) / ROPE_AXIS_DIM)
    t = jnp.arange(l)
    pos = jnp.where(jnp.asarray(axis)[None, :] == 0, (t // GRID_W)[:, None], (t % GRID_W)[:, None])
    ang = pos.astype(F32) * inv_freq[jnp.asarray(freq_idx)][None, :]
    sign = jnp.where(jnp.asarray(first_half), -1.0, 1.0)[None, :]
    return jnp.cos(ang), jnp.sin(ang) * sign


def _norm_rope(x, gain, cos_t, sin_t, bd, first_half):
    x2_hi, x2_lo = _split_bf16(x * x)
    ss = _dot(x2_hi, bd) + _dot(x2_lo, bd)
    xn = x * lax.rsqrt(ss * (1.0 / HEAD_DIM) + NORM_EPS) * gain
    half = ROPE_AXIS_DIM // 2
    partner = jnp.where(first_half, pltpu.roll(xn, LANES - half, 1), pltpu.roll(xn, half, 1))
    return xn * cos_t + partner * sin_t


def _qkprep_body(q_ref, k_ref, v_ref, qg_ref, kg_ref, cos_ref, sin_ref, bd_ref, qt_ref, ko_ref, vt_ref):
    tq = qt_ref.shape[4] // GQA_GROUP
    cos_t = cos_ref[...]
    sin_t = sin_ref[...]
    bd = bd_ref[...]
    lane = lax.broadcasted_iota(jnp.int32, cos_t.shape, 1)
    first_half = (lane % ROPE_AXIS_DIM) < (ROPE_AXIS_DIM // 2)
    zeros = jnp.zeros((HEAD_DIM, tq), BF16)
    for j in range(ATTN_WIDTH // LANES):
        sl = slice(j * LANES, (j + 1) * LANES)
        qr = _norm_rope(q_ref[0, :, sl].astype(F32), qg_ref[0], cos_t, sin_t, bd, first_half)
        qr = qr * (HEAD_DIM ** -0.5 * LOG2E)
        h = (2 * j) // GQA_GROUP
        for u in range(qt_ref.shape[2]):
            t = qr[u * tq:(u + 1) * tq, :].T.astype(BF16)
            for e in range(2):
                piece = t[e * HEAD_DIM:(e + 1) * HEAD_DIM, :]
                blockcol = jnp.concatenate([piece, zeros] if h == 0 else [zeros, piece], axis=0)
                g = (2 * j + e) % GQA_GROUP
                qt_ref[0, h, u, :, g * tq:(g + 1) * tq] = blockcol
    kr = _norm_rope(k_ref[0].astype(F32), kg_ref[0], cos_t, sin_t, bd, first_half)
    ko_ref[0] = kr.astype(BF16)
    vt_ref[0, 0] = v_ref[0].astype(F32).T.astype(BF16)


def _qk_prep(proj, q_gain2, k_gain2, layer):
    b, l, _ = proj.shape
    tk = min(KV_BLOCK, l)
    tq = min(Q_TILE, l)
    cos_t, sin_t = _rope_tables(l)
    blk = np.arange(LANES) // HEAD_DIM
    bd = jnp.asarray((blk[:, None] == blk[None, :]).astype(np.float32), dtype=BF16)
    return pl.pallas_call(
        _qkprep_body,
        grid=(b, l // tk),
        in_specs=[pl.BlockSpec((1, tk, ATTN_WIDTH), lambda i, j: (i, j, COL_AQ)),
                  pl.BlockSpec((1, tk, KV_WIDTH), lambda i, j: (i, j, COL_AK)),
                  pl.BlockSpec((1, tk, KV_WIDTH), lambda i, j: (i, j, COL_AV)),
                  pl.BlockSpec((1, 1, LANES), lambda i, j: (layer, 0, 0)),
                  pl.BlockSpec((1, 1, LANES), lambda i, j: (layer, 0, 0)),
                  pl.BlockSpec((tk, LANES), lambda i, j: (j, 0)),
                  pl.BlockSpec((tk, LANES), lambda i, j: (j, 0)),
                  pl.BlockSpec((LANES, LANES), lambda i, j: (0, 0))],
        out_specs=[pl.BlockSpec((1, N_KV_HEADS, tk // tq, KV_WIDTH, GQA_GROUP * tq),
                                lambda i, j: (i, 0, j, 0, 0)),
                   pl.BlockSpec((1, tk, KV_WIDTH), lambda i, j: (i, j, 0)),
                   pl.BlockSpec((1, 1, KV_WIDTH, tk), lambda i, j: (i, j, 0, 0))],
        out_shape=[jax.ShapeDtypeStruct((b, N_KV_HEADS, l // tq, KV_WIDTH, GQA_GROUP * tq), BF16),
                   jax.ShapeDtypeStruct((b, l, KV_WIDTH), BF16),
                   jax.ShapeDtypeStruct((b, l // tk, KV_WIDTH, tk), BF16)],
        compiler_params=_cparams("arbitrary", "arbitrary"),
        name="qk_norm_rope",
    )(proj, proj, proj, q_gain2, k_gain2, cos_t, sin_t, bd)


def _attn_finish(acc_scr, l_fin, o_ref):
    tq = o_ref.shape[1]
    outs = []
    for h in range(N_KV_HEADS):
        o = (acc_scr[h] / l_fin[h]).T
        oh = o[:, h * HEAD_DIM:(h + 1) * HEAD_DIM]
        outs += [oh[g * tq:(g + 1) * tq, :] for g in range(GQA_GROUP)]
    o_ref[0] = jnp.concatenate(outs, axis=1).astype(BF16)


def _attn_bounded_body(bound_ref, qt_ref, k_ref, vt_ref, o_ref, acc_scr):
    nblk = vt_ref.shape[1]
    tk = vt_ref.shape[3]
    cols = qt_ref.shape[4]
    shift = bound_ref[0, 0]
    acc_scr[...] = jnp.zeros_like(acc_scr)

    def step(i, sums):
        kb = k_ref[0, pl.ds(pl.multiple_of(i * tk, tk), tk), :]
        vt = vt_ref[0, i]
        scores = [_dot(kb, qt_ref[0, h, 0]) for h in range(N_KV_HEADS)]
        new = []
        for h in range(N_KV_HEADS):
            p = jnp.exp2(scores[h] - shift)
            new.append(sums[h] + jnp.sum(p, axis=0, keepdims=True))
            acc_scr[h] = acc_scr[h] + _dot(vt, p.astype(BF16))
        return tuple(new)

    init = tuple(jnp.zeros((1, cols), F32) for _ in range(N_KV_HEADS))
    _attn_finish(acc_scr, lax.fori_loop(0, nblk, step, init, unroll=4), o_ref)


def _attn_online_body(bound_ref, qt_ref, k_ref, vt_ref, o_ref, acc_scr):
    nblk = vt_ref.shape[1]
    tk = vt_ref.shape[3]
    cols = qt_ref.shape[4]
    acc_scr[...] = jnp.zeros_like(acc_scr)
    l_fin = []
    for h in range(N_KV_HEADS):
        qt = qt_ref[0, h, 0]

        def step(i, carry):
            m_prev, l_prev = carry
            kb = k_ref[0, pl.ds(pl.multiple_of(i * tk, tk), tk), :]
            s = _dot(kb, qt)
            m_new = jnp.maximum(m_prev, jnp.max(s, axis=0, keepdims=True))
            alpha = jnp.exp2(m_prev - m_new)
            p = jnp.exp2(s - m_new)
            l_new = alpha * l_prev + jnp.sum(p, axis=0, keepdims=True)
            acc_scr[h] = alpha * acc_scr[h] + _dot(vt_ref[0, i], p.astype(BF16))
            return m_new, l_new

        init = (jnp.full((1, cols), -jnp.inf, F32), jnp.zeros((1, cols), F32))
        l_fin.append(lax.fori_loop(0, nblk, step, init)[1])
    _attn_finish(acc_scr, l_fin, o_ref)


def _attention(qt, k_rot, vt, bound):
    b, l, _ = k_rot.shape
    tq = qt.shape[4] // GQA_GROUP
    nblk, tk = vt.shape[1], vt.shape[3]

    def call(body):
        return pl.pallas_call(
            body,
            grid=(b, l // tq),
            in_specs=[pl.BlockSpec(memory_space=pltpu.SMEM),
                      pl.BlockSpec((1, N_KV_HEADS, 1, KV_WIDTH, GQA_GROUP * tq), lambda i, j: (i, 0, j, 0, 0)),
                      pl.BlockSpec((1, l, KV_WIDTH), lambda i, j: (i, 0, 0)),
                      pl.BlockSpec((1, nblk, KV_WIDTH, tk), lambda i, j: (i, 0, 0, 0))],
            out_specs=pl.BlockSpec((1, tq, ATTN_WIDTH), lambda i, j: (i, j, 0)),
            out_shape=jax.ShapeDtypeStruct((b, l, ATTN_WIDTH), BF16),
            scratch_shapes=[pltpu.VMEM((N_KV_HEADS, KV_WIDTH, GQA_GROUP * tq), F32)],
            compiler_params=_cparams("arbitrary", "arbitrary"),
            name="gqa_attention",
        )(bound, qt, k_rot, vt)

    return lax.cond(bound[0, 0] <= SAFE_SCORE_BOUND,
                    lambda: call(_attn_bounded_body), lambda: call(_attn_online_body))


def _route(logits):
    lane = lax.broadcasted_iota(jnp.int32, logits.shape, 1).astype(F32)
    neg = jnp.float32(-jnp.inf)
    lanemin = lambda cond: jnp.min(jnp.where(cond, lane, float(ROUTE_LANES)), axis=-1, keepdims=True)
    gl = jnp.where(lane < N_GROUPS, logits, neg)
    gmax = jnp.max(gl, axis=-1, keepdims=True)
    g_idx = lanemin(gl == gmax)
    pg_top = 1.0 / jnp.sum(jnp.exp(gl - gmax), axis=-1, keepdims=True)
    first = N_GROUPS + g_idx * EXPERTS_PER_GROUP
    in_group = (lane >= first) & (lane < first + EXPERTS_PER_GROUP)
    el = jnp.where(in_group, logits, neg)
    a_max = jnp.max(el, axis=-1, keepdims=True)
    a_idx = lanemin(el == a_max)
    el2 = jnp.where(lane == a_idx, neg, el)
    b_max = jnp.max(el2, axis=-1, keepdims=True)
    b_idx = lanemin(el2 == b_max)
    r = jnp.exp(b_max - a_max)
    w_a = pg_top / (1.0 + r)
    w_b = pg_top * r / (1.0 + r)
    return a_idx - N_GROUPS, b_idx - N_GROUPS, w_a, w_b


def _outproj_body(of_ref, ob_ref, hg_ref, at_ref, x_ref, mod_ref, hgain_ref, wout_ref, fgain_ref,
                  wr_ref, br_ref, tri_ref, xo_ref, h_ref, rt_ref, cnt_ref, cnt_scr, *, seg):
    tm = x_ref.shape[1]
    tile = pl.program_id(0) * pl.num_programs(1) + pl.program_id(1)

    @pl.when(tile == 0)
    def _():
        cnt_scr[...] = jnp.zeros_like(cnt_scr)

    parts = []
    for head in range(HG_HEADS):
        sl = slice(head * HG_DIM, (head + 1) * HG_DIM)
        o = of_ref[0, :, sl].astype(F32) + ob_ref[0, :, sl].astype(F32)
        o = o * lax.rsqrt(jnp.mean(o * o, axis=-1, keepdims=True) + NORM_EPS) * hgain_ref[0]
        gt = hg_ref[0, :, sl].astype(F32)
        parts.append((o * (gt * _sigmoid(gt))).astype(BF16))
    o_hg = jnp.concatenate(parts, axis=1)
    mix = _dot(o_hg, wout_ref[0, :HG_WIDTH, :]) + _dot(at_ref[0], wout_ref[0, HG_WIDTH:, :])
    x = x_ref[0] + mod_ref[0, 2:3, :] * mix
    xo_ref[0] = x
    y = x * lax.rsqrt(jnp.mean(x * x, axis=-1, keepdims=True) + NORM_EPS) * fgain_ref[0]
    h = y * (1.0 + mod_ref[0, 4:5, :]) + mod_ref[0, 3:4, :]
    h_ref[0] = h
    h_hi, h_lo = _split_bf16(h)
    w_hi, w_lo = _split_bf16(wr_ref[0])
    logits = _dot(h_hi, w_hi) + _dot(h_hi, w_lo) + _dot(h_lo, w_hi) + br_ref[0]
    e_a, e_b, w_a, w_b = _route(logits)

    lane = lax.broadcasted_iota(jnp.int32, logits.shape, 1).astype(F32)
    group0 = (((tile * tm) // seg) * N_EXPERTS).astype(F32)
    hot_a = jnp.where(lane == e_a + group0, 1.0, 0.0)
    hot_b = jnp.where(lane == e_b + group0, 1.0, 0.0)
    hot = hot_a + hot_b
    before = cnt_scr[...] + _dot(tri_ref[...], hot.astype(BF16))
    rank_a = jnp.sum(hot_a * before, axis=-1, keepdims=True)
    rank_b = jnp.sum(hot_b * before, axis=-1, keepdims=True)
    cnt_scr[...] = cnt_scr[...] + jnp.sum(hot, axis=0, keepdims=True)
    cnt_ref[...] = cnt_scr[...]
    cols = (e_a, e_b, w_a, w_b, rank_a, rank_b)
    out = jnp.zeros_like(logits)
    for c, v in enumerate(cols):
        out = jnp.where(lane == float(c), v, out)
    rt_ref[0] = out


def _out_proj(o_f, o_b, proj, o_at, x, modr, mod_row0, hgain, w_out_bf, fgain, w_route, b_route, layer,
              seg):
    b, l, _ = x.shape
    tm = min(ROW_TILE, l)
    tok = lambda w, col=0: pl.BlockSpec((1, tm, w), lambda i, j: (i, j, col))
    lay = lambda *s: pl.BlockSpec((1,) + s, lambda i, j: (layer,) + (0,) * len(s))
    tri = jnp.asarray(np.tril(np.ones((tm, tm), np.float32), -1), dtype=BF16)
    return pl.pallas_call(
        functools.partial(_outproj_body, seg=seg),
        grid=(b, l // tm),
        in_specs=[tok(HG_WIDTH), tok(HG_WIDTH), tok(HG_WIDTH, COL_HG), tok(ATTN_WIDTH), tok(D_MODEL),
                  pl.BlockSpec((1, N_MOD, D_MODEL), lambda i, j: (mod_row0 + i, 0, 0)),
                  lay(1, HG_DIM), lay(D_MODEL, D_MODEL), lay(1, D_MODEL),
                  lay(D_MODEL, ROUTE_LANES), lay(1, ROUTE_LANES),
                  pl.BlockSpec((tm, tm), lambda i, j: (0, 0))],
        out_specs=[tok(D_MODEL), tok(D_MODEL), tok(ROUTE_LANES),
                   pl.BlockSpec((1, ROUTE_LANES), lambda i, j: (0, 0))],
        out_shape=[jax.ShapeDtypeStruct((b, l, D_MODEL), F32),
                   jax.ShapeDtypeStruct((b, l, D_MODEL), F32),
                   jax.ShapeDtypeStruct((b, l, ROUTE_LANES), F32),
                   jax.ShapeDtypeStruct((1, ROUTE_LANES), F32)],
        scratch_shapes=[pltpu.VMEM((1, ROUTE_LANES), F32)],
        compiler_params=_cparams("arbitrary", "arbitrary"),
        name="out_proj_router",
    )(o_f, o_b, proj, o_at, x, modr, hgain, w_out_bf, fgain, w_route, b_route, tri)


def _moe_body(te_ref, tv_ref, tu_ref, tok_ref, h_ref, wg0_ref, wu0_ref, wd0_ref, wg1_ref, wu1_ref,
              wd1_ref, o_ref, xs_scr):
    step = pl.program_id(0)
    tr = xs_scr.shape[1]
    weights = ((wg0_ref, wu0_ref, wd0_ref), (wg1_ref, wu1_ref, wd1_ref))

    def gather(u):
        first = tu_ref[2 * step + u]

        def row(r, carry):
            xs_scr[u, pl.ds(r, 1), :] = h_ref[0, pl.ds(tok_ref[first + r], 1), :]
            return carry

        lax.fori_loop(0, tr, row, 0, unroll=8)

    def ffn(u):
        wg_ref, wu_ref, wd_ref = weights[u]
        xs = xs_scr[u].astype(BF16)
        gt = _dot(xs, wg_ref[0])
        up = _dot(xs, wu_ref[0])
        hid = (gt * _sigmoid(gt)) * up
        return _dot(hid.astype(BF16), wd_ref[0])

    valid0 = tv_ref[2 * step] != 0
    valid1 = tv_ref[2 * step + 1] != 0

    @pl.when(valid1)
    def _():
        gather(0)
        gather(1)
        o_ref[:tr, :] = ffn(0)
        o_ref[tr:, :] = ffn(1)

    @pl.when(jnp.logical_and(valid0, jnp.logical_not(valid1)))
    def _():
        gather(0)
        o_ref[:tr, :] = ffn(0)
        o_ref[tr:, :] = jnp.zeros((tr, o_ref.shape[1]), F32)

    @pl.when(jnp.logical_not(valid0))
    def _():
        o_ref[...] = jnp.zeros_like(o_ref)


def _moe_ffn(h_seg, sorted_tok, tile_expert, tile_valid, tile_first, wg, wu, wd, layer):
    nseg, seg, d = h_seg.shape
    n_tiles = tile_expert.shape[0]
    tr = EXPERT_TILE
    steps_per_seg = n_tiles // nseg // 2
    wspec = lambda k, n, u: pl.BlockSpec(
        (1, k, n), lambda i, te, *_: (layer * N_EXPERTS + te[2 * i + u], 0, 0))
    wspecs = [wspec(D_MODEL, D_EXPERT, u) if j < 2 else wspec(D_EXPERT, D_MODEL, u)
              for u in range(2) for j in range(3)]
    return pl.pallas_call(
        _moe_body,
        grid_spec=pltpu.PrefetchScalarGridSpec(
            num_scalar_prefetch=4,
            grid=(n_tiles // 2,),
            in_specs=[pl.BlockSpec((1, seg, d), lambda i, *_: (i // steps_per_seg, 0, 0))] + wspecs,
            out_specs=pl.BlockSpec((2 * tr, d), lambda i, *_: (i, 0)),
            scratch_shapes=[pltpu.VMEM((2, tr, d), F32)]),
        out_shape=jax.ShapeDtypeStruct((n_tiles * tr, d), F32),
        compiler_params=_cparams("arbitrary"),
        name="routed_expert_ffn",
    )(tile_expert, tile_valid, tile_first, sorted_tok, h_seg, wg, wu, wd, wg, wu, wd)


def _combine_body(pa_ref, pb_ref, rows_ref, rt_ref, x_ref, mod_ref, fin_ref, o_ref, a_scr, b_scr, *,
                  final):
    tm = o_ref.shape[1]

    def gather(r, carry):
        a_scr[pl.ds(r, 1), :] = rows_ref[0, pl.ds(pa_ref[0, 0, r], 1), :]
        b_scr[pl.ds(r, 1), :] = rows_ref[0, pl.ds(pb_ref[0, 0, r], 1), :]
        return carry

    lax.fori_loop(0, tm, gather, 0, unroll=8)
    rt = rt_ref[0]
    moe = rt[:, 2:3] * a_scr[...] + rt[:, 3:4] * b_scr[...]
    x = x_ref[0] + mod_ref[0, 5:6, :] * moe
    if final:
        x = x * lax.rsqrt(jnp.mean(x * x, axis=-1, keepdims=True) + NORM_EPS) * fin_ref[...]
    o_ref[0] = x


def _combine(pos_a, pos_b, rows_seg, route, x, modr, mod_row0, final_gain, final):
    b, l, _ = x.shape
    nseg, rseg, d = rows_seg.shape
    tm = min(COMBINE_TILE, l)
    tiles_l = l // tm
    seg_tiles = (b * tiles_l) // nseg
    flat = lambda i, j: i * tiles_l + j
    tok = lambda w: pl.BlockSpec((1, tm, w), lambda i, j: (i, j, 0))
    smem = pl.BlockSpec((1, 1, tm), lambda i, j: (flat(i, j), 0, 0), memory_space=pltpu.SMEM)
    return pl.pallas_call(
        functools.partial(_combine_body, final=final),
        grid=(b, tiles_l),
        in_specs=[smem, smem,
                  pl.BlockSpec((1, rseg, d), lambda i, j: (flat(i, j) // seg_tiles, 0, 0),
                               pipeline_mode=pl.Buffered(1)),
                  tok(ROUTE_LANES), tok(D_MODEL),
                  pl.BlockSpec((1, N_MOD, D_MODEL), lambda i, j: (mod_row0 + i, 0, 0)),
                  pl.BlockSpec((1, D_MODEL), lambda i, j: (0, 0))],
        out_specs=tok(D_MODEL),
        out_shape=jax.ShapeDtypeStruct((b, l, D_MODEL), F32),
        scratch_shapes=[pltpu.VMEM((tm, d), F32), pltpu.VMEM((tm, d), F32)],
        compiler_params=_cparams("arbitrary", "arbitrary"),
        name="moe_combine",
    )(pos_a.reshape(-1, 1, tm), pos_b.reshape(-1, 1, tm), rows_seg, route, x, modr, final_gain)


def _routing_tables(route, counts, nseg, seg):
    t = route.shape[0]
    tr = EXPERT_TILE
    rseg = 2 * seg + N_EXPERTS * tr
    ngrp = nseg * N_EXPERTS
    cnt = counts[0, :ngrp].astype(jnp.int32)
    pad2 = (((cnt + tr - 1) // tr) * tr).reshape(nseg, N_EXPERTS)
    end_local = jnp.cumsum(pad2, axis=1)
    start_local = (end_local - pad2).reshape(-1)
    start_unp = jnp.cumsum(cnt) - cnt
    ids = route[:, 0:2].astype(jnp.int32)
    rank = route[:, 4:6].astype(jnp.int32)
    key = (jnp.arange(t, dtype=jnp.int32) // seg)[:, None] * N_EXPERTS + ids
    hot = key[:, :, None] == jnp.arange(ngrp, dtype=jnp.int32)[None, None, :]
    lookup = lambda table: jnp.sum(jnp.where(hot, table[None, None, :], 0), axis=-1)
    pos = lookup(start_local) + rank
    order = lookup(start_unp) + rank
    tok = jnp.broadcast_to((jnp.arange(t, dtype=jnp.int32) % seg)[:, None], (t, 2))
    _, sorted_tok = lax.sort((order.reshape(-1), tok.reshape(-1)), num_keys=1)
    sorted_tok = jnp.concatenate([sorted_tok, jnp.zeros((tr,), jnp.int32)])
    tiles_per_seg = rseg // tr
    local_row = jnp.arange(tiles_per_seg, dtype=jnp.int32) * tr
    tile_grp = jnp.sum(local_row[None, :, None] >= end_local[:, None, :], axis=-1)
    tile_valid = (tile_grp < N_EXPERTS).astype(jnp.int32)
    tile_expert = jnp.minimum(tile_grp, N_EXPERTS - 1).astype(jnp.int32)
    grp = tile_expert + (jnp.arange(nseg, dtype=jnp.int32) * N_EXPERTS)[:, None]
    hot_t = grp[:, :, None] == jnp.arange(ngrp, dtype=jnp.int32)[None, None, :]
    skipped_pad = jnp.sum(jnp.where(hot_t, (start_local - start_unp)[None, None, :], 0), axis=-1)
    tile_first = (local_row[None, :] - skipped_pad) * tile_valid
    return (sorted_tok, pos[:, 0], pos[:, 1], tile_expert.reshape(-1), tile_valid.reshape(-1),
            tile_first.reshape(-1), rseg)


def _trunk(x, modr, mod_row0, nb, params):
    b, l, _ = x.shape
    t = b * l
    seg = min(MOE_SEGMENT, t)
    nseg = t // seg
    depth = params["w_in"].shape[0]
    for layer in range(depth):
        row0 = layer * nb + mod_row0
        proj, fx_min = _in_proj(x, modr, row0, params["norm_mix"], params["w_in"], layer)
        o_f, o_b = _hgrn(proj, params["lbs"], layer, fx_min)
        qt, k_rot, vt = _qk_prep(proj, params["q_gain2"], params["k_gain2"], layer)
        o_at = _attention(qt, k_rot, vt, params["score_bound"][layer])
        x_mid, h, route, counts = _out_proj(o_f, o_b, proj, o_at, x, modr, row0, params["hg_gain"],
                                            params["w_out"], params["norm_ffn"], params["w_route"],
                                            params["b_route"], layer, seg)
        sorted_tok, pos_a, pos_b, tile_e, tile_v, tile_f, rseg = _routing_tables(
            route.reshape(t, ROUTE_LANES), counts, nseg, seg)
        rows = _moe_ffn(h.reshape(nseg, seg, D_MODEL), sorted_tok, tile_e, tile_v, tile_f,
                        params["w_e_gate"], params["w_e_up"], params["w_e_down"], layer)
        x = _combine(pos_a, pos_b, rows.reshape(nseg, rseg, D_MODEL), route, x_mid, modr, row0,
                     params["final_norm"], final=(layer == depth - 1))
    return x


def kernel(x_prompt, x_sample, c_prompt, c_sample, w_in, w_out, hg_lb, hg_out_norm, q_norm, k_norm,
           norm_mix, norm_ffn, w_ada, b_ada, w_group, b_group, w_router, b_router, w_e_gate, w_e_up,
           w_e_down, final_norm):
    depth = w_in.shape[0]
    bp, bs = c_prompt.shape[0], c_sample.shape[0]
    nb = -(-(bp + bs) // SUBLANES) * SUBLANES
    c_all = jnp.zeros((nb, D_MODEL), F32).at[:bp].set(c_prompt).at[bp:bp + bs].set(c_sample)
    mod = _modulation(c_all, w_ada, b_ada)
    modr = mod.reshape(depth * nb, N_MOD, D_MODEL)
    pad = ROUTE_LANES - N_GROUPS - N_EXPERTS
    params = {
        "w_in": w_in.astype(BF16),
        "w_out": w_out.astype(BF16),
        "lbs": _lower_bounds(hg_lb).reshape(depth, 1, HG_WIDTH),
        "hg_gain": hg_out_norm.reshape(depth, 1, HG_DIM),
        "q_gain2": jnp.tile(q_norm, (1, LANES // HEAD_DIM)).reshape(depth, 1, LANES),
        "k_gain2": jnp.tile(k_norm, (1, LANES // HEAD_DIM)).reshape(depth, 1, LANES),
        "score_bound": (1.02 * LOG2E * HEAD_DIM ** 0.5 * jnp.max(jnp.abs(q_norm), axis=1)
                        * jnp.max(jnp.abs(k_norm), axis=1)).reshape(depth, 1, 1),
        "norm_mix": norm_mix.reshape(depth, 1, D_MODEL),
        "norm_ffn": norm_ffn.reshape(depth, 1, D_MODEL),
        "w_route": jnp.pad(jnp.concatenate([w_group, w_router], axis=-1), ((0, 0), (0, 0), (0, pad))),
        "b_route": jnp.pad(jnp.concatenate([b_group, b_router], axis=-1),
                           ((0, 0), (0, pad))).reshape(depth, 1, ROUTE_LANES),
        "w_e_gate": w_e_gate.astype(BF16).reshape(depth * N_EXPERTS, D_MODEL, D_EXPERT),
        "w_e_up": w_e_up.astype(BF16).reshape(depth * N_EXPERTS, D_MODEL, D_EXPERT),
        "w_e_down": w_e_down.astype(BF16).reshape(depth * N_EXPERTS, D_EXPERT, D_MODEL),
        "final_norm": final_norm.reshape(1, D_MODEL),
    }
    y_prompt = _trunk(x_prompt, modr, 0, nb, params)
    y_sample = _trunk(x_sample, modr, bp, nb, params)
    return (y_prompt, y_sample)
```

```python
import functools

import numpy as np
import jax
import jax.numpy as jnp
from jax import lax
from jax.experimental import pallas as pl
from jax.experimental.pallas import tpu as pltpu

F32 = jnp.float32
BF16 = jnp.bfloat16

D_MODEL = 1024
HG_WIDTH = 512
HG_HEADS = 4
HG_DIM = 128
LB_FLOOR = 1e-30
ATTN_WIDTH = 512
HEAD_DIM = 64
N_Q_HEADS = 8
N_KV_HEADS = 2
GQA_GROUP = 4
KV_WIDTH = 128
GRID_W = 64
ROPE_AXIS_DIM = 32
ROPE_THETA = 10000.0
N_GROUPS = 4
EXPERTS_PER_GROUP = 4
N_EXPERTS = 16
D_EXPERT = 512
N_MOD = 6
NORM_EPS = 1e-6
IN_PROJ_WIDTH = 5 * HG_WIDTH + ATTN_WIDTH + 2 * KV_WIDTH
COL_HQ, COL_HF_FWD, COL_HF_BWD, COL_HI, COL_HG, COL_AQ = 0, 1, 2, 3, 4, 5
COL_AK, COL_AV = 24, 25

LANES = 128
SUBLANES = 8
VMEM_LIMIT = 56 * 1024 * 1024
ROW_TILE = 512
HG_CHUNK = 128
KV_BLOCK = 512
Q_TILE = 256
EXPERT_TILE = 128
COMBINE_TILE = 256
MOE_SEGMENT = 4096
ROUTE_LANES = 128
LOG2E = 1.4426950408889634
SAFE_SCORE_BOUND = 40.0


def _cparams(*sem):
    return pltpu.CompilerParams(dimension_semantics=sem, vmem_limit_bytes=VMEM_LIMIT)


def _dot(a, b):
    return jnp.dot(a, b, preferred_element_type=F32)


def _dot_nt(a, b):
    return lax.dot_general(a, b, (((1,), (1,)), ((), ())), preferred_element_type=F32)


def _split_bf16(x):
    hi = x.astype(BF16)
    lo = (x - hi.astype(F32)).astype(BF16)
    return hi, lo


def _sigmoid(x):
    return 1.0 / (1.0 + jnp.exp(-x))


def _mod_body(c_ref, w_ref, b_ref, o_ref):
    c = c_ref[...]
    cs = c * _sigmoid(c)
    o_ref[0] = _dot(cs.astype(BF16), w_ref[0].astype(BF16)) + b_ref[0]


def _modulation(c_all, w_ada, b_ada):
    nb = c_all.shape[0]
    depth, _, n = w_ada.shape
    tn = 1024
    return pl.pallas_call(
        _mod_body,
        grid=(depth, n // tn),
        in_specs=[pl.BlockSpec((nb, D_MODEL), lambda l, j: (0, 0)),
                  pl.BlockSpec((1, D_MODEL, tn), lambda l, j: (l, 0, j)),
                  pl.BlockSpec((1, 1, tn), lambda l, j: (l, 0, j))],
        out_specs=pl.BlockSpec((1, nb, tn), lambda l, j: (l, 0, j)),
        out_shape=jax.ShapeDtypeStruct((depth, nb, n), F32),
        compiler_params=_cparams("arbitrary", "arbitrary"),
        name="adaln_modulation",
    )(c_all, w_ada, b_ada.reshape(depth, 1, n))


def _lb_body(p_ref, o_ref):
    depth = p_ref.shape[0]
    rows = [p_ref[l:l + 1, :] for l in range(depth)]
    m = functools.reduce(jnp.maximum, rows)
    es = [jnp.exp(r - m) for r in rows]
    tot = functools.reduce(lambda a, b: a + b, es)
    sm = [e / tot for e in es]
    acc = jnp.zeros_like(sm[0])
    for l in range(depth):
        acc = acc + sm[l]
        o_ref[l:l + 1, :] = acc - sm[0]


def _lower_bounds(hg_lb):
    return pl.pallas_call(
        _lb_body,
        out_shape=jax.ShapeDtypeStruct(hg_lb.shape, F32),
        name="hgrn_lower_bounds",
    )(hg_lb)


def _inproj_body(x_ref, mod_ref, gain_ref, w_ref, p_ref, fmin_ref):
    x = x_ref[0]
    ms = jnp.mean(x * x, axis=-1, keepdims=True)
    y = x * lax.rsqrt(ms + NORM_EPS) * gain_ref[0]
    h = y * (1.0 + mod_ref[0, 1:2, :]) + mod_ref[0, 0:1, :]
    proj = _dot(h.astype(BF16), w_ref[0]).astype(BF16)
    p_ref[0] = proj
    lo, hi = COL_HF_FWD * HG_WIDTH, (COL_HF_BWD + 1) * HG_WIDTH
    cur = jnp.min(proj[:, lo:hi].astype(F32), axis=0, keepdims=True)
    first = jnp.logical_and(pl.program_id(0) == 0, pl.program_id(1) == 0)

    @pl.when(first)
    def _():
        fmin_ref[...] = cur

    @pl.when(jnp.logical_not(first))
    def _():
        fmin_ref[...] = jnp.minimum(fmin_ref[...], cur)


def _in_proj(x, modr, mod_row0, gain, w_in_bf, layer):
    b, l, _ = x.shape
    tm = min(ROW_TILE, l)
    return pl.pallas_call(
        _inproj_body,
        grid=(b, l // tm),
        in_specs=[pl.BlockSpec((1, tm, D_MODEL), lambda i, j: (i, j, 0)),
                  pl.BlockSpec((1, N_MOD, D_MODEL), lambda i, j: (mod_row0 + i, 0, 0)),
                  pl.BlockSpec((1, 1, D_MODEL), lambda i, j: (layer, 0, 0)),
                  pl.BlockSpec((1, D_MODEL, IN_PROJ_WIDTH), lambda i, j: (layer, 0, 0))],
        out_specs=[pl.BlockSpec((1, tm, IN_PROJ_WIDTH), lambda i, j: (i, j, 0)),
                   pl.BlockSpec((1, 2 * HG_WIDTH), lambda i, j: (0, 0))],
        out_shape=[jax.ShapeDtypeStruct((b, l, IN_PROJ_WIDTH), BF16),
                   jax.ShapeDtypeStruct((1, 2 * HG_WIDTH), F32)],
        compiler_params=_cparams("arbitrary", "arbitrary"),
        name="in_proj",
    )(x, modr, gain, w_in_bf)


HG_LEVELS = (64, 32, 16, 8, 4, 2, 1)
HG_INBLOCK = 8
HG_INBLOCK_MAX_EXPONENT = 80.0
HG_HEAD_GROUPS = ((0, 1, 2, 3),)
MASK_DIAG = len(HG_LEVELS)
MASK_INBLOCK = len(HG_LEVELS) + 1


def _hg_constants():
    c = HG_CHUNK
    t = np.arange(c)[:, None]
    s = np.arange(c)[None, :]
    masks = np.zeros((2, len(HG_LEVELS) + 2, c, c), np.float32)
    for li, h in enumerate(HG_LEVELS):
        same = (t // (2 * h)) == (s // (2 * h))
        masks[0, li] = same & ((t % (2 * h)) >= h) & ((s % (2 * h)) < h)
        masks[1, li] = same & ((t % (2 * h)) < h) & ((s % (2 * h)) >= h)
    masks[:, MASK_DIAG] = (t == s)
    same = (t // HG_INBLOCK) == (s // HG_INBLOCK)
    masks[0, MASK_INBLOCK] = same & (s <= t)
    masks[1, MASK_INBLOCK] = same & (s >= t)
    cum = np.stack([(s <= t), (s >= t)]).astype(np.float32)
    return jnp.asarray(masks), jnp.asarray(cum, dtype=BF16)


def _row_bcast(a, rows, blk):
    parts = [jnp.broadcast_to(a[r:r + 1, :], (blk, a.shape[1])) for r in rows]
    return parts[0] if len(parts) == 1 else jnp.concatenate(parts, axis=0)


def _level_reference(cum, h, reverse, sub):
    c = cum.shape[0]
    if 2 * h >= SUBLANES:
        rows = [p0 + (h if reverse else h - 1) for p0 in range(0, c, 2 * h)]
        return _row_bcast(cum, rows, 2 * h)
    out = None
    for p0 in reversed(range(0, SUBLANES, 2 * h)):
        r = p0 + (h if reverse else h - 1)
        piece = _row_bcast(cum, [v0 + r for v0 in range(0, c, SUBLANES)], SUBLANES)
        out = piece if out is None else jnp.where(sub < p0 + 2 * h, piece, out)
    return out


def _hg_chunks(chains, lb_ref, cum_ref, masks_ref, st_ref, inblock):
    c = HG_CHUNK
    sub = lax.broadcasted_iota(jnp.int32, (c, HG_DIM), 0) % SUBLANES
    work = []
    for head, d, q_ref, f_ref, i_ref, o_ref in chains:
        sl = slice(head * HG_DIM, (head + 1) * HG_DIM)
        lb = lb_ref[0, :, sl]
        hq = q_ref[0, :, sl].astype(F32)
        sig = _sigmoid(f_ref[0, :, sl].astype(F32))
        one_m_lb = 1.0 - lb
        w = dict(head=head, d=d, sl=sl, o_ref=o_ref, hi=i_ref[0, :, sl], q=hq * _sigmoid(hq),
                 g=jnp.log(jnp.maximum(lb, LB_FLOOR) + one_m_lb * sig),
                 kk=one_m_lb * (1.0 - sig))
        work.append(w)
    for w in work:
        g_hi, g_lo = _split_bf16(w["g"])
        w["cum"] = _dot(cum_ref[w["d"]], g_hi) + _dot(cum_ref[w["d"]], g_lo)
        w["q_bf"] = w["q"].astype(BF16)
        w["kk_bf"] = w["kk"].astype(BF16)
        w["a"] = jnp.zeros((c, c), F32)

    for li, h in enumerate(HG_LEVELS):
        if inblock and 2 * h <= HG_INBLOCK:
            continue
        for w in work:
            ref = _level_reference(w["cum"], h, w["d"] == 1, sub)
            e = jnp.exp2((jnp.abs(w["cum"] - ref) * (-LOG2E)).astype(BF16))
            w["a"] = w["a"] + _dot_nt(w["q_bf"] * e, w["kk_bf"] * e) * masks_ref[w["d"], li]
    for w in work:
        q, kk, cum, d = w["q"], w["kk"], w["cum"], w["d"]
        if inblock:
            first = HG_INBLOCK - 1 if d == 1 else 0
            ref = _row_bcast(cum, [p0 + first for p0 in range(0, c, HG_INBLOCK)], HG_INBLOCK)
            z = (cum - ref) * LOG2E
            al = _dot_nt((q * jnp.exp2(z)).astype(BF16), (kk * jnp.exp2(-z)).astype(BF16))
            w["a"] = w["a"] + al * masks_ref[d, MASK_INBLOCK]
        else:
            w["a"] = w["a"] + jnp.sum(q * kk, axis=-1, keepdims=True) * masks_ref[d, MASK_DIAG]
    for w in work:
        q, kk, cum, d, head = w["q"], w["kk"], w["cum"], w["d"], w["head"]
        hi = w["hi"]
        st = st_ref[d, head]
        tot = cum[0:1, :] if d == 1 else cum[c - 1:c, :]
        o = _dot(w["a"].astype(BF16), hi)
        o = o + _dot_nt((q * jnp.exp(cum)).astype(BF16), st.astype(BF16))
        k_end = (kk * jnp.exp(tot - cum)).astype(BF16)
        st_ref[d, head] = jnp.exp(tot) * st + _dot(hi.astype(F32).T.astype(BF16), k_end)
        w["o_ref"][0, :, w["sl"]] = o.astype(BF16)


def _hgrn_body(qf_ref, ff_ref, if_ref, qb_ref, fb_ref, ib_ref, lb_ref, cum_ref, masks_ref,
               of_ref, ob_ref, st_ref, *, inblock):
    @pl.when(pl.program_id(1) == 0)
    def _():
        st_ref[...] = jnp.zeros_like(st_ref)

    for heads in HG_HEAD_GROUPS:
        chains = [(head, d) + refs for head in heads
                  for d, refs in enumerate(((qf_ref, ff_ref, if_ref, of_ref),
                                            (qb_ref, fb_ref, ib_ref, ob_ref)))]
        _hg_chunks(chains, lb_ref, cum_ref, masks_ref, st_ref, inblock)


def _hgrn(proj, lbs, layer, fx_min):
    b, l, _ = proj.shape
    c = HG_CHUNK
    nc = l // c
    masks, cum = _hg_constants()
    fwd = lambda col: pl.BlockSpec((1, c, HG_WIDTH), lambda i, j: (i, j, col))
    bwd = lambda col: pl.BlockSpec((1, c, HG_WIDTH), lambda i, j: (i, nc - 1 - j, col))
    out_sd = jax.ShapeDtypeStruct((b, l, HG_WIDTH), BF16)

    def call(inblock):
        return pl.pallas_call(
            functools.partial(_hgrn_body, inblock=inblock),
            grid=(b, nc),
            in_specs=[fwd(COL_HQ), fwd(COL_HF_FWD), fwd(COL_HI),
                      bwd(COL_HQ), bwd(COL_HF_BWD), bwd(COL_HI),
                      pl.BlockSpec((1, 1, HG_WIDTH), lambda i, j: (layer, 0, 0)),
                      pl.BlockSpec(cum.shape, lambda i, j: (0, 0, 0)),
                      pl.BlockSpec(masks.shape, lambda i, j: (0, 0, 0, 0))],
            out_specs=[pl.BlockSpec((1, c, HG_WIDTH), lambda i, j: (i, j, 0)),
                       pl.BlockSpec((1, c, HG_WIDTH), lambda i, j: (i, nc - 1 - j, 0))],
            out_shape=[out_sd, out_sd],
            scratch_shapes=[pltpu.VMEM((2, HG_HEADS, HG_DIM, HG_DIM), F32)],
            compiler_params=_cparams("arbitrary", "arbitrary"),
            name="hgrn2_recurrence",
        )(proj, proj, proj, proj, proj, proj, lbs, cum, masks)

    lb = jnp.tile(lbs[layer], (1, 2))
    worst = jnp.log(jnp.maximum(lb, LB_FLOOR) + (1.0 - lb) * _sigmoid(fx_min))
    safe = (HG_INBLOCK - 1) * jnp.max(-worst) <= HG_INBLOCK_MAX_EXPONENT
    return lax.cond(safe, lambda: call(True), lambda: call(False))


def _rope_tables(l):
    lane = np.arange(LANES)
    dd = lane % HEAD_DIM
    axis = dd // ROPE_AXIS_DIM
    first_half = (dd % ROPE_AXIS_DIM) < (ROPE_AXIS_DIM // 2)
    freq_idx = dd % (ROPE_AXIS_DIM // 2)
    inv_freq = ROPE_THETA ** (-jnp.arange(0, ROPE_AXIS_DIM, 2, dtype=F32) / ROPE_AXIS_DIM)
    t = jnp.arange(l)
    pos = jnp.where(jnp.asarray(axis)[None, :] == 0, (t // GRID_W)[:, None], (t % GRID_W)[:, None])
    ang = pos.astype(F32) * inv_freq[jnp.asarray(freq_idx)][None, :]
    sign = jnp.where(jnp.asarray(first_half), -1.0, 1.0)[None, :]
    return jnp.cos(ang), jnp.sin(ang) * sign


def _norm_rope(x, gain, cos_t, sin_t, bd, first_half):
    x2_hi, x2_lo = _split_bf16(x * x)
    ss = _dot(x2_hi, bd) + _dot(x2_lo, bd)
    xn = x * lax.rsqrt(ss * (1.0 / HEAD_DIM) + NORM_EPS) * gain
    half = ROPE_AXIS_DIM // 2
    partner = jnp.where(first_half, pltpu.roll(xn, LANES - half, 1), pltpu.roll(xn, half, 1))
    return xn * cos_t + partner * sin_t


def _qkprep_body(q_ref, k_ref, v_ref, qg_ref, kg_ref, cos_ref, sin_ref, bd_ref, qt_ref, ko_ref, vt_ref):
    tq = qt_ref.shape[4] // GQA_GROUP
    cos_t = cos_ref[...]
    sin_t = sin_ref[...]
    bd = bd_ref[...]
    lane = lax.broadcasted_iota(jnp.int32, cos_t.shape, 1)
    first_half = (lane % ROPE_AXIS_DIM) < (ROPE_AXIS_DIM // 2)
    zeros = jnp.zeros((HEAD_DIM, tq), BF16)
    for j in range(ATTN_WIDTH // LANES):
        sl = slice(j * LANES, (j + 1) * LANES)
        qr = _norm_rope(q_ref[0, :, sl].astype(F32), qg_ref[0], cos_t, sin_t, bd, first_half)
        qr = qr * (HEAD_DIM ** -0.5 * LOG2E)
        h = (2 * j) // GQA_GROUP
        for u in range(qt_ref.shape[2]):
            t = qr[u * tq:(u + 1) * tq, :].T.astype(BF16)
            for e in range(2):
                piece = t[e * HEAD_DIM:(e + 1) * HEAD_DIM, :]
                blockcol = jnp.concatenate([piece, zeros] if h == 0 else [zeros, piece], axis=0)
                g = (2 * j + e) % GQA_GROUP
                qt_ref[0, h, u, :, g * tq:(g + 1) * tq] = blockcol
    kr = _norm_rope(k_ref[0].astype(F32), kg_ref[0], cos_t, sin_t, bd, first_half)
    ko_ref[0] = kr.astype(BF16)
    vt_ref[0, 0] = v_ref[0].astype(F32).T.astype(BF16)


def _qk_prep(proj, q_gain2, k_gain2, layer):
    b, l, _ = proj.shape
    tk = min(KV_BLOCK, l)
    tq = min(Q_TILE, l)
    cos_t, sin_t = _rope_tables(l)
    blk = np.arange(LANES) // HEAD_DIM
    bd = jnp.asarray((blk[:, None] == blk[None, :]).astype(np.float32), dtype=BF16)
    return pl.pallas_call(
        _qkprep_body,
        grid=(b, l // tk),
        in_specs=[pl.BlockSpec((1, tk, ATTN_WIDTH), lambda i, j: (i, j, COL_AQ)),
                  pl.BlockSpec((1, tk, KV_WIDTH), lambda i, j: (i, j, COL_AK)),
                  pl.BlockSpec((1, tk, KV_WIDTH), lambda i, j: (i, j, COL_AV)),
                  pl.BlockSpec((1, 1, LANES), lambda i, j: (layer, 0, 0)),
                  pl.BlockSpec((1, 1, LANES), lambda i, j: (layer, 0, 0)),
                  pl.BlockSpec((tk, LANES), lambda i, j: (j, 0)),
                  pl.BlockSpec((tk, LANES), lambda i, j: (j, 0)),
                  pl.BlockSpec((LANES, LANES), lambda i, j: (0, 0))],
        out_specs=[pl.BlockSpec((1, N_KV_HEADS, tk // tq, KV_WIDTH, GQA_GROUP * tq),
                                lambda i, j: (i, 0, j, 0, 0)),
                   pl.BlockSpec((1, tk, KV_WIDTH), lambda i, j: (i, j, 0)),
                   pl.BlockSpec((1, 1, KV_WIDTH, tk), lambda i, j: (i, j, 0, 0))],
        out_shape=[jax.ShapeDtypeStruct((b, N_KV_HEADS, l // tq, KV_WIDTH, GQA_GROUP * tq), BF16),
                   jax.ShapeDtypeStruct((b, l, KV_WIDTH), BF16),
                   jax.ShapeDtypeStruct((b, l // tk, KV_WIDTH, tk), BF16)],
        compiler_params=_cparams("arbitrary", "arbitrary"),
        name="qk_norm_rope",
    )(proj, proj, proj, q_gain2, k_gain2, cos_t, sin_t, bd)


def _attn_finish(acc_scr, l_fin, o_ref):
    tq = o_ref.shape[1]
    outs = []
    for h in range(N_KV_HEADS):
        o = (acc_scr[h] / l_fin[h]).T
        oh = o[:, h * HEAD_DIM:(h + 1) * HEAD_DIM]
        outs += [oh[g * tq:(g + 1) * tq, :] for g in range(GQA_GROUP)]
    o_ref[0] = jnp.concatenate(outs, axis=1).astype(BF16)


def _attn_bounded_body(bound_ref, qt_ref, k_ref, vt_ref, o_ref, acc_scr):
    nblk = vt_ref.shape[1]
    tk = vt_ref.shape[3]
    cols = qt_ref.shape[4]
    shift = bound_ref[0, 0]
    acc_scr[...] = jnp.zeros_like(acc_scr)

    def step(i, sums):
        kb = k_ref[0, pl.ds(pl.multiple_of(i * tk, tk), tk), :]
        vt = vt_ref[0, i]
        scores = [_dot(kb, qt_ref[0, h, 0]) for h in range(N_KV_HEADS)]
        new = []
        for h in range(N_KV_HEADS):
            p = jnp.exp2(scores[h] - shift)
            new.append(sums[h] + jnp.sum(p, axis=0, keepdims=True))
            acc_scr[h] = acc_scr[h] + _dot(vt, p.astype(BF16))
        return tuple(new)

    init = tuple(jnp.zeros((1, cols), F32) for _ in range(N_KV_HEADS))
    _attn_finish(acc_scr, lax.fori_loop(0, nblk, step, init, unroll=4), o_ref)


def _attn_online_body(bound_ref, qt_ref, k_ref, vt_ref, o_ref, acc_scr):
    nblk = vt_ref.shape[1]
    tk = vt_ref.shape[3]
    cols = qt_ref.shape[4]
    acc_scr[...] = jnp.zeros_like(acc_scr)
    l_fin = []
    for h in range(N_KV_HEADS):
        qt = qt_ref[0, h, 0]

        def step(i, carry):
            m_prev, l_prev = carry
            kb = k_ref[0, pl.ds(pl.multiple_of(i * tk, tk), tk), :]
            s = _dot(kb, qt)
            m_new = jnp.maximum(m_prev, jnp.max(s, axis=0, keepdims=True))
            alpha = jnp.exp2(m_prev - m_new)
            p = jnp.exp2(s - m_new)
            l_new = alpha * l_prev + jnp.sum(p, axis=0, keepdims=True)
            acc_scr[h] = alpha * acc_scr[h] + _dot(vt_ref[0, i], p.astype(BF16))
            return m_new, l_new

        init = (jnp.full((1, cols), -jnp.inf, F32), jnp.zeros((1, cols), F32))
        l_fin.append(lax.fori_loop(0, nblk, step, init)[1])
    _attn_finish(acc_scr, l_fin, o_ref)


def _attention(qt, k_rot, vt, bound):
    b, l, _ = k_rot.shape
    tq = qt.shape[4] // GQA_GROUP
    nblk, tk = vt.shape[1], vt.shape[3]

    def call(body):
        return pl.pallas_call(
            body,
            grid=(b, l // tq),
            in_specs=[pl.BlockSpec(memory_space=pltpu.SMEM),
                      pl.BlockSpec((1, N_KV_HEADS, 1, KV_WIDTH, GQA_GROUP * tq), lambda i, j: (i, 0, j, 0, 0)),
                      pl.BlockSpec((1, l, KV_WIDTH), lambda i, j: (i, 0, 0)),
                      pl.BlockSpec((1, nblk, KV_WIDTH, tk), lambda i, j: (i, 0, 0, 0))],
            out_specs=pl.BlockSpec((1, tq, ATTN_WIDTH), lambda i, j: (i, j, 0)),
            out_shape=jax.ShapeDtypeStruct((b, l, ATTN_WIDTH), BF16),
            scratch_shapes=[pltpu.VMEM((N_KV_HEADS, KV_WIDTH, GQA_GROUP * tq), F32)],
            compiler_params=_cparams("arbitrary", "arbitrary"),
            name="gqa_attention",
        )(bound, qt, k_rot, vt)

    return lax.cond(bound[0, 0] <= SAFE_SCORE_BOUND,
                    lambda: call(_attn_bounded_body), lambda: call(_attn_online_body))


def _route(logits):
    lane = lax.broadcasted_iota(jnp.int32, logits.shape, 1).astype(F32)
    neg = jnp.float32(-jnp.inf)
    lanemin = lambda cond: jnp.min(jnp.where(cond, lane, float(ROUTE_LANES)), axis=-1, keepdims=True)
    gl = jnp.where(lane < N_GROUPS, logits, neg)
    gmax = jnp.max(gl, axis=-1, keepdims=True)
    g_idx = lanemin(gl == gmax)
    pg_top = 1.0 / jnp.sum(jnp.exp(gl - gmax), axis=-1, keepdims=True)
    first = N_GROUPS + g_idx * EXPERTS_PER_GROUP
    in_group = (lane >= first) & (lane < first + EXPERTS_PER_GROUP)
    el = jnp.where(in_group, logits, neg)
    a_max = jnp.max(el, axis=-1, keepdims=True)
    a_idx = lanemin(el == a_max)
    el2 = jnp.where(lane == a_idx, neg, el)
    b_max = jnp.max(el2, axis=-1, keepdims=True)
    b_idx = lanemin(el2 == b_max)
    r = jnp.exp(b_max - a_max)
    w_a = pg_top / (1.0 + r)
    w_b = pg_top * r / (1.0 + r)
    return a_idx - N_GROUPS, b_idx - N_GROUPS, w_a, w_b


def _outproj_body(of_ref, ob_ref, hg_ref, at_ref, x_ref, mod_ref, hgain_ref, wout_ref, fgain_ref,
                  wr_ref, br_ref, tri_ref, xo_ref, h_ref, rt_ref, cnt_ref, cnt_scr, *, seg):
    tm = x_ref.shape[1]
    tile = pl.program_id(0) * pl.num_programs(1) + pl.program_id(1)

    @pl.when(tile == 0)
    def _():
        cnt_scr[...] = jnp.zeros_like(cnt_scr)

    parts = []
    for head in range(HG_HEADS):
        sl = slice(head * HG_DIM, (head + 1) * HG_DIM)
        o = of_ref[0, :, sl].astype(F32) + ob_ref[0, :, sl].astype(F32)
        o = o * lax.rsqrt(jnp.mean(o * o, axis=-1, keepdims=True) + NORM_EPS) * hgain_ref[0]
        gt = hg_ref[0, :, sl].astype(F32)
        parts.append((o * (gt * _sigmoid(gt))).astype(BF16))
    o_hg = jnp.concatenate(parts, axis=1)
    mix = _dot(o_hg, wout_ref[0, :HG_WIDTH, :]) + _dot(at_ref[0], wout_ref[0, HG_WIDTH:, :])
    x = x_ref[0] + mod_ref[0, 2:3, :] * mix
    xo_ref[0] = x
    y = x * lax.rsqrt(jnp.mean(x * x, axis=-1, keepdims=True) + NORM_EPS) * fgain_ref[0]
    h = y * (1.0 + mod_ref[0, 4:5, :]) + mod_ref[0, 3:4, :]
    h_ref[0] = h
    h_hi, h_lo = _split_bf16(h)
    both = _dot(h_hi, wr_ref[0])
    logits = (both[:, :ROUTE_LANES] + both[:, ROUTE_LANES:] + _dot(h_lo, wr_ref[0, :, :ROUTE_LANES])
              + br_ref[0])
    e_a, e_b, w_a, w_b = _route(logits)

    lane = lax.broadcasted_iota(jnp.int32, logits.shape, 1).astype(F32)
    group0 = (((tile * tm) // seg) * N_EXPERTS).astype(F32)
    hot_a = jnp.where(lane == e_a + group0, 1.0, 0.0)
    hot_b = jnp.where(lane == e_b + group0, 1.0, 0.0)
    hot = hot_a + hot_b
    before = cnt_scr[...] + _dot(tri_ref[...], hot.astype(BF16))
    rank_a = jnp.sum(hot_a * before, axis=-1, keepdims=True)
    rank_b = jnp.sum(hot_b * before, axis=-1, keepdims=True)
    cnt_scr[...] = cnt_scr[...] + jnp.sum(hot, axis=0, keepdims=True)
    cnt_ref[...] = cnt_scr[...]
    cols = (e_a, e_b, w_a, w_b, rank_a, rank_b)
    out = jnp.zeros_like(logits)
    for c, v in enumerate(cols):
        out = jnp.where(lane == float(c), v, out)
    rt_ref[0] = out


def _out_proj(o_f, o_b, proj, o_at, x, modr, mod_row0, hgain, w_out_bf, fgain, w_route, b_route, layer,
              seg):
    b, l, _ = x.shape
    tm = min(ROW_TILE, l)
    tok = lambda w, col=0: pl.BlockSpec((1, tm, w), lambda i, j: (i, j, col))
    lay = lambda *s: pl.BlockSpec((1,) + s, lambda i, j: (layer,) + (0,) * len(s))
    tri = jnp.asarray(np.tril(np.ones((tm, tm), np.float32), -1), dtype=BF16)
    return pl.pallas_call(
        functools.partial(_outproj_body, seg=seg),
        grid=(b, l // tm),
        in_specs=[tok(HG_WIDTH), tok(HG_WIDTH), tok(HG_WIDTH, COL_HG), tok(ATTN_WIDTH), tok(D_MODEL),
                  pl.BlockSpec((1, N_MOD, D_MODEL), lambda i, j: (mod_row0 + i, 0, 0)),
                  lay(1, HG_DIM), lay(D_MODEL, D_MODEL), lay(1, D_MODEL),
                  lay(D_MODEL, 2 * ROUTE_LANES), lay(1, ROUTE_LANES),
                  pl.BlockSpec((tm, tm), lambda i, j: (0, 0))],
        out_specs=[tok(D_MODEL), tok(D_MODEL), tok(ROUTE_LANES),
                   pl.BlockSpec((1, ROUTE_LANES), lambda i, j: (0, 0))],
        out_shape=[jax.ShapeDtypeStruct((b, l, D_MODEL), F32),
                   jax.ShapeDtypeStruct((b, l, D_MODEL), F32),
                   jax.ShapeDtypeStruct((b, l, ROUTE_LANES), F32),
                   jax.ShapeDtypeStruct((1, ROUTE_LANES), F32)],
        scratch_shapes=[pltpu.VMEM((1, ROUTE_LANES), F32)],
        compiler_params=_cparams("arbitrary", "arbitrary"),
        name="out_proj_router",
    )(o_f, o_b, proj, o_at, x, modr, hgain, w_out_bf, fgain, w_route, b_route, tri)


def _moe_body(te_ref, tv_ref, tu_ref, tok_ref, h_ref, wg0_ref, wu0_ref, wd0_ref, wg1_ref, wu1_ref,
              wd1_ref, o_ref, xs_scr, *, steps_per_seg):
    step = pl.program_id(0)
    tr = xs_scr.shape[2]
    cur = step % 2
    weights = ((wg0_ref, wu0_ref, wd0_ref), (wg1_ref, wu1_ref, wd1_ref))

    def ffn(u):
        wg_ref, wu_ref, wd_ref = weights[u]
        xs = xs_scr[cur, u].astype(BF16)
        gt = _dot(xs, wg_ref[0])
        up = _dot(xs, wu_ref[0])
        hid = (gt * _sigmoid(gt)) * up
        return _dot(hid.astype(BF16), wd_ref[0])

    valid0 = tv_ref[2 * step] != 0
    valid1 = tv_ref[2 * step + 1] != 0

    @pl.when(jnp.logical_and(valid0, step % steps_per_seg == 0))
    def _():
        for u in range(2):
            first = tu_ref[2 * step + u]

            def row(r, carry):
                xs_scr[cur, u, pl.ds(r, 1), :] = h_ref[0, pl.ds(tok_ref[first + r], 1), :]
                return carry

            lax.fori_loop(0, tr, row, 0, unroll=8)

    @pl.when(valid1)
    def _():
        nxt = jnp.minimum(step + 1, pl.num_programs(0) - 1)
        for u in range(2):
            first = tu_ref[2 * nxt + u]
            for r in range(tr):
                xs_scr[1 - cur, u, r:r + 1, :] = h_ref[0, pl.ds(tok_ref[first + r], 1), :]
        o_ref[:tr, :] = ffn(0)
        o_ref[tr:, :] = ffn(1)

    @pl.when(jnp.logical_and(valid0, jnp.logical_not(valid1)))
    def _():
        o_ref[:tr, :] = ffn(0)
        o_ref[tr:, :] = jnp.zeros((tr, o_ref.shape[1]), F32)

    @pl.when(jnp.logical_not(valid0))
    def _():
        o_ref[...] = jnp.zeros_like(o_ref)


def _moe_ffn(h_seg, sorted_tok, tile_expert, tile_valid, tile_first, wg, wu, wd, layer):
    nseg, seg, d = h_seg.shape
    n_tiles = tile_expert.shape[0]
    tr = EXPERT_TILE
    steps_per_seg = n_tiles // nseg // 2
    wspec = lambda k, n, u: pl.BlockSpec(
        (1, k, n), lambda i, te, *_: (layer * N_EXPERTS + te[2 * i + u], 0, 0))
    wspecs = [wspec(D_MODEL, D_EXPERT, u) if j < 2 else wspec(D_EXPERT, D_MODEL, u)
              for u in range(2) for j in range(3)]
    return pl.pallas_call(
        functools.partial(_moe_body, steps_per_seg=steps_per_seg),
        grid_spec=pltpu.PrefetchScalarGridSpec(
            num_scalar_prefetch=4,
            grid=(n_tiles // 2,),
            in_specs=[pl.BlockSpec((1, seg, d), lambda i, *_: (i // steps_per_seg, 0, 0))] + wspecs,
            out_specs=pl.BlockSpec((2 * tr, d), lambda i, *_: (i, 0)),
            scratch_shapes=[pltpu.VMEM((2, 2, tr, d), F32)]),
        out_shape=jax.ShapeDtypeStruct((n_tiles * tr, d), F32),
        compiler_params=_cparams("arbitrary"),
        name="routed_expert_ffn",
    )(tile_expert, tile_valid, tile_first, sorted_tok, h_seg, wg, wu, wd, wg, wu, wd)


def _combine_body(pa_ref, pb_ref, rows_ref, rt_ref, x_ref, mod_ref, fin_ref, o_ref, a_scr, b_scr, *,
                  final):
    tm = o_ref.shape[1]

    def gather(r, carry):
        a_scr[pl.ds(r, 1), :] = rows_ref[0, pl.ds(pa_ref[0, 0, r], 1), :]
        b_scr[pl.ds(r, 1), :] = rows_ref[0, pl.ds(pb_ref[0, 0, r], 1), :]
        return carry

    lax.fori_loop(0, tm, gather, 0, unroll=8)
    rt = rt_ref[0]
    moe = rt[:, 2:3] * a_scr[...] + rt[:, 3:4] * b_scr[...]
    x = x_ref[0] + mod_ref[0, 5:6, :] * moe
    if final:
        x = x * lax.rsqrt(jnp.mean(x * x, axis=-1, keepdims=True) + NORM_EPS) * fin_ref[...]
    o_ref[0] = x


def _combine(pos_a, pos_b, rows_seg, route, x, modr, mod_row0, final_gain, final):
    b, l, _ = x.shape
    nseg, rseg, d = rows_seg.shape
    tm = min(COMBINE_TILE, l)
    tiles_l = l // tm
    seg_tiles = (b * tiles_l) // nseg
    flat = lambda i, j: i * tiles_l + j
    tok = lambda w: pl.BlockSpec((1, tm, w), lambda i, j: (i, j, 0))
    smem = pl.BlockSpec((1, 1, tm), lambda i, j: (flat(i, j), 0, 0), memory_space=pltpu.SMEM)
    return pl.pallas_call(
        functools.partial(_combine_body, final=final),
        grid=(b, tiles_l),
        in_specs=[smem, smem,
                  pl.BlockSpec((1, rseg, d), lambda i, j: (flat(i, j) // seg_tiles, 0, 0),
                               pipeline_mode=pl.Buffered(1)),
                  tok(ROUTE_LANES), tok(D_MODEL),
                  pl.BlockSpec((1, N_MOD, D_MODEL), lambda i, j: (mod_row0 + i, 0, 0)),
                  pl.BlockSpec((1, D_MODEL), lambda i, j: (0, 0))],
        out_specs=tok(D_MODEL),
        out_shape=jax.ShapeDtypeStruct((b, l, D_MODEL), F32),
        scratch_shapes=[pltpu.VMEM((tm, d), F32), pltpu.VMEM((tm, d), F32)],
        compiler_params=_cparams("arbitrary", "arbitrary"),
        name="moe_combine",
    )(pos_a.reshape(-1, 1, tm), pos_b.reshape(-1, 1, tm), rows_seg, route, x, modr, final_gain)


def _routing_tables(route, counts, nseg, seg):
    t = route.shape[0]
    tr = EXPERT_TILE
    rseg = 2 * seg + N_EXPERTS * tr
    ngrp = nseg * N_EXPERTS
    cnt = counts[0, :ngrp].astype(jnp.int32)
    pad2 = (((cnt + tr - 1) // tr) * tr).reshape(nseg, N_EXPERTS)
    end_local = jnp.cumsum(pad2, axis=1)
    start_local = (end_local - pad2).reshape(-1)
    start_unp = jnp.cumsum(cnt) - cnt
    ids = route[:, 0:2].astype(jnp.int32)
    rank = route[:, 4:6].astype(jnp.int32)
    key = (jnp.arange(t, dtype=jnp.int32) // seg)[:, None] * N_EXPERTS + ids
    hot = key[:, :, None] == jnp.arange(ngrp, dtype=jnp.int32)[None, None, :]
    lookup = lambda table: jnp.sum(jnp.where(hot, table[None, None, :], 0), axis=-1)
    pos = lookup(start_local) + rank
    order = lookup(start_unp) + rank
    tok = jnp.broadcast_to((jnp.arange(t, dtype=jnp.int32) % seg)[:, None], (t, 2))
    _, sorted_tok = lax.sort((order.reshape(-1), tok.reshape(-1)), num_keys=1)
    sorted_tok = jnp.concatenate([sorted_tok, jnp.zeros((tr,), jnp.int32)])
    tiles_per_seg = rseg // tr
    local_row = jnp.arange(tiles_per_seg, dtype=jnp.int32) * tr
    tile_grp = jnp.sum(local_row[None, :, None] >= end_local[:, None, :], axis=-1)
    tile_valid = (tile_grp < N_EXPERTS).astype(jnp.int32)
    tile_expert = jnp.minimum(tile_grp, N_EXPERTS - 1).astype(jnp.int32)
    grp = tile_expert + (jnp.arange(nseg, dtype=jnp.int32) * N_EXPERTS)[:, None]
    hot_t = grp[:, :, None] == jnp.arange(ngrp, dtype=jnp.int32)[None, None, :]
    skipped_pad = jnp.sum(jnp.where(hot_t, (start_local - start_unp)[None, None, :], 0), axis=-1)
    tile_first = (local_row[None, :] - skipped_pad) * tile_valid
    return (sorted_tok, pos[:, 0], pos[:, 1], tile_expert.reshape(-1), tile_valid.reshape(-1),
            tile_first.reshape(-1), rseg)


def _trunk(x, modr, mod_row0, nb, params):
    b, l, _ = x.shape
    t = b * l
    seg = min(MOE_SEGMENT, t)
    nseg = t // seg
    depth = params["w_in"].shape[0]
    for layer in range(depth):
        row0 = layer * nb + mod_row0
        proj, fx_min = _in_proj(x, modr, row0, params["norm_mix"], params["w_in"], layer)
        o_f, o_b = _hgrn(proj, params["lbs"], layer, fx_min)
        qt, k_rot, vt = _qk_prep(proj, params["q_gain2"], params["k_gain2"], layer)
        o_at = _attention(qt, k_rot, vt, params["score_bound"][layer])
        x_mid, h, route, counts = _out_proj(o_f, o_b, proj, o_at, x, modr, row0, params["hg_gain"],
                                            params["w_out"], params["norm_ffn"], params["w_route"],
                                            params["b_route"], layer, seg)
        sorted_tok, pos_a, pos_b, tile_e, tile_v, tile_f, rseg = _routing_tables(
            route.reshape(t, ROUTE_LANES), counts, nseg, seg)
        rows = _moe_ffn(h.reshape(nseg, seg, D_MODEL), sorted_tok, tile_e, tile_v, tile_f,
                        params["w_e_gate"], params["w_e_up"], params["w_e_down"], layer)
        x = _combine(pos_a, pos_b, rows.reshape(nseg, rseg, D_MODEL), route, x_mid, modr, row0,
                     params["final_norm"], final=(layer == depth - 1))
    return x


def kernel(x_prompt, x_sample, c_prompt, c_sample, w_in, w_out, hg_lb, hg_out_norm, q_norm, k_norm,
           norm_mix, norm_ffn, w_ada, b_ada, w_group, b_group, w_router, b_router, w_e_gate, w_e_up,
           w_e_down, final_norm):
    depth = w_in.shape[0]
    bp, bs = c_prompt.shape[0], c_sample.shape[0]
    nb = -(-(bp + bs) // SUBLANES) * SUBLANES
    c_all = jnp.zeros((nb, D_MODEL), F32).at[:bp].set(c_prompt).at[bp:bp + bs].set(c_sample)
    mod = _modulation(c_all, w_ada, b_ada)
    modr = mod.reshape(depth * nb, N_MOD, D_MODEL)
    pad = ROUTE_LANES - N_GROUPS - N_EXPERTS
    params = {
        "w_in": w_in.astype(BF16),
        "w_out": w_out.astype(BF16),
        "lbs": _lower_bounds(hg_lb).reshape(depth, 1, HG_WIDTH),
        "hg_gain": hg_out_norm.reshape(depth, 1, HG_DIM),
        "q_gain2": jnp.tile(q_norm, (1, LANES // HEAD_DIM)).reshape(depth, 1, LANES),
        "k_gain2": jnp.tile(k_norm, (1, LANES // HEAD_DIM)).reshape(depth, 1, LANES),
        "score_bound": (1.02 * LOG2E * HEAD_DIM ** 0.5 * jnp.max(jnp.abs(q_norm), axis=1)
                        * jnp.max(jnp.abs(k_norm), axis=1)).reshape(depth, 1, 1),
        "norm_mix": norm_mix.reshape(depth, 1, D_MODEL),
        "norm_ffn": norm_ffn.reshape(depth, 1, D_MODEL),
        "w_route": jnp.concatenate(_split_bf16(jnp.pad(jnp.concatenate([w_group, w_router], axis=-1),
                                                       ((0, 0), (0, 0), (0, pad)))), axis=-1),
        "b_route": jnp.pad(jnp.concatenate([b_group, b_router], axis=-1),
                           ((0, 0), (0, pad))).reshape(depth, 1, ROUTE_LANES),
        "w_e_gate": w_e_gate.astype(BF16).reshape(depth * N_EXPERTS, D_MODEL, D_EXPERT),
        "w_e_up": w_e_up.astype(BF16).reshape(depth * N_EXPERTS, D_MODEL, D_EXPERT),
        "w_e_down": w_e_down.astype(BF16).reshape(depth * N_EXPERTS, D_EXPERT, D_MODEL),
        "final_norm": final_norm.reshape(1, D_MODEL),
    }
    y_prompt = _trunk(x_prompt, modr, 0, nb, params)
    y_sample = _trunk(x_sample, modr, bp, nb, params)
    return (y_prompt, y_sample)
```

```python
import functools

import numpy as np
import jax
import jax.numpy as jnp
from jax import lax
from jax.experimental import pallas as pl
from jax.experimental.pallas import tpu as pltpu

F32 = jnp.float32
BF16 = jnp.bfloat16

D_MODEL = 1024
HG_WIDTH = 512
HG_HEADS = 4
HG_DIM = 128
LB_FLOOR = 1e-30
ATTN_WIDTH = 512
HEAD_DIM = 64
N_Q_HEADS = 8
N_KV_HEADS = 2
GQA_GROUP = 4
KV_WIDTH = 128
GRID_W = 64
ROPE_AXIS_DIM = 32
ROPE_THETA = 10000.0
N_GROUPS = 4
EXPERTS_PER_GROUP = 4
N_EXPERTS = 16
D_EXPERT = 512
N_MOD = 6
NORM_EPS = 1e-6
IN_PROJ_WIDTH = 5 * HG_WIDTH + ATTN_WIDTH + 2 * KV_WIDTH
COL_HQ, COL_HF_FWD, COL_HF_BWD, COL_HI, COL_HG, COL_AQ = 0, 1, 2, 3, 4, 5
COL_AK, COL_AV = 24, 25

LANES = 128
SUBLANES = 8
VMEM_LIMIT = 56 * 1024 * 1024
ROW_TILE = 512
OUTPROJ_ROWS = 512
HG_CHUNK = 128
KV_BLOCK = 512
Q_TILE = 256
EXPERT_TILE = 128
FFN_TILES_PER_STEP = 4
COMBINE_TILE = 256
MOE_SEGMENT = 4096
ROUTE_LANES = 128
LOG2E = 1.4426950408889634
SAFE_SCORE_BOUND = 40.0


def _cparams(*sem):
    return pltpu.CompilerParams(dimension_semantics=sem, vmem_limit_bytes=VMEM_LIMIT)


def _dot(a, b):
    return jnp.dot(a, b, preferred_element_type=F32)


def _dot_nt(a, b):
    return lax.dot_general(a, b, (((1,), (1,)), ((), ())), preferred_element_type=F32)


def _split_bf16(x):
    hi = x.astype(BF16)
    lo = (x - hi.astype(F32)).astype(BF16)
    return hi, lo


def _sigmoid(x):
    return 1.0 / (1.0 + jnp.exp(-x))


def _mod_body(c_ref, w_ref, b_ref, o_ref):
    c = c_ref[...]
    cs = c * _sigmoid(c)
    o_ref[0] = _dot(cs.astype(BF16), w_ref[0].astype(BF16)) + b_ref[0]


def _modulation(c_all, w_ada, b_ada):
    nb = c_all.shape[0]
    depth, _, n = w_ada.shape
    tn = 1024
    return pl.pallas_call(
        _mod_body,
        grid=(depth, n // tn),
        in_specs=[pl.BlockSpec((nb, D_MODEL), lambda l, j: (0, 0)),
                  pl.BlockSpec((1, D_MODEL, tn), lambda l, j: (l, 0, j)),
                  pl.BlockSpec((1, 1, tn), lambda l, j: (l, 0, j))],
        out_specs=pl.BlockSpec((1, nb, tn), lambda l, j: (l, 0, j)),
        out_shape=jax.ShapeDtypeStruct((depth, nb, n), F32),
        compiler_params=_cparams("arbitrary", "arbitrary"),
        name="adaln_modulation",
    )(c_all, w_ada, b_ada.reshape(depth, 1, n))


def _lb_body(p_ref, o_ref):
    depth = p_ref.shape[0]
    rows = [p_ref[l:l + 1, :] for l in range(depth)]
    m = functools.reduce(jnp.maximum, rows)
    es = [jnp.exp(r - m) for r in rows]
    tot = functools.reduce(lambda a, b: a + b, es)
    sm = [e / tot for e in es]
    acc = jnp.zeros_like(sm[0])
    for l in range(depth):
        acc = acc + sm[l]
        o_ref[l:l + 1, :] = acc - sm[0]


def _lower_bounds(hg_lb):
    return pl.pallas_call(
        _lb_body,
        out_shape=jax.ShapeDtypeStruct(hg_lb.shape, F32),
        name="hgrn_lower_bounds",
    )(hg_lb)


def _inproj_body(x_ref, mod_ref, gain_ref, w_ref, p_ref, fmin_ref):
    x = x_ref[0]
    ms = jnp.mean(x * x, axis=-1, keepdims=True)
    y = x * lax.rsqrt(ms + NORM_EPS) * gain_ref[0]
    h = y * (1.0 + mod_ref[0, 1:2, :]) + mod_ref[0, 0:1, :]
    proj = _dot(h.astype(BF16), w_ref[0]).astype(BF16)
    p_ref[0] = proj
    lo, hi = COL_HF_FWD * HG_WIDTH, (COL_HF_BWD + 1) * HG_WIDTH
    cur = jnp.min(proj[:, lo:hi].astype(F32), axis=0, keepdims=True)
    first = jnp.logical_and(pl.program_id(0) == 0, pl.program_id(1) == 0)

    @pl.when(first)
    def _():
        fmin_ref[...] = cur

    @pl.when(jnp.logical_not(first))
    def _():
        fmin_ref[...] = jnp.minimum(fmin_ref[...], cur)


def _in_proj(x, modr, mod_row0, gain, w_in_bf, layer):
    b, l, _ = x.shape
    tm = min(ROW_TILE, l)
    return pl.pallas_call(
        _inproj_body,
        grid=(b, l // tm),
        in_specs=[pl.BlockSpec((1, tm, D_MODEL), lambda i, j: (i, j, 0)),
                  pl.BlockSpec((1, N_MOD, D_MODEL), lambda i, j: (mod_row0 + i, 0, 0)),
                  pl.BlockSpec((1, 1, D_MODEL), lambda i, j: (layer, 0, 0)),
                  pl.BlockSpec((1, D_MODEL, IN_PROJ_WIDTH), lambda i, j: (layer, 0, 0))],
        out_specs=[pl.BlockSpec((1, tm, IN_PROJ_WIDTH), lambda i, j: (i, j, 0)),
                   pl.BlockSpec((1, 2 * HG_WIDTH), lambda i, j: (0, 0))],
        out_shape=[jax.ShapeDtypeStruct((b, l, IN_PROJ_WIDTH), BF16),
                   jax.ShapeDtypeStruct((1, 2 * HG_WIDTH), F32)],
        compiler_params=_cparams("arbitrary", "arbitrary"),
        name="in_proj",
    )(x, modr, gain, w_in_bf)


HG_LEVELS = (64, 32, 16, 8, 4, 2, 1)
HG_INBLOCK = 8
HG_INBLOCK_MAX_EXPONENT = 80.0
HG_HEAD_GROUPS = ((0, 1, 2, 3),)
MASK_DIAG = len(HG_LEVELS)
MASK_INBLOCK = len(HG_LEVELS) + 1


def _hg_constants():
    c = HG_CHUNK
    t = np.arange(c)[:, None]
    s = np.arange(c)[None, :]
    masks = np.zeros((2, len(HG_LEVELS) + 2, c, c), np.float32)
    for li, h in enumerate(HG_LEVELS):
        same = (t // (2 * h)) == (s // (2 * h))
        masks[0, li] = same & ((t % (2 * h)) >= h) & ((s % (2 * h)) < h)
        masks[1, li] = same & ((t % (2 * h)) < h) & ((s % (2 * h)) >= h)
    masks[:, MASK_DIAG] = (t == s)
    same = (t // HG_INBLOCK) == (s // HG_INBLOCK)
    masks[0, MASK_INBLOCK] = same & (s <= t)
    masks[1, MASK_INBLOCK] = same & (s >= t)
    cum = np.stack([(s <= t), (s >= t)]).astype(np.float32)
    return jnp.asarray(masks), jnp.asarray(cum, dtype=BF16)


def _row_bcast(a, rows, blk):
    parts = [jnp.broadcast_to(a[r:r + 1, :], (blk, a.shape[1])) for r in rows]
    return parts[0] if len(parts) == 1 else jnp.concatenate(parts, axis=0)


def _level_reference(cum, h, reverse, sub):
    c = cum.shape[0]
    if 2 * h >= SUBLANES:
        rows = [p0 + (h if reverse else h - 1) for p0 in range(0, c, 2 * h)]
        return _row_bcast(cum, rows, 2 * h)
    out = None
    for p0 in reversed(range(0, SUBLANES, 2 * h)):
        r = p0 + (h if reverse else h - 1)
        piece = _row_bcast(cum, [v0 + r for v0 in range(0, c, SUBLANES)], SUBLANES)
        out = piece if out is None else jnp.where(sub < p0 + 2 * h, piece, out)
    return out


def _hg_chunks(chains, lb_ref, cum_ref, masks_ref, st_ref, inblock):
    c = HG_CHUNK
    sub = lax.broadcasted_iota(jnp.int32, (c, HG_DIM), 0) % SUBLANES
    work = []
    for head, d, q_ref, f_ref, i_ref, o_ref in chains:
        sl = slice(head * HG_DIM, (head + 1) * HG_DIM)
        lb = lb_ref[0, :, sl]
        hq = q_ref[0, :, sl].astype(F32)
        sig = _sigmoid(f_ref[0, :, sl].astype(F32))
        one_m_lb = 1.0 - lb
        w = dict(head=head, d=d, sl=sl, o_ref=o_ref, hi=i_ref[0, :, sl], q=hq * _sigmoid(hq),
                 g=jnp.log(jnp.maximum(lb, LB_FLOOR) + one_m_lb * sig),
                 kk=one_m_lb * (1.0 - sig))
        work.append(w)
    for w in work:
        g_hi, g_lo = _split_bf16(w["g"])
        w["cum"] = _dot(cum_ref[w["d"]], g_hi) + _dot(cum_ref[w["d"]], g_lo)
        w["q_bf"] = w["q"].astype(BF16)
        w["kk_bf"] = w["kk"].astype(BF16)
        w["a"] = jnp.zeros((c, c), F32)

    for li, h in enumerate(HG_LEVELS):
        if inblock and 2 * h <= HG_INBLOCK:
            continue
        for w in work:
            ref = _level_reference(w["cum"], h, w["d"] == 1, sub)
            e = jnp.exp2((jnp.abs(w["cum"] - ref) * (-LOG2E)).astype(BF16))
            w["a"] = w["a"] + _dot_nt(w["q_bf"] * e, w["kk_bf"] * e) * masks_ref[w["d"], li]
    for w in work:
        q, kk, cum, d = w["q"], w["kk"], w["cum"], w["d"]
        if inblock:
            first = HG_INBLOCK - 1 if d == 1 else 0
            ref = _row_bcast(cum, [p0 + first for p0 in range(0, c, HG_INBLOCK)], HG_INBLOCK)
            z = (cum - ref) * LOG2E
            al = _dot_nt((q * jnp.exp2(z)).astype(BF16), (kk * jnp.exp2(-z)).astype(BF16))
            w["a"] = w["a"] + al * masks_ref[d, MASK_INBLOCK]
        else:
            w["a"] = w["a"] + jnp.sum(q * kk, axis=-1, keepdims=True) * masks_ref[d, MASK_DIAG]
    for w in work:
        q, kk, cum, d, head = w["q"], w["kk"], w["cum"], w["d"], w["head"]
        hi = w["hi"]
        st = st_ref[d, head]
        tot = cum[0:1, :] if d == 1 else cum[c - 1:c, :]
        o = _dot(w["a"].astype(BF16), hi)
        o = o + _dot_nt((q * jnp.exp(cum)).astype(BF16), st.astype(BF16))
        k_end = (kk * jnp.exp(tot - cum)).astype(BF16)
        st_ref[d, head] = jnp.exp(tot) * st + _dot(hi.astype(F32).T.astype(BF16), k_end)
        w["o_ref"][0, :, w["sl"]] = o.astype(BF16)


def _hgrn_body(qf_ref, ff_ref, if_ref, qb_ref, fb_ref, ib_ref, lb_ref, cum_ref, masks_ref,
               of_ref, ob_ref, st_ref, *, inblock):
    @pl.when(pl.program_id(1) == 0)
    def _():
        st_ref[...] = jnp.zeros_like(st_ref)

    for heads in HG_HEAD_GROUPS:
        chains = [(head, d) + refs for head in heads
                  for d, refs in enumerate(((qf_ref, ff_ref, if_ref, of_ref),
                                            (qb_ref, fb_ref, ib_ref, ob_ref)))]
        _hg_chunks(chains, lb_ref, cum_ref, masks_ref, st_ref, inblock)


def _hgrn(proj, lbs, layer, fx_min):
    b, l, _ = proj.shape
    c = HG_CHUNK
    nc = l // c
    masks, cum = _hg_constants()
    fwd = lambda col: pl.BlockSpec((1, c, HG_WIDTH), lambda i, j: (i, j, col))
    bwd = lambda col: pl.BlockSpec((1, c, HG_WIDTH), lambda i, j: (i, nc - 1 - j, col))
    out_sd = jax.ShapeDtypeStruct((b, l, HG_WIDTH), BF16)

    def call(inblock):
        return pl.pallas_call(
            functools.partial(_hgrn_body, inblock=inblock),
            grid=(b, nc),
            in_specs=[fwd(COL_HQ), fwd(COL_HF_FWD), fwd(COL_HI),
                      bwd(COL_HQ), bwd(COL_HF_BWD), bwd(COL_HI),
                      pl.BlockSpec((1, 1, HG_WIDTH), lambda i, j: (layer, 0, 0)),
                      pl.BlockSpec(cum.shape, lambda i, j: (0, 0, 0)),
                      pl.BlockSpec(masks.shape, lambda i, j: (0, 0, 0, 0))],
            out_specs=[pl.BlockSpec((1, c, HG_WIDTH), lambda i, j: (i, j, 0)),
                       pl.BlockSpec((1, c, HG_WIDTH), lambda i, j: (i, nc - 1 - j, 0))],
            out_shape=[out_sd, out_sd],
            scratch_shapes=[pltpu.VMEM((2, HG_HEADS, HG_DIM, HG_DIM), F32)],
            compiler_params=_cparams("arbitrary", "arbitrary"),
            name="hgrn2_recurrence",
        )(proj, proj, proj, proj, proj, proj, lbs, cum, masks)

    lb = jnp.tile(lbs[layer], (1, 2))
    worst = jnp.log(jnp.maximum(lb, LB_FLOOR) + (1.0 - lb) * _sigmoid(fx_min))
    safe = (HG_INBLOCK - 1) * jnp.max(-worst) <= HG_INBLOCK_MAX_EXPONENT
    return lax.cond(safe, lambda: call(True), lambda: call(False))


def _rope_tables(l):
    lane = np.arange(LANES)
    dd = lane % HEAD_DIM
    axis = dd // ROPE_AXIS_DIM
    first_half = (dd % ROPE_AXIS_DIM) < (ROPE_AXIS_DIM // 2)
    freq_idx = dd % (ROPE_AXIS_DIM // 2)
    inv_freq = ROPE_THETA ** (-jnp.arange(0, ROPE_AXIS_DIM, 2, dtype=F32) / ROPE_AXIS_DIM)
    t = jnp.arange(l)
    pos = jnp.where(jnp.asarray(axis)[None, :] == 0, (t // GRID_W)[:, None], (t % GRID_W)[:, None])
    ang = pos.astype(F32) * inv_freq[jnp.asarray(freq_idx)][None, :]
    sign = jnp.where(jnp.asarray(first_half), -1.0, 1.0)[None, :]
    return jnp.cos(ang), jnp.sin(ang) * sign


def _norm_rope(x, gain, cos_t, sin_t, bd, first_half):
    x2_hi, x2_lo = _split_bf16(x * x)
    ss = _dot(x2_hi, bd) + _dot(x2_lo, bd)
    xn = x * lax.rsqrt(ss * (1.0 / HEAD_DIM) + NORM_EPS) * gain
    half = ROPE_AXIS_DIM // 2
    partner = jnp.where(first_half, pltpu.roll(xn, LANES - half, 1), pltpu.roll(xn, half, 1))
    return xn * cos_t + partner * sin_t


def _qkprep_body(q_ref, k_ref, v_ref, qg_ref, kg_ref, cos_ref, sin_ref, bd_ref, qt_ref, ko_ref, vt_ref):
    tq = qt_ref.shape[4] // GQA_GROUP
    cos_t = cos_ref[...]
    sin_t = sin_ref[...]
    bd = bd_ref[...]
    lane = lax.broadcasted_iota(jnp.int32, cos_t.shape, 1)
    first_half = (lane % ROPE_AXIS_DIM) < (ROPE_AXIS_DIM // 2)
    zeros = jnp.zeros((HEAD_DIM, tq), BF16)
    for j in range(ATTN_WIDTH // LANES):
        sl = slice(j * LANES, (j + 1) * LANES)
        qr = _norm_rope(q_ref[0, :, sl].astype(F32), qg_ref[0], cos_t, sin_t, bd, first_half)
        qr = qr * (HEAD_DIM ** -0.5 * LOG2E)
        h = (2 * j) // GQA_GROUP
        for u in range(qt_ref.shape[2]):
            t = qr[u * tq:(u + 1) * tq, :].T.astype(BF16)
            for e in range(2):
                piece = t[e * HEAD_DIM:(e + 1) * HEAD_DIM, :]
                blockcol = jnp.concatenate([piece, zeros] if h == 0 else [zeros, piece], axis=0)
                g = (2 * j + e) % GQA_GROUP
                qt_ref[0, h, u, :, g * tq:(g + 1) * tq] = blockcol
    kr = _norm_rope(k_ref[0].astype(F32), kg_ref[0], cos_t, sin_t, bd, first_half)
    ko_ref[0] = kr.astype(BF16)
    vt_ref[0, 0] = v_ref[0].astype(F32).T.astype(BF16)


def _qk_prep(proj, q_gain2, k_gain2, layer):
    b, l, _ = proj.shape
    tk = min(KV_BLOCK, l)
    tq = min(Q_TILE, l)
    cos_t, sin_t = _rope_tables(l)
    blk = np.arange(LANES) // HEAD_DIM
    bd = jnp.asarray((blk[:, None] == blk[None, :]).astype(np.float32), dtype=BF16)
    return pl.pallas_call(
        _qkprep_body,
        grid=(b, l // tk),
        in_specs=[pl.BlockSpec((1, tk, ATTN_WIDTH), lambda i, j: (i, j, COL_AQ)),
                  pl.BlockSpec((1, tk, KV_WIDTH), lambda i, j: (i, j, COL_AK)),
                  pl.BlockSpec((1, tk, KV_WIDTH), lambda i, j: (i, j, COL_AV)),
                  pl.BlockSpec((1, 1, LANES), lambda i, j: (layer, 0, 0)),
                  pl.BlockSpec((1, 1, LANES), lambda i, j: (layer, 0, 0)),
                  pl.BlockSpec((tk, LANES), lambda i, j: (j, 0)),
                  pl.BlockSpec((tk, LANES), lambda i, j: (j, 0)),
                  pl.BlockSpec((LANES, LANES), lambda i, j: (0, 0))],
        out_specs=[pl.BlockSpec((1, N_KV_HEADS, tk // tq, KV_WIDTH, GQA_GROUP * tq),
                                lambda i, j: (i, 0, j, 0, 0)),
                   pl.BlockSpec((1, tk, KV_WIDTH), lambda i, j: (i, j, 0)),
                   pl.BlockSpec((1, 1, KV_WIDTH, tk), lambda i, j: (i, j, 0, 0))],
        out_shape=[jax.ShapeDtypeStruct((b, N_KV_HEADS, l // tq, KV_WIDTH, GQA_GROUP * tq), BF16),
                   jax.ShapeDtypeStruct((b, l, KV_WIDTH), BF16),
                   jax.ShapeDtypeStruct((b, l // tk, KV_WIDTH, tk), BF16)],
        compiler_params=_cparams("arbitrary", "arbitrary"),
        name="qk_norm_rope",
    )(proj, proj, proj, q_gain2, k_gain2, cos_t, sin_t, bd)


def _attn_finish(acc_scr, l_fin, o_ref):
    tq = o_ref.shape[1]
    outs = []
    for h in range(N_KV_HEADS):
        o = (acc_scr[h] / l_fin[h]).T
        oh = o[:, h * HEAD_DIM:(h + 1) * HEAD_DIM]
        outs += [oh[g * tq:(g + 1) * tq, :] for g in range(GQA_GROUP)]
    o_ref[0] = jnp.concatenate(outs, axis=1).astype(BF16)


def _attn_bounded_body(bound_ref, qt_ref, k_ref, vt_ref, o_ref, acc_scr):
    nblk = vt_ref.shape[1]
    tk = vt_ref.shape[3]
    cols = qt_ref.shape[4]
    shift = bound_ref[0, 0]
    acc_scr[...] = jnp.zeros_like(acc_scr)

    def step(i, sums):
        kb = k_ref[0, pl.ds(pl.multiple_of(i * tk, tk), tk), :]
        vt = vt_ref[0, i]
        scores = [_dot(kb, qt_ref[0, h, 0]) for h in range(N_KV_HEADS)]
        new = []
        for h in range(N_KV_HEADS):
            p = jnp.exp2(scores[h] - shift)
            new.append(sums[h] + jnp.sum(p, axis=0, keepdims=True))
            acc_scr[h] = acc_scr[h] + _dot(vt, p.astype(BF16))
        return tuple(new)

    init = tuple(jnp.zeros((1, cols), F32) for _ in range(N_KV_HEADS))
    _attn_finish(acc_scr, lax.fori_loop(0, nblk, step, init, unroll=4), o_ref)


def _attn_online_body(bound_ref, qt_ref, k_ref, vt_ref, o_ref, acc_scr):
    nblk = vt_ref.shape[1]
    tk = vt_ref.shape[3]
    cols = qt_ref.shape[4]
    acc_scr[...] = jnp.zeros_like(acc_scr)
    l_fin = []
    for h in range(N_KV_HEADS):
        qt = qt_ref[0, h, 0]

        def step(i, carry):
            m_prev, l_prev = carry
            kb = k_ref[0, pl.ds(pl.multiple_of(i * tk, tk), tk), :]
            s = _dot(kb, qt)
            m_new = jnp.maximum(m_prev, jnp.max(s, axis=0, keepdims=True))
            alpha = jnp.exp2(m_prev - m_new)
            p = jnp.exp2(s - m_new)
            l_new = alpha * l_prev + jnp.sum(p, axis=0, keepdims=True)
            acc_scr[h] = alpha * acc_scr[h] + _dot(vt_ref[0, i], p.astype(BF16))
            return m_new, l_new

        init = (jnp.full((1, cols), -jnp.inf, F32), jnp.zeros((1, cols), F32))
        l_fin.append(lax.fori_loop(0, nblk, step, init)[1])
    _attn_finish(acc_scr, l_fin, o_ref)


def _attention(qt, k_rot, vt, bound):
    b, l, _ = k_rot.shape
    tq = qt.shape[4] // GQA_GROUP
    nblk, tk = vt.shape[1], vt.shape[3]

    def call(body):
        return pl.pallas_call(
            body,
            grid=(b, l // tq),
            in_specs=[pl.BlockSpec(memory_space=pltpu.SMEM),
                      pl.BlockSpec((1, N_KV_HEADS, 1, KV_WIDTH, GQA_GROUP * tq), lambda i, j: (i, 0, j, 0, 0)),
                      pl.BlockSpec((1, l, KV_WIDTH), lambda i, j: (i, 0, 0)),
                      pl.BlockSpec((1, nblk, KV_WIDTH, tk), lambda i, j: (i, 0, 0, 0))],
            out_specs=pl.BlockSpec((1, tq, ATTN_WIDTH), lambda i, j: (i, j, 0)),
            out_shape=jax.ShapeDtypeStruct((b, l, ATTN_WIDTH), BF16),
            scratch_shapes=[pltpu.VMEM((N_KV_HEADS, KV_WIDTH, GQA_GROUP * tq), F32)],
            compiler_params=_cparams("arbitrary", "arbitrary"),
            name="gqa_attention",
        )(bound, qt, k_rot, vt)

    return lax.cond(bound[0, 0] <= SAFE_SCORE_BOUND,
                    lambda: call(_attn_bounded_body), lambda: call(_attn_online_body))


def _route(logits):
    lane = lax.broadcasted_iota(jnp.int32, logits.shape, 1).astype(F32)
    neg = jnp.float32(-jnp.inf)
    lanemin = lambda cond: jnp.min(jnp.where(cond, lane, float(ROUTE_LANES)), axis=-1, keepdims=True)
    gl = jnp.where(lane < N_GROUPS, logits, neg)
    gmax = jnp.max(gl, axis=-1, keepdims=True)
    g_idx = lanemin(gl == gmax)
    pg_top = 1.0 / jnp.sum(jnp.exp(gl - gmax), axis=-1, keepdims=True)
    first = N_GROUPS + g_idx * EXPERTS_PER_GROUP
    in_group = (lane >= first) & (lane < first + EXPERTS_PER_GROUP)
    el = jnp.where(in_group, logits, neg)
    a_max = jnp.max(el, axis=-1, keepdims=True)
    a_idx = lanemin(el == a_max)
    el2 = jnp.where(lane == a_idx, neg, el)
    b_max = jnp.max(el2, axis=-1, keepdims=True)
    b_idx = lanemin(el2 == b_max)
    r = jnp.exp(b_max - a_max)
    w_a = pg_top / (1.0 + r)
    w_b = pg_top * r / (1.0 + r)
    return a_idx - N_GROUPS, b_idx - N_GROUPS, w_a, w_b


def _outproj_body(of_ref, ob_ref, hg_ref, at_ref, x_ref, mod_ref, hgain_ref, wout_ref, fgain_ref,
                  wr_ref, br_ref, tri_ref, xo_ref, h_ref, rt_ref, cnt_ref, cnt_scr, *, seg):
    tm = x_ref.shape[1]
    tile = pl.program_id(0) * pl.num_programs(1) + pl.program_id(1)

    @pl.when(tile == 0)
    def _():
        cnt_scr[...] = jnp.zeros_like(cnt_scr)

    nblk = max(1, tm // OUTPROJ_ROWS)
    blocks = [slice(k * (tm // nblk), (k + 1) * (tm // nblk)) for k in range(nblk)]
    o_hg = []
    for rows in blocks:
        parts = []
        for head in range(HG_HEADS):
            sl = slice(head * HG_DIM, (head + 1) * HG_DIM)
            o = of_ref[0, rows, sl].astype(F32) + ob_ref[0, rows, sl].astype(F32)
            o = o * lax.rsqrt(jnp.mean(o * o, axis=-1, keepdims=True) + NORM_EPS) * hgain_ref[0]
            gt = hg_ref[0, rows, sl].astype(F32)
            parts.append((o * (gt * _sigmoid(gt))).astype(BF16))
        o_hg.append(jnp.concatenate(parts, axis=1))
    mix = [_dot(o_hg[k], wout_ref[0, :HG_WIDTH, :]) + _dot(at_ref[0, rows, :], wout_ref[0, HG_WIDTH:, :])
           for k, rows in enumerate(blocks)]
    hs = []
    for k, rows in enumerate(blocks):
        x = x_ref[0, rows, :] + mod_ref[0, 2:3, :] * mix[k]
        xo_ref[0, rows, :] = x
        y = x * lax.rsqrt(jnp.mean(x * x, axis=-1, keepdims=True) + NORM_EPS) * fgain_ref[0]
        h = y * (1.0 + mod_ref[0, 4:5, :]) + mod_ref[0, 3:4, :]
        h_ref[0, rows, :] = h
        hs.append(_split_bf16(h))
    routed = []
    for h_hi, h_lo in hs:
        both = _dot(h_hi, wr_ref[0])
        logits = (both[:, :ROUTE_LANES] + both[:, ROUTE_LANES:]
                  + _dot(h_lo, wr_ref[0, :, :ROUTE_LANES]) + br_ref[0])
        routed.append(_route(logits))
    e_a, e_b, w_a, w_b = (jnp.concatenate([r[c] for r in routed], axis=0) for c in range(4))

    lane = lax.broadcasted_iota(jnp.int32, (tm, ROUTE_LANES), 1).astype(F32)
    group0 = (((tile * tm) // seg) * N_EXPERTS).astype(F32)
    hot_a = jnp.where(lane == e_a + group0, 1.0, 0.0)
    hot_b = jnp.where(lane == e_b + group0, 1.0, 0.0)
    hot = hot_a + hot_b
    before = cnt_scr[...] + _dot(tri_ref[...], hot.astype(BF16))
    rank_a = jnp.sum(hot_a * before, axis=-1, keepdims=True)
    rank_b = jnp.sum(hot_b * before, axis=-1, keepdims=True)
    cnt_scr[...] = cnt_scr[...] + jnp.sum(hot, axis=0, keepdims=True)
    cnt_ref[...] = cnt_scr[...]
    cols = (e_a, e_b, w_a, w_b, rank_a, rank_b)
    out = jnp.zeros_like(lane)
    for c, v in enumerate(cols):
        out = jnp.where(lane == float(c), v, out)
    rt_ref[0] = out


def _out_proj(o_f, o_b, proj, o_at, x, modr, mod_row0, hgain, w_out_bf, fgain, w_route, b_route, layer,
              seg):
    b, l, _ = x.shape
    tm = min(ROW_TILE, l)
    tok = lambda w, col=0: pl.BlockSpec((1, tm, w), lambda i, j: (i, j, col))
    lay = lambda *s: pl.BlockSpec((1,) + s, lambda i, j: (layer,) + (0,) * len(s))
    tri = jnp.asarray(np.tril(np.ones((tm, tm), np.float32), -1), dtype=BF16)
    return pl.pallas_call(
        functools.partial(_outproj_body, seg=seg),
        grid=(b, l // tm),
        in_specs=[tok(HG_WIDTH), tok(HG_WIDTH), tok(HG_WIDTH, COL_HG), tok(ATTN_WIDTH), tok(D_MODEL),
                  pl.BlockSpec((1, N_MOD, D_MODEL), lambda i, j: (mod_row0 + i, 0, 0)),
                  lay(1, HG_DIM), lay(D_MODEL, D_MODEL), lay(1, D_MODEL),
                  lay(D_MODEL, 2 * ROUTE_LANES), lay(1, ROUTE_LANES),
                  pl.BlockSpec((tm, tm), lambda i, j: (0, 0))],
        out_specs=[tok(D_MODEL), tok(D_MODEL), tok(ROUTE_LANES),
                   pl.BlockSpec((1, ROUTE_LANES), lambda i, j: (0, 0))],
        out_shape=[jax.ShapeDtypeStruct((b, l, D_MODEL), F32),
                   jax.ShapeDtypeStruct((b, l, D_MODEL), F32),
                   jax.ShapeDtypeStruct((b, l, ROUTE_LANES), F32),
                   jax.ShapeDtypeStruct((1, ROUTE_LANES), F32)],
        scratch_shapes=[pltpu.VMEM((1, ROUTE_LANES), F32)],
        compiler_params=_cparams("arbitrary", "arbitrary"),
        name="out_proj_router",
    )(o_f, o_b, proj, o_at, x, modr, hgain, w_out_bf, fgain, w_route, b_route, tri)


def _moe_body(te_ref, tv_ref, tu_ref, tok_ref, h_ref, *refs, steps_per_seg):
    nu = FFN_TILES_PER_STEP
    weights = [refs[3 * u:3 * u + 3] for u in range(nu)]
    o_ref, xs_scr = refs[3 * nu:]
    step = pl.program_id(0)
    tr = xs_scr.shape[2]
    cur = step % 2

    def ffn(u, xs):
        wg_ref, wu_ref, wd_ref = weights[u]
        gt = _dot(xs, wg_ref[0])
        up = _dot(xs, wu_ref[0])
        hid = (gt * _sigmoid(gt)) * up
        o_ref[u * tr:(u + 1) * tr, :] = _dot(hid.astype(BF16), wd_ref[0])

    first_valid = tv_ref[nu * step] != 0
    last_valid = tv_ref[nu * step + nu - 1] != 0

    @pl.when(jnp.logical_and(first_valid, step % steps_per_seg == 0))
    def _():
        for u in range(nu):
            first = tu_ref[nu * step + u]

            def row(r, carry):
                xs_scr[cur, u, pl.ds(r, 1), :] = h_ref[0, pl.ds(tok_ref[first + r], 1), :]
                return carry

            lax.fori_loop(0, tr, row, 0, unroll=8)

    @pl.when(last_valid)
    def _():
        nxt = jnp.minimum(step + 1, pl.num_programs(0) - 1)
        xs = [xs_scr[cur, u].astype(BF16) for u in range(nu)]
        for u in range(nu):
            first = tu_ref[nu * nxt + u]
            for r in range(tr):
                xs_scr[1 - cur, u, r:r + 1, :] = h_ref[0, pl.ds(tok_ref[first + r], 1), :]
        for u in range(nu):
            ffn(u, xs[u])

    @pl.when(jnp.logical_and(first_valid, jnp.logical_not(last_valid)))
    def _():
        for u in range(nu):
            ffn(u, xs_scr[cur, u].astype(BF16))

    @pl.when(jnp.logical_not(first_valid))
    def _():
        o_ref[...] = jnp.zeros_like(o_ref)


def _moe_ffn(h_seg, sorted_tok, tile_expert, tile_valid, tile_first, wg, wu, wd, layer):
    nseg, seg, d = h_seg.shape
    n_tiles = tile_expert.shape[0]
    tr = EXPERT_TILE
    nu = FFN_TILES_PER_STEP
    steps_per_seg = n_tiles // nseg // nu
    wspec = lambda k, n, u: pl.BlockSpec(
        (1, k, n), lambda i, te, *_: (layer * N_EXPERTS + te[nu * i + u], 0, 0))
    wspecs = [wspec(D_MODEL, D_EXPERT, u) if j < 2 else wspec(D_EXPERT, D_MODEL, u)
              for u in range(nu) for j in range(3)]
    return pl.pallas_call(
        functools.partial(_moe_body, steps_per_seg=steps_per_seg),
        grid_spec=pltpu.PrefetchScalarGridSpec(
            num_scalar_prefetch=4,
            grid=(n_tiles // nu,),
            in_specs=[pl.BlockSpec((1, seg, d), lambda i, *_: (i // steps_per_seg, 0, 0),
                                   pipeline_mode=pl.Buffered(1))] + wspecs,
            out_specs=pl.BlockSpec((nu * tr, d), lambda i, *_: (i, 0)),
            scratch_shapes=[pltpu.VMEM((2, nu, tr, d), F32)]),
        out_shape=jax.ShapeDtypeStruct((n_tiles * tr, d), F32),
        compiler_params=_cparams("arbitrary"),
        name="routed_expert_ffn",
    )(tile_expert, tile_valid, tile_first, sorted_tok, h_seg, *([wg, wu, wd] * nu))


def _combine_body(pa_ref, pb_ref, rows_ref, rt_ref, x_ref, mod_ref, fin_ref, o_ref, ab_scr, *,
                  final, seg_tiles):
    tm = o_ref.shape[1]
    tile = pl.program_id(0) * pl.num_programs(1) + pl.program_id(1)
    cur = tile % 2
    pos = (pa_ref, pb_ref)

    @pl.when(tile % seg_tiles == 0)
    def _():
        def gather(r, carry):
            for slot in range(2):
                ab_scr[cur, slot, pl.ds(r, 1), :] = rows_ref[0, pl.ds(pos[slot][tile * tm + r], 1), :]
            return carry

        lax.fori_loop(0, tm, gather, 0, unroll=8)

    rt = rt_ref[0]
    moe = rt[:, 2:3] * ab_scr[cur, 0] + rt[:, 3:4] * ab_scr[cur, 1]
    nxt = jnp.minimum(tile + 1, pl.num_programs(0) * pl.num_programs(1) - 1) * tm
    for r in range(tm):
        for slot in range(2):
            ab_scr[1 - cur, slot, r:r + 1, :] = rows_ref[0, pl.ds(pos[slot][nxt + r], 1), :]
    x = x_ref[0] + mod_ref[0, 5:6, :] * moe
    if final:
        x = x * lax.rsqrt(jnp.mean(x * x, axis=-1, keepdims=True) + NORM_EPS) * fin_ref[...]
    o_ref[0] = x


def _combine(pos_a, pos_b, rows_seg, route, x, modr, mod_row0, final_gain, final):
    b, l, _ = x.shape
    nseg, rseg, d = rows_seg.shape
    tm = min(COMBINE_TILE, l)
    tiles_l = l // tm
    seg_tiles = (b * tiles_l) // nseg
    tok = lambda w: pl.BlockSpec((1, tm, w), lambda i, j, *_: (i, j, 0))
    return pl.pallas_call(
        functools.partial(_combine_body, final=final, seg_tiles=seg_tiles),
        grid_spec=pltpu.PrefetchScalarGridSpec(
            num_scalar_prefetch=2,
            grid=(b, tiles_l),
            in_specs=[pl.BlockSpec((1, rseg, d), lambda i, j, *_: ((i * tiles_l + j) // seg_tiles, 0, 0),
                                   pipeline_mode=pl.Buffered(1)),
                      tok(ROUTE_LANES), tok(D_MODEL),
                      pl.BlockSpec((1, N_MOD, D_MODEL), lambda i, j, *_: (mod_row0 + i, 0, 0)),
                      pl.BlockSpec((1, D_MODEL), lambda i, j, *_: (0, 0))],
            out_specs=tok(D_MODEL),
            scratch_shapes=[pltpu.VMEM((2, 2, tm, d), F32)]),
        out_shape=jax.ShapeDtypeStruct((b, l, D_MODEL), F32),
        compiler_params=_cparams("arbitrary", "arbitrary"),
        name="moe_combine",
    )(pos_a, pos_b, rows_seg, route, x, modr, final_gain)


def _routing_tables(route, counts, nseg, seg):
    t = route.shape[0]
    tr = EXPERT_TILE
    rseg = 2 * seg + N_EXPERTS * tr
    ngrp = nseg * N_EXPERTS
    cnt = counts[0, :ngrp].astype(jnp.int32)
    pad2 = (((cnt + tr - 1) // tr) * tr).reshape(nseg, N_EXPERTS)
    end_local = jnp.cumsum(pad2, axis=1)
    start_local = (end_local - pad2).reshape(-1)
    start_unp = jnp.cumsum(cnt) - cnt
    ids = route[:, 0:2].astype(jnp.int32)
    rank = route[:, 4:6].astype(jnp.int32)
    key = (jnp.arange(t, dtype=jnp.int32) // seg)[:, None] * N_EXPERTS + ids
    hot = key[:, :, None] == jnp.arange(ngrp, dtype=jnp.int32)[None, None, :]
    lookup = lambda table: jnp.sum(jnp.where(hot, table[None, None, :], 0), axis=-1)
    pos = lookup(start_local) + rank
    order = lookup(start_unp) + rank
    tok = jnp.broadcast_to((jnp.arange(t, dtype=jnp.int32) % seg)[:, None], (t, 2))
    _, sorted_tok = lax.sort((order.reshape(-1), tok.reshape(-1)), num_keys=1)
    sorted_tok = jnp.concatenate([sorted_tok, jnp.zeros((tr,), jnp.int32)])
    tiles_per_seg = rseg // tr
    local_row = jnp.arange(tiles_per_seg, dtype=jnp.int32) * tr
    tile_grp = jnp.sum(local_row[None, :, None] >= end_local[:, None, :], axis=-1)
    tile_valid = (tile_grp < N_EXPERTS).astype(jnp.int32)
    tile_expert = jnp.minimum(tile_grp, N_EXPERTS - 1).astype(jnp.int32)
    grp = tile_expert + (jnp.arange(nseg, dtype=jnp.int32) * N_EXPERTS)[:, None]
    hot_t = grp[:, :, None] == jnp.arange(ngrp, dtype=jnp.int32)[None, None, :]
    skipped_pad = jnp.sum(jnp.where(hot_t, (start_local - start_unp)[None, None, :], 0), axis=-1)
    tile_first = (local_row[None, :] - skipped_pad) * tile_valid
    return (sorted_tok, pos[:, 0], pos[:, 1], tile_expert.reshape(-1), tile_valid.reshape(-1),
            tile_first.reshape(-1), rseg)


def _trunk(x, modr, mod_row0, nb, params):
    b, l, _ = x.shape
    t = b * l
    seg = min(MOE_SEGMENT, t)
    nseg = t // seg
    depth = params["w_in"].shape[0]
    for layer in range(depth):
        row0 = layer * nb + mod_row0
        proj, fx_min = _in_proj(x, modr, row0, params["norm_mix"], params["w_in"], layer)
        o_f, o_b = _hgrn(proj, params["lbs"], layer, fx_min)
        qt, k_rot, vt = _qk_prep(proj, params["q_gain2"], params["k_gain2"], layer)
        o_at = _attention(qt, k_rot, vt, params["score_bound"][layer])
        x_mid, h, route, counts = _out_proj(o_f, o_b, proj, o_at, x, modr, row0, params["hg_gain"],
                                            params["w_out"], params["norm_ffn"], params["w_route"],
                                            params["b_route"], layer, seg)
        sorted_tok, pos_a, pos_b, tile_e, tile_v, tile_f, rseg = _routing_tables(
            route.reshape(t, ROUTE_LANES), counts, nseg, seg)
        rows = _moe_ffn(h.reshape(nseg, seg, D_MODEL), sorted_tok, tile_e, tile_v, tile_f,
                        params["w_e_gate"], params["w_e_up"], params["w_e_down"], layer)
        x = _combine(pos_a, pos_b, rows.reshape(nseg, rseg, D_MODEL), route, x_mid, modr, row0,
                     params["final_norm"], final=(layer == depth - 1))
    return x


def kernel(x_prompt, x_sample, c_prompt, c_sample, w_in, w_out, hg_lb, hg_out_norm, q_norm, k_norm,
           norm_mix, norm_ffn, w_ada, b_ada, w_group, b_group, w_router, b_router, w_e_gate, w_e_up,
           w_e_down, final_norm):
    depth = w_in.shape[0]
    bp, bs = c_prompt.shape[0], c_sample.shape[0]
    nb = -(-(bp + bs) // SUBLANES) * SUBLANES
    c_all = jnp.zeros((nb, D_MODEL), F32).at[:bp].set(c_prompt).at[bp:bp + bs].set(c_sample)
    mod = _modulation(c_all, w_ada, b_ada)
    modr = mod.reshape(depth * nb, N_MOD, D_MODEL)
    pad = ROUTE_LANES - N_GROUPS - N_EXPERTS
    params = {
        "w_in": w_in.astype(BF16),
        "w_out": w_out.astype(BF16),
        "lbs": _lower_bounds(hg_lb).reshape(depth, 1, HG_WIDTH),
        "hg_gain": hg_out_norm.reshape(depth, 1, HG_DIM),
        "q_gain2": jnp.tile(q_norm, (1, LANES // HEAD_DIM)).reshape(depth, 1, LANES),
        "k_gain2": jnp.tile(k_norm, (1, LANES // HEAD_DIM)).reshape(depth, 1, LANES),
        "score_bound": (1.02 * LOG2E * HEAD_DIM ** 0.5 * jnp.max(jnp.abs(q_norm), axis=1)
                        * jnp.max(jnp.abs(k_norm), axis=1)).reshape(depth, 1, 1),
        "norm_mix": norm_mix.reshape(depth, 1, D_MODEL),
        "norm_ffn": norm_ffn.reshape(depth, 1, D_MODEL),
        "w_route": jnp.concatenate(_split_bf16(jnp.pad(jnp.concatenate([w_group, w_router], axis=-1),
                                                       ((0, 0), (0, 0), (0, pad)))), axis=-1),
        "b_route": jnp.pad(jnp.concatenate([b_group, b_router], axis=-1),
                           ((0, 0), (0, pad))).reshape(depth, 1, ROUTE_LANES),
        "w_e_gate": w_e_gate.astype(BF16).reshape(depth * N_EXPERTS, D_MODEL, D_EXPERT),
        "w_e_up": w_e_up.astype(BF16).reshape(depth * N_EXPERTS, D_MODEL, D_EXPERT),
        "w_e_down": w_e_down.astype(BF16).reshape(depth * N_EXPERTS, D_EXPERT, D_MODEL),
        "final_norm": final_norm.reshape(1, D_MODEL),
    }
    y_prompt = _trunk(x_prompt, modr, 0, nb, params)
    y_sample = _trunk(x_sample, modr, bp, nb, params)
    return (y_prompt, y_sample)
```

```python
import functools

import numpy as np
import jax
import jax.numpy as jnp
from jax import lax
from jax.experimental import pallas as pl
from jax.experimental.pallas import tpu as pltpu

F32 = jnp.float32
BF16 = jnp.bfloat16

D_MODEL = 1024
HG_WIDTH = 512
HG_HEADS = 4
HG_DIM = 128
LB_FLOOR = 1e-30
ATTN_WIDTH = 512
HEAD_DIM = 64
N_Q_HEADS = 8
N_KV_HEADS = 2
GQA_GROUP = 4
KV_WIDTH = 128
GRID_W = 64
ROPE_AXIS_DIM = 32
ROPE_THETA = 10000.0
N_GROUPS = 4
EXPERTS_PER_GROUP = 4
N_EXPERTS = 16
D_EXPERT = 512
N_MOD = 6
NORM_EPS = 1e-6
IN_PROJ_WIDTH = 5 * HG_WIDTH + ATTN_WIDTH + 2 * KV_WIDTH
COL_HQ, COL_HF_FWD, COL_HF_BWD, COL_HI, COL_HG, COL_AQ = 0, 1, 2, 3, 4, 5
COL_AK, COL_AV = 24, 25

LANES = 128
SUBLANES = 8
VMEM_LIMIT = 56 * 1024 * 1024
ROW_TILE = 512
OUTPROJ_ROWS = 512
HG_CHUNK = 128
KV_BLOCK = 512
Q_TILE = 256
EXPERT_TILE = 128
FFN_TILES_PER_STEP = 4
COMBINE_TILE = 256
MOE_SEGMENT = 4096
ROUTE_LANES = 128
LOG2E = 1.4426950408889634
SAFE_SCORE_BOUND = 40.0


def _cparams(*sem):
    return pltpu.CompilerParams(dimension_semantics=sem, vmem_limit_bytes=VMEM_LIMIT)


def _dot(a, b):
    return jnp.dot(a, b, preferred_element_type=F32)


def _dot_nt(a, b):
    return lax.dot_general(a, b, (((1,), (1,)), ((), ())), preferred_element_type=F32)


def _split_bf16(x):
    hi = x.astype(BF16)
    lo = (x - hi.astype(F32)).astype(BF16)
    return hi, lo


def _sigmoid(x):
    return 1.0 / (1.0 + jnp.exp(-x))


def _mod_body(c_ref, w_ref, b_ref, o_ref):
    c = c_ref[...]
    cs = c * _sigmoid(c)
    o_ref[0] = _dot(cs.astype(BF16), w_ref[0].astype(BF16)) + b_ref[0]


def _modulation(c_all, w_ada, b_ada):
    nb = c_all.shape[0]
    depth, _, n = w_ada.shape
    tn = 1024
    return pl.pallas_call(
        _mod_body,
        grid=(depth, n // tn),
        in_specs=[pl.BlockSpec((nb, D_MODEL), lambda l, j: (0, 0)),
                  pl.BlockSpec((1, D_MODEL, tn), lambda l, j: (l, 0, j)),
                  pl.BlockSpec((1, 1, tn), lambda l, j: (l, 0, j))],
        out_specs=pl.BlockSpec((1, nb, tn), lambda l, j: (l, 0, j)),
        out_shape=jax.ShapeDtypeStruct((depth, nb, n), F32),
        compiler_params=_cparams("arbitrary", "arbitrary"),
        name="adaln_modulation",
    )(c_all, w_ada, b_ada.reshape(depth, 1, n))


def _lb_body(p_ref, o_ref):
    depth = p_ref.shape[0]
    rows = [p_ref[l:l + 1, :] for l in range(depth)]
    m = functools.reduce(jnp.maximum, rows)
    es = [jnp.exp(r - m) for r in rows]
    tot = functools.reduce(lambda a, b: a + b, es)
    sm = [e / tot for e in es]
    acc = jnp.zeros_like(sm[0])
    for l in range(depth):
        acc = acc + sm[l]
        o_ref[l:l + 1, :] = acc - sm[0]


def _lower_bounds(hg_lb):
    return pl.pallas_call(
        _lb_body,
        out_shape=jax.ShapeDtypeStruct(hg_lb.shape, F32),
        name="hgrn_lower_bounds",
    )(hg_lb)


def _inproj_body(x_ref, mod_ref, gain_ref, w_ref, p_ref, fmin_ref):
    x = x_ref[0]
    ms = jnp.mean(x * x, axis=-1, keepdims=True)
    y = x * lax.rsqrt(ms + NORM_EPS) * gain_ref[0]
    h = y * (1.0 + mod_ref[0, 1:2, :]) + mod_ref[0, 0:1, :]
    proj = _dot(h.astype(BF16), w_ref[0]).astype(BF16)
    p_ref[0] = proj
    lo, hi = COL_HF_FWD * HG_WIDTH, (COL_HF_BWD + 1) * HG_WIDTH
    cur = jnp.min(proj[:, lo:hi].astype(F32), axis=0, keepdims=True)
    first = jnp.logical_and(pl.program_id(0) == 0, pl.program_id(1) == 0)

    @pl.when(first)
    def _():
        fmin_ref[...] = cur

    @pl.when(jnp.logical_not(first))
    def _():
        fmin_ref[...] = jnp.minimum(fmin_ref[...], cur)


def _in_proj(x, modr, mod_row0, gain, w_in_bf, layer):
    b, l, _ = x.shape
    tm = min(ROW_TILE, l)
    return pl.pallas_call(
        _inproj_body,
        grid=(b, l // tm),
        in_specs=[pl.BlockSpec((1, tm, D_MODEL), lambda i, j: (i, j, 0)),
                  pl.BlockSpec((1, N_MOD, D_MODEL), lambda i, j: (mod_row0 + i, 0, 0)),
                  pl.BlockSpec((1, 1, D_MODEL), lambda i, j: (layer, 0, 0)),
                  pl.BlockSpec((1, D_MODEL, IN_PROJ_WIDTH), lambda i, j: (layer, 0, 0))],
        out_specs=[pl.BlockSpec((1, tm, IN_PROJ_WIDTH), lambda i, j: (i, j, 0)),
                   pl.BlockSpec((1, 2 * HG_WIDTH), lambda i, j: (0, 0))],
        out_shape=[jax.ShapeDtypeStruct((b, l, IN_PROJ_WIDTH), BF16),
                   jax.ShapeDtypeStruct((1, 2 * HG_WIDTH), F32)],
        compiler_params=_cparams("arbitrary", "arbitrary"),
        name="in_proj",
    )(x, modr, gain, w_in_bf)


HG_LEVELS = (64, 32, 16, 8, 4, 2, 1)
HG_INBLOCK = 8
HG_INBLOCK_MAX_EXPONENT = 80.0
HG_HEAD_GROUPS = ((0, 1, 2, 3),)
MASK_DIAG = len(HG_LEVELS)
MASK_INBLOCK = len(HG_LEVELS) + 1


def _hg_constants():
    c = HG_CHUNK
    t = np.arange(c)[:, None]
    s = np.arange(c)[None, :]
    masks = np.zeros((2, len(HG_LEVELS) + 2, c, c), np.float32)
    for li, h in enumerate(HG_LEVELS):
        same = (t // (2 * h)) == (s // (2 * h))
        masks[0, li] = same & ((t % (2 * h)) >= h) & ((s % (2 * h)) < h)
        masks[1, li] = same & ((t % (2 * h)) < h) & ((s % (2 * h)) >= h)
    masks[:, MASK_DIAG] = (t == s)
    same = (t // HG_INBLOCK) == (s // HG_INBLOCK)
    masks[0, MASK_INBLOCK] = same & (s <= t)
    masks[1, MASK_INBLOCK] = same & (s >= t)
    cum = np.stack([(s <= t), (s >= t)]).astype(np.float32)
    return jnp.asarray(masks), jnp.asarray(cum, dtype=BF16)


def _row_bcast(a, rows, blk):
    parts = [jnp.broadcast_to(a[r:r + 1, :], (blk, a.shape[1])) for r in rows]
    return parts[0] if len(parts) == 1 else jnp.concatenate(parts, axis=0)


def _level_reference(cum, h, reverse, sub):
    c = cum.shape[0]
    if 2 * h >= SUBLANES:
        rows = [p0 + (h if reverse else h - 1) for p0 in range(0, c, 2 * h)]
        return _row_bcast(cum, rows, 2 * h)
    out = None
    for p0 in reversed(range(0, SUBLANES, 2 * h)):
        r = p0 + (h if reverse else h - 1)
        piece = _row_bcast(cum, [v0 + r for v0 in range(0, c, SUBLANES)], SUBLANES)
        out = piece if out is None else jnp.where(sub < p0 + 2 * h, piece, out)
    return out


def _hg_chunks(chains, lb_ref, cum_ref, masks_ref, st_ref, inblock):
    c = HG_CHUNK
    sub = lax.broadcasted_iota(jnp.int32, (c, HG_DIM), 0) % SUBLANES
    work = []
    for head, d, q_ref, f_ref, i_ref, o_ref in chains:
        sl = slice(head * HG_DIM, (head + 1) * HG_DIM)
        lb = lb_ref[0, :, sl]
        hq = q_ref[0, :, sl].astype(F32)
        sig = _sigmoid(f_ref[0, :, sl].astype(F32))
        one_m_lb = 1.0 - lb
        w = dict(head=head, d=d, sl=sl, o_ref=o_ref, hi=i_ref[0, :, sl], q=hq * _sigmoid(hq),
                 g=jnp.log(jnp.maximum(lb, LB_FLOOR) + one_m_lb * sig),
                 kk=one_m_lb * (1.0 - sig))
        work.append(w)
    for w in work:
        g_hi, g_lo = _split_bf16(w["g"])
        w["cum"] = _dot(cum_ref[w["d"]], g_hi) + _dot(cum_ref[w["d"]], g_lo)
        w["q_bf"] = w["q"].astype(BF16)
        w["kk_bf"] = w["kk"].astype(BF16)
        w["a"] = jnp.zeros((c, c), F32)

    for li, h in enumerate(HG_LEVELS):
        if inblock and 2 * h <= HG_INBLOCK:
            continue
        for w in work:
            ref = _level_reference(w["cum"], h, w["d"] == 1, sub)
            e = jnp.exp2((jnp.abs(w["cum"] - ref) * (-LOG2E)).astype(BF16))
            w["a"] = w["a"] + _dot_nt(w["q_bf"] * e, w["kk_bf"] * e) * masks_ref[w["d"], li]
    for w in work:
        q, kk, cum, d = w["q"], w["kk"], w["cum"], w["d"]
        if inblock:
            first = HG_INBLOCK - 1 if d == 1 else 0
            ref = _row_bcast(cum, [p0 + first for p0 in range(0, c, HG_INBLOCK)], HG_INBLOCK)
            z = (cum - ref) * LOG2E
            al = _dot_nt((q * jnp.exp2(z)).astype(BF16), (kk * jnp.exp2(-z)).astype(BF16))
            w["a"] = w["a"] + al * masks_ref[d, MASK_INBLOCK]
        else:
            w["a"] = w["a"] + jnp.sum(q * kk, axis=-1, keepdims=True) * masks_ref[d, MASK_DIAG]
    for w in work:
        q, kk, cum, d, head = w["q"], w["kk"], w["cum"], w["d"], w["head"]
        hi = w["hi"]
        st = st_ref[d, head]
        tot = cum[0:1, :] if d == 1 else cum[c - 1:c, :]
        o = _dot(w["a"].astype(BF16), hi)
        o = o + _dot_nt((q * jnp.exp(cum)).astype(BF16), st.astype(BF16))
        k_end = (kk * jnp.exp(tot - cum)).astype(BF16)
        st_ref[d, head] = jnp.exp(tot) * st + _dot(hi.astype(F32).T.astype(BF16), k_end)
        w["o_ref"][0, :, w["sl"]] = o.astype(BF16)


def _hgrn_body(qf_ref, ff_ref, if_ref, qb_ref, fb_ref, ib_ref, lb_ref, cum_ref, masks_ref,
               of_ref, ob_ref, st_ref, *, inblock):
    @pl.when(pl.program_id(1) == 0)
    def _():
        st_ref[...] = jnp.zeros_like(st_ref)

    for heads in HG_HEAD_GROUPS:
        chains = [(head, d) + refs for head in heads
                  for d, refs in enumerate(((qf_ref, ff_ref, if_ref, of_ref),
                                            (qb_ref, fb_ref, ib_ref, ob_ref)))]
        _hg_chunks(chains, lb_ref, cum_ref, masks_ref, st_ref, inblock)


def _hgrn(proj, lbs, layer, fx_min):
    b, l, _ = proj.shape
    c = HG_CHUNK
    nc = l // c
    masks, cum = _hg_constants()
    fwd = lambda col: pl.BlockSpec((1, c, HG_WIDTH), lambda i, j: (i, j, col))
    bwd = lambda col: pl.BlockSpec((1, c, HG_WIDTH), lambda i, j: (i, nc - 1 - j, col))
    out_sd = jax.ShapeDtypeStruct((b, l, HG_WIDTH), BF16)

    def call(inblock):
        return pl.pallas_call(
            functools.partial(_hgrn_body, inblock=inblock),
            grid=(b, nc),
            in_specs=[fwd(COL_HQ), fwd(COL_HF_FWD), fwd(COL_HI),
                      bwd(COL_HQ), bwd(COL_HF_BWD), bwd(COL_HI),
                      pl.BlockSpec((1, 1, HG_WIDTH), lambda i, j: (layer, 0, 0)),
                      pl.BlockSpec(cum.shape, lambda i, j: (0, 0, 0)),
                      pl.BlockSpec(masks.shape, lambda i, j: (0, 0, 0, 0))],
            out_specs=[pl.BlockSpec((1, c, HG_WIDTH), lambda i, j: (i, j, 0)),
                       pl.BlockSpec((1, c, HG_WIDTH), lambda i, j: (i, nc - 1 - j, 0))],
            out_shape=[out_sd, out_sd],
            scratch_shapes=[pltpu.VMEM((2, HG_HEADS, HG_DIM, HG_DIM), F32)],
            compiler_params=_cparams("arbitrary", "arbitrary"),
            name="hgrn2_recurrence",
        )(proj, proj, proj, proj, proj, proj, lbs, cum, masks)

    lb = jnp.tile(lbs[layer], (1, 2))
    worst = jnp.log(jnp.maximum(lb, LB_FLOOR) + (1.0 - lb) * _sigmoid(fx_min))
    safe = (HG_INBLOCK - 1) * jnp.max(-worst) <= HG_INBLOCK_MAX_EXPONENT
    return lax.cond(safe, lambda: call(True), lambda: call(False))


def _rope_tables(l):
    lane = np.arange(LANES)
    dd = lane % HEAD_DIM
    axis = dd // ROPE_AXIS_DIM
    first_half = (dd % ROPE_AXIS_DIM) < (ROPE_AXIS_DIM // 2)
    freq_idx = dd % (ROPE_AXIS_DIM // 2)
    inv_freq = ROPE_THETA ** (-jnp.arange(0, ROPE_AXIS_DIM, 2, dtype=F32) / ROPE_AXIS_DIM)
    t = jnp.arange(l)
    pos = jnp.where(jnp.asarray(axis)[None, :] == 0, (t // GRID_W)[:, None], (t % GRID_W)[:, None])
    ang = pos.astype(F32) * inv_freq[jnp.asarray(freq_idx)][None, :]
    sign = jnp.where(jnp.asarray(first_half), -1.0, 1.0)[None, :]
    return jnp.cos(ang), jnp.sin(ang) * sign


def _norm_rope(x, gain, cos_t, sin_t, bd, first_half):
    x2_hi, x2_lo = _split_bf16(x * x)
    ss = _dot(x2_hi, bd) + _dot(x2_lo, bd)
    xn = x * lax.rsqrt(ss * (1.0 / HEAD_DIM) + NORM_EPS) * gain
    half = ROPE_AXIS_DIM // 2
    partner = jnp.where(first_half, pltpu.roll(xn, LANES - half, 1), pltpu.roll(xn, half, 1))
    return xn * cos_t + partner * sin_t


def _qkprep_body(q_ref, k_ref, v_ref, qg_ref, kg_ref, cos_ref, sin_ref, bd_ref, qt_ref, ko_ref, vt_ref):
    tq = qt_ref.shape[4] // GQA_GROUP
    cos_t = cos_ref[...]
    sin_t = sin_ref[...]
    bd = bd_ref[...]
    lane = lax.broadcasted_iota(jnp.int32, cos_t.shape, 1)
    first_half = (lane % ROPE_AXIS_DIM) < (ROPE_AXIS_DIM // 2)
    zeros = jnp.zeros((HEAD_DIM, tq), BF16)
    for j in range(ATTN_WIDTH // LANES):
        sl = slice(j * LANES, (j + 1) * LANES)
        qr = _norm_rope(q_ref[0, :, sl].astype(F32), qg_ref[0], cos_t, sin_t, bd, first_half)
        qr = qr * (HEAD_DIM ** -0.5 * LOG2E)
        h = (2 * j) // GQA_GROUP
        for u in range(qt_ref.shape[2]):
            t = qr[u * tq:(u + 1) * tq, :].T.astype(BF16)
            for e in range(2):
                piece = t[e * HEAD_DIM:(e + 1) * HEAD_DIM, :]
                blockcol = jnp.concatenate([piece, zeros] if h == 0 else [zeros, piece], axis=0)
                g = (2 * j + e) % GQA_GROUP
                qt_ref[0, h, u, :, g * tq:(g + 1) * tq] = blockcol
    kr = _norm_rope(k_ref[0].astype(F32), kg_ref[0], cos_t, sin_t, bd, first_half)
    ko_ref[0] = kr.astype(BF16)
    vt_ref[0, 0] = v_ref[0].astype(F32).T.astype(BF16)


def _qk_prep(proj, q_gain2, k_gain2, layer):
    b, l, _ = proj.shape
    tk = min(KV_BLOCK, l)
    tq = min(Q_TILE, l)
    cos_t, sin_t = _rope_tables(l)
    blk = np.arange(LANES) // HEAD_DIM
    bd = jnp.asarray((blk[:, None] == blk[None, :]).astype(np.float32), dtype=BF16)
    return pl.pallas_call(
        _qkprep_body,
        grid=(b, l // tk),
        in_specs=[pl.BlockSpec((1, tk, ATTN_WIDTH), lambda i, j: (i, j, COL_AQ)),
                  pl.BlockSpec((1, tk, KV_WIDTH), lambda i, j: (i, j, COL_AK)),
                  pl.BlockSpec((1, tk, KV_WIDTH), lambda i, j: (i, j, COL_AV)),
                  pl.BlockSpec((1, 1, LANES), lambda i, j: (layer, 0, 0)),
                  pl.BlockSpec((1, 1, LANES), lambda i, j: (layer, 0, 0)),
                  pl.BlockSpec((tk, LANES), lambda i, j: (j, 0)),
                  pl.BlockSpec((tk, LANES), lambda i, j: (j, 0)),
                  pl.BlockSpec((LANES, LANES), lambda i, j: (0, 0))],
        out_specs=[pl.BlockSpec((1, N_KV_HEADS, tk // tq, KV_WIDTH, GQA_GROUP * tq),
                                lambda i, j: (i, 0, j, 0, 0)),
                   pl.BlockSpec((1, tk, KV_WIDTH), lambda i, j: (i, j, 0)),
                   pl.BlockSpec((1, 1, KV_WIDTH, tk), lambda i, j: (i, j, 0, 0))],
        out_shape=[jax.ShapeDtypeStruct((b, N_KV_HEADS, l // tq, KV_WIDTH, GQA_GROUP * tq), BF16),
                   jax.ShapeDtypeStruct((b, l, KV_WIDTH), BF16),
                   jax.ShapeDtypeStruct((b, l // tk, KV_WIDTH, tk), BF16)],
        compiler_params=_cparams("arbitrary", "arbitrary"),
        name="qk_norm_rope",
    )(proj, proj, proj, q_gain2, k_gain2, cos_t, sin_t, bd)


def _attn_finish(acc_scr, l_fin, o_ref):
    tq = o_ref.shape[1]
    outs = []
    for h in range(N_KV_HEADS):
        o = (acc_scr[h] / l_fin[h]).T
        oh = o[:, h * HEAD_DIM:(h + 1) * HEAD_DIM]
        outs += [oh[g * tq:(g + 1) * tq, :] for g in range(GQA_GROUP)]
    o_ref[0] = jnp.concatenate(outs, axis=1).astype(BF16)


def _attn_bounded_body(bound_ref, qt_ref, k_ref, vt_ref, o_ref, acc_scr):
    nblk = vt_ref.shape[1]
    tk = vt_ref.shape[3]
    cols = qt_ref.shape[4]
    shift = bound_ref[0, 0]
    acc_scr[...] = jnp.zeros_like(acc_scr)

    def step(i, sums):
        kb = k_ref[0, pl.ds(pl.multiple_of(i * tk, tk), tk), :]
        vt = vt_ref[0, i]
        scores = [_dot(kb, qt_ref[0, h, 0]) for h in range(N_KV_HEADS)]
        new = []
        for h in range(N_KV_HEADS):
            p = jnp.exp2(scores[h] - shift)
            new.append(sums[h] + jnp.sum(p, axis=0, keepdims=True))
            acc_scr[h] = acc_scr[h] + _dot(vt, p.astype(BF16))
        return tuple(new)

    init = tuple(jnp.zeros((1, cols), F32) for _ in range(N_KV_HEADS))
    _attn_finish(acc_scr, lax.fori_loop(0, nblk, step, init, unroll=4), o_ref)


def _attn_online_body(bound_ref, qt_ref, k_ref, vt_ref, o_ref, acc_scr):
    nblk = vt_ref.shape[1]
    tk = vt_ref.shape[3]
    cols = qt_ref.shape[4]
    acc_scr[...] = jnp.zeros_like(acc_scr)
    l_fin = []
    for h in range(N_KV_HEADS):
        qt = qt_ref[0, h, 0]

        def step(i, carry):
            m_prev, l_prev = carry
            kb = k_ref[0, pl.ds(pl.multiple_of(i * tk, tk), tk), :]
            s = _dot(kb, qt)
            m_new = jnp.maximum(m_prev, jnp.max(s, axis=0, keepdims=True))
            alpha = jnp.exp2(m_prev - m_new)
            p = jnp.exp2(s - m_new)
            l_new = alpha * l_prev + jnp.sum(p, axis=0, keepdims=True)
            acc_scr[h] = alpha * acc_scr[h] + _dot(vt_ref[0, i], p.astype(BF16))
            return m_new, l_new

        init = (jnp.full((1, cols), -jnp.inf, F32), jnp.zeros((1, cols), F32))
        l_fin.append(lax.fori_loop(0, nblk, step, init)[1])
    _attn_finish(acc_scr, l_fin, o_ref)


def _attention(qt, k_rot, vt, bound):
    b, l, _ = k_rot.shape
    tq = qt.shape[4] // GQA_GROUP
    nblk, tk = vt.shape[1], vt.shape[3]

    def call(body):
        return pl.pallas_call(
            body,
            grid=(b, l // tq),
            in_specs=[pl.BlockSpec(memory_space=pltpu.SMEM),
                      pl.BlockSpec((1, N_KV_HEADS, 1, KV_WIDTH, GQA_GROUP * tq), lambda i, j: (i, 0, j, 0, 0)),
                      pl.BlockSpec((1, l, KV_WIDTH), lambda i, j: (i, 0, 0)),
                      pl.BlockSpec((1, nblk, KV_WIDTH, tk), lambda i, j: (i, 0, 0, 0))],
            out_specs=pl.BlockSpec((1, tq, ATTN_WIDTH), lambda i, j: (i, j, 0)),
            out_shape=jax.ShapeDtypeStruct((b, l, ATTN_WIDTH), BF16),
            scratch_shapes=[pltpu.VMEM((N_KV_HEADS, KV_WIDTH, GQA_GROUP * tq), F32)],
            compiler_params=_cparams("arbitrary", "arbitrary"),
            name="gqa_attention",
        )(bound, qt, k_rot, vt)

    return lax.cond(bound[0, 0] <= SAFE_SCORE_BOUND,
                    lambda: call(_attn_bounded_body), lambda: call(_attn_online_body))


def _route(logits):
    lane = lax.broadcasted_iota(jnp.int32, logits.shape, 1).astype(F32)
    neg = jnp.float32(-jnp.inf)
    lanemin = lambda cond: jnp.min(jnp.where(cond, lane, float(ROUTE_LANES)), axis=-1, keepdims=True)
    gl = jnp.where(lane < N_GROUPS, logits, neg)
    gmax = jnp.max(gl, axis=-1, keepdims=True)
    g_idx = lanemin(gl == gmax)
    pg_top = 1.0 / jnp.sum(jnp.exp(gl - gmax), axis=-1, keepdims=True)
    first = N_GROUPS + g_idx * EXPERTS_PER_GROUP
    in_group = (lane >= first) & (lane < first + EXPERTS_PER_GROUP)
    el = jnp.where(in_group, logits, neg)
    a_max = jnp.max(el, axis=-1, keepdims=True)
    a_idx = lanemin(el == a_max)
    el2 = jnp.where(lane == a_idx, neg, el)
    b_max = jnp.max(el2, axis=-1, keepdims=True)
    b_idx = lanemin(el2 == b_max)
    r = jnp.exp(b_max - a_max)
    w_a = pg_top / (1.0 + r)
    w_b = pg_top * r / (1.0 + r)
    return a_idx - N_GROUPS, b_idx - N_GROUPS, w_a, w_b


def _outproj_body(of_ref, ob_ref, hg_ref, at_ref, x_ref, mod_ref, hgain_ref, wout_ref, fgain_ref,
                  wr_ref, br_ref, tri_ref, xo_ref, h_ref, rt_ref, cnt_ref, cnt_scr, *, seg):
    tm = x_ref.shape[1]
    tile = pl.program_id(0) * pl.num_programs(1) + pl.program_id(1)

    @pl.when(tile == 0)
    def _():
        cnt_scr[...] = jnp.zeros_like(cnt_scr)

    nblk = max(1, tm // OUTPROJ_ROWS)
    blocks = [slice(k * (tm // nblk), (k + 1) * (tm // nblk)) for k in range(nblk)]
    o_hg = []
    for rows in blocks:
        parts = []
        for head in range(HG_HEADS):
            sl = slice(head * HG_DIM, (head + 1) * HG_DIM)
            o = of_ref[0, rows, sl].astype(F32) + ob_ref[0, rows, sl].astype(F32)
            o = o * lax.rsqrt(jnp.mean(o * o, axis=-1, keepdims=True) + NORM_EPS) * hgain_ref[0]
            gt = hg_ref[0, rows, sl].astype(F32)
            parts.append((o * (gt * _sigmoid(gt))).astype(BF16))
        o_hg.append(jnp.concatenate(parts, axis=1))
    mix = [_dot(o_hg[k], wout_ref[0, :HG_WIDTH, :]) + _dot(at_ref[0, rows, :], wout_ref[0, HG_WIDTH:, :])
           for k, rows in enumerate(blocks)]
    hs = []
    for k, rows in enumerate(blocks):
        x = x_ref[0, rows, :] + mod_ref[0, 2:3, :] * mix[k]
        xo_ref[0, rows, :] = x
        y = x * lax.rsqrt(jnp.mean(x * x, axis=-1, keepdims=True) + NORM_EPS) * fgain_ref[0]
        h = y * (1.0 + mod_ref[0, 4:5, :]) + mod_ref[0, 3:4, :]
        h_ref[0, rows, :] = h
        hs.append(_split_bf16(h))
    routed = []
    for h_hi, h_lo in hs:
        both = _dot(h_hi, wr_ref[0])
        logits = (both[:, :ROUTE_LANES] + both[:, ROUTE_LANES:]
                  + _dot(h_lo, wr_ref[0, :, :ROUTE_LANES]) + br_ref[0])
        routed.append(_route(logits))
    e_a, e_b, w_a, w_b = (jnp.concatenate([r[c] for r in routed], axis=0) for c in range(4))

    lane = lax.broadcasted_iota(jnp.int32, (tm, ROUTE_LANES), 1).astype(F32)
    group0 = (((tile * tm) // seg) * N_EXPERTS).astype(F32)
    hot_a = jnp.where(lane == e_a + group0, 1.0, 0.0)
    hot_b = jnp.where(lane == e_b + group0, 1.0, 0.0)
    hot = hot_a + hot_b
    before = cnt_scr[...] + _dot(tri_ref[...], hot.astype(BF16))
    rank_a = jnp.sum(hot_a * before, axis=-1, keepdims=True)
    rank_b = jnp.sum(hot_b * before, axis=-1, keepdims=True)
    cnt_scr[...] = cnt_scr[...] + jnp.sum(hot, axis=0, keepdims=True)
    cnt_ref[...] = cnt_scr[...]
    cols = (e_a, e_b, w_a, w_b, rank_a, rank_b)
    out = jnp.zeros_like(lane)
    for c, v in enumerate(cols):
        out = jnp.where(lane == float(c), v, out)
    rt_ref[0] = out


def _out_proj(o_f, o_b, proj, o_at, x, modr, mod_row0, hgain, w_out_bf, fgain, w_route, b_route, layer,
              seg):
    b, l, _ = x.shape
    tm = min(ROW_TILE, l)
    tok = lambda w, col=0: pl.BlockSpec((1, tm, w), lambda i, j: (i, j, col))
    lay = lambda *s: pl.BlockSpec((1,) + s, lambda i, j: (layer,) + (0,) * len(s))
    tri = jnp.asarray(np.tril(np.ones((tm, tm), np.float32), -1), dtype=BF16)
    return pl.pallas_call(
        functools.partial(_outproj_body, seg=seg),
        grid=(b, l // tm),
        in_specs=[tok(HG_WIDTH), tok(HG_WIDTH), tok(HG_WIDTH, COL_HG), tok(ATTN_WIDTH), tok(D_MODEL),
                  pl.BlockSpec((1, N_MOD, D_MODEL), lambda i, j: (mod_row0 + i, 0, 0)),
                  lay(1, HG_DIM), lay(D_MODEL, D_MODEL), lay(1, D_MODEL),
                  lay(D_MODEL, 2 * ROUTE_LANES), lay(1, ROUTE_LANES),
                  pl.BlockSpec((tm, tm), lambda i, j: (0, 0))],
        out_specs=[tok(D_MODEL), tok(D_MODEL), tok(ROUTE_LANES),
                   pl.BlockSpec((1, ROUTE_LANES), lambda i, j: (0, 0))],
        out_shape=[jax.ShapeDtypeStruct((b, l, D_MODEL), F32),
                   jax.ShapeDtypeStruct((b, l, D_MODEL), F32),
                   jax.ShapeDtypeStruct((b, l, ROUTE_LANES), F32),
                   jax.ShapeDtypeStruct((1, ROUTE_LANES), F32)],
        scratch_shapes=[pltpu.VMEM((1, ROUTE_LANES), F32)],
        compiler_params=_cparams("arbitrary", "arbitrary"),
        name="out_proj_router",
    )(o_f, o_b, proj, o_at, x, modr, hgain, w_out_bf, fgain, w_route, b_route, tri)


def _moe_body(te_ref, tv_ref, tu_ref, tok_ref, h_ref, *refs, steps_per_seg):
    nu = FFN_TILES_PER_STEP
    weights = [refs[3 * u:3 * u + 3] for u in range(nu)]
    o_ref, xs_scr = refs[3 * nu:]
    step = pl.program_id(0)
    tr = xs_scr.shape[2]
    cur = step % 2
    tile_of = lambda s, u: _ffn_tile_index(s, u, steps_per_seg)

    def ffn(u, xs):
        wg_ref, wu_ref, wd_ref = weights[u]
        gt = _dot(xs, wg_ref[0])
        up = _dot(xs, wu_ref[0])
        hid = (gt * _sigmoid(gt)) * up
        o_ref[0, u, 0] = _dot(hid.astype(BF16), wd_ref[0])

    any_valid = tv_ref[tile_of(step, 0)] != 0

    @pl.when(jnp.logical_and(any_valid, step % steps_per_seg == 0))
    def _():
        for u in range(nu):
            first = tu_ref[tile_of(step, u)]

            def row(r, carry):
                xs_scr[cur, u, pl.ds(r, 1), :] = h_ref[0, pl.ds(tok_ref[first + r], 1), :]
                return carry

            lax.fori_loop(0, tr, row, 0, unroll=8)

    @pl.when(any_valid)
    def _():
        nxt = jnp.minimum(step + 1, pl.num_programs(0) - 1)
        xs = [xs_scr[cur, u].astype(BF16) for u in range(nu)]
        for u in range(nu):
            first = tu_ref[tile_of(nxt, u)]
            for r in range(tr):
                xs_scr[1 - cur, u, r:r + 1, :] = h_ref[0, pl.ds(tok_ref[first + r], 1), :]
        for u in range(nu):
            ffn(u, xs[u])

    @pl.when(jnp.logical_not(any_valid))
    def _():
        o_ref[...] = jnp.zeros_like(o_ref)


def _moe_ffn(h_seg, sorted_tok, tile_expert, tile_valid, tile_first, wg, wu, wd, layer):
    nseg, seg, d = h_seg.shape
    n_tiles = tile_expert.shape[0]
    tr = EXPERT_TILE
    nu = FFN_TILES_PER_STEP
    steps_per_seg = n_tiles // nseg // nu
    wspec = lambda k, n, u: pl.BlockSpec(
        (1, k, n), lambda i, te, *_: (layer * N_EXPERTS + te[_ffn_tile_index(i, u, steps_per_seg)], 0, 0))
    wspecs = [wspec(D_MODEL, D_EXPERT, u) if j < 2 else wspec(D_EXPERT, D_MODEL, u)
              for u in range(nu) for j in range(3)]
    rows = pl.pallas_call(
        functools.partial(_moe_body, steps_per_seg=steps_per_seg),
        grid_spec=pltpu.PrefetchScalarGridSpec(
            num_scalar_prefetch=4,
            grid=(n_tiles // nu,),
            in_specs=[pl.BlockSpec((1, seg, d), lambda i, *_: (i // steps_per_seg, 0, 0),
                                   pipeline_mode=pl.Buffered(1))] + wspecs,
            out_specs=pl.BlockSpec((1, nu, 1, tr, d),
                                   lambda i, *_: (i // steps_per_seg, 0, i % steps_per_seg, 0, 0)),
            scratch_shapes=[pltpu.VMEM((2, nu, tr, d), F32)]),
        out_shape=jax.ShapeDtypeStruct((nseg, nu, steps_per_seg, tr, d), F32),
        compiler_params=_cparams("arbitrary"),
        name="routed_expert_ffn",
    )(tile_expert, tile_valid, tile_first, sorted_tok, h_seg, *([wg, wu, wd] * nu))
    return rows.reshape(n_tiles * tr, d)


def _ffn_tile_index(step, slot, steps_per_seg):
    seg = step // steps_per_seg
    return (seg * FFN_TILES_PER_STEP + slot) * steps_per_seg + step % steps_per_seg


def _combine_body(pa_ref, pb_ref, rows_ref, rt_ref, x_ref, mod_ref, fin_ref, o_ref, ab_scr, *,
                  final, seg_tiles):
    tm = o_ref.shape[1]
    tile = pl.program_id(0) * pl.num_programs(1) + pl.program_id(1)
    cur = tile % 2
    pos = (pa_ref, pb_ref)

    @pl.when(tile % seg_tiles == 0)
    def _():
        def gather(r, carry):
            for slot in range(2):
                ab_scr[cur, slot, pl.ds(r, 1), :] = rows_ref[0, pl.ds(pos[slot][tile * tm + r], 1), :]
            return carry

        lax.fori_loop(0, tm, gather, 0, unroll=8)

    rt = rt_ref[0]
    moe = rt[:, 2:3] * ab_scr[cur, 0] + rt[:, 3:4] * ab_scr[cur, 1]
    nxt = jnp.minimum(tile + 1, pl.num_programs(0) * pl.num_programs(1) - 1) * tm
    for r in range(tm):
        for slot in range(2):
            ab_scr[1 - cur, slot, r:r + 1, :] = rows_ref[0, pl.ds(pos[slot][nxt + r], 1), :]
    x = x_ref[0] + mod_ref[0, 5:6, :] * moe
    if final:
        x = x * lax.rsqrt(jnp.mean(x * x, axis=-1, keepdims=True) + NORM_EPS) * fin_ref[...]
    o_ref[0] = x


def _combine(pos_a, pos_b, rows_seg, route, x, modr, mod_row0, final_gain, final):
    b, l, _ = x.shape
    nseg, rseg, d = rows_seg.shape
    tm = min(COMBINE_TILE, l)
    tiles_l = l // tm
    seg_tiles = (b * tiles_l) // nseg
    tok = lambda w: pl.BlockSpec((1, tm, w), lambda i, j, *_: (i, j, 0))
    return pl.pallas_call(
        functools.partial(_combine_body, final=final, seg_tiles=seg_tiles),
        grid_spec=pltpu.PrefetchScalarGridSpec(
            num_scalar_prefetch=2,
            grid=(b, tiles_l),
            in_specs=[pl.BlockSpec((1, rseg, d), lambda i, j, *_: ((i * tiles_l + j) // seg_tiles, 0, 0),
                                   pipeline_mode=pl.Buffered(1)),
                      tok(ROUTE_LANES), tok(D_MODEL),
                      pl.BlockSpec((1, N_MOD, D_MODEL), lambda i, j, *_: (mod_row0 + i, 0, 0)),
                      pl.BlockSpec((1, D_MODEL), lambda i, j, *_: (0, 0))],
            out_specs=tok(D_MODEL),
            scratch_shapes=[pltpu.VMEM((2, 2, tm, d), F32)]),
        out_shape=jax.ShapeDtypeStruct((b, l, D_MODEL), F32),
        compiler_params=_cparams("arbitrary", "arbitrary"),
        name="moe_combine",
    )(pos_a, pos_b, rows_seg, route, x, modr, final_gain)


def _routing_tables(route, counts, nseg, seg):
    t = route.shape[0]
    tr = EXPERT_TILE
    rseg = 2 * seg + N_EXPERTS * tr
    ngrp = nseg * N_EXPERTS
    cnt = counts[0, :ngrp].astype(jnp.int32)
    pad2 = (((cnt + tr - 1) // tr) * tr).reshape(nseg, N_EXPERTS)
    end_local = jnp.cumsum(pad2, axis=1)
    start_local = (end_local - pad2).reshape(-1)
    start_unp = jnp.cumsum(cnt) - cnt
    ids = route[:, 0:2].astype(jnp.int32)
    rank = route[:, 4:6].astype(jnp.int32)
    key = (jnp.arange(t, dtype=jnp.int32) // seg)[:, None] * N_EXPERTS + ids
    hot = key[:, :, None] == jnp.arange(ngrp, dtype=jnp.int32)[None, None, :]
    lookup = lambda table: jnp.sum(jnp.where(hot, table[None, None, :], 0), axis=-1)
    pos = lookup(start_local) + rank
    order = lookup(start_unp) + rank
    tok = jnp.broadcast_to((jnp.arange(t, dtype=jnp.int32) % seg)[:, None], (t, 2))
    _, sorted_tok = lax.sort((order.reshape(-1), tok.reshape(-1)), num_keys=1)
    sorted_tok = jnp.concatenate([sorted_tok, jnp.zeros((tr,), jnp.int32)])
    tiles_per_seg = rseg // tr
    local_row = jnp.arange(tiles_per_seg, dtype=jnp.int32) * tr
    tile_grp = jnp.sum(local_row[None, :, None] >= end_local[:, None, :], axis=-1)
    tile_valid = (tile_grp < N_EXPERTS).astype(jnp.int32)
    tile_expert = jnp.minimum(tile_grp, N_EXPERTS - 1).astype(jnp.int32)
    grp = tile_expert + (jnp.arange(nseg, dtype=jnp.int32) * N_EXPERTS)[:, None]
    hot_t = grp[:, :, None] == jnp.arange(ngrp, dtype=jnp.int32)[None, None, :]
    skipped_pad = jnp.sum(jnp.where(hot_t, (start_local - start_unp)[None, None, :], 0), axis=-1)
    tile_first = (local_row[None, :] - skipped_pad) * tile_valid
    return (sorted_tok, pos[:, 0], pos[:, 1], tile_expert.reshape(-1), tile_valid.reshape(-1),
            tile_first.reshape(-1), rseg)


def _trunk(x, modr, mod_row0, nb, params):
    b, l, _ = x.shape
    t = b * l
    seg = min(MOE_SEGMENT, t)
    nseg = t // seg
    depth = params["w_in"].shape[0]
    for layer in range(depth):
        row0 = layer * nb + mod_row0
        proj, fx_min = _in_proj(x, modr, row0, params["norm_mix"], params["w_in"], layer)
        o_f, o_b = _hgrn(proj, params["lbs"], layer, fx_min)
        qt, k_rot, vt = _qk_prep(proj, params["q_gain2"], params["k_gain2"], layer)
        o_at = _attention(qt, k_rot, vt, params["score_bound"][layer])
        x_mid, h, route, counts = _out_proj(o_f, o_b, proj, o_at, x, modr, row0, params["hg_gain"],
                                            params["w_out"], params["norm_ffn"], params["w_route"],
                                            params["b_route"], layer, seg)
        sorted_tok, pos_a, pos_b, tile_e, tile_v, tile_f, rseg = _routing_tables(
            route.reshape(t, ROUTE_LANES), counts, nseg, seg)
        rows = _moe_ffn(h.reshape(nseg, seg, D_MODEL), sorted_tok, tile_e, tile_v, tile_f,
                        params["w_e_gate"], params["w_e_up"], params["w_e_down"], layer)
        x = _combine(pos_a, pos_b, rows.reshape(nseg, rseg, D_MODEL), route, x_mid, modr, row0,
                     params["final_norm"], final=(layer == depth - 1))
    return x


def kernel(x_prompt, x_sample, c_prompt, c_sample, w_in, w_out, hg_lb, hg_out_norm, q_norm, k_norm,
           norm_mix, norm_ffn, w_ada, b_ada, w_group, b_group, w_router, b_router, w_e_gate, w_e_up,
           w_e_down, final_norm):
    depth = w_in.shape[0]
    bp, bs = c_prompt.shape[0], c_sample.shape[0]
    nb = -(-(bp + bs) // SUBLANES) * SUBLANES
    c_all = jnp.zeros((nb, D_MODEL), F32).at[:bp].set(c_prompt).at[bp:bp + bs].set(c_sample)
    mod = _modulation(c_all, w_ada, b_ada)
    modr = mod.reshape(depth * nb, N_MOD, D_MODEL)
    pad = ROUTE_LANES - N_GROUPS - N_EXPERTS
    params = {
        "w_in": w_in.astype(BF16),
        "w_out": w_out.astype(BF16),
        "lbs": _lower_bounds(hg_lb).reshape(depth, 1, HG_WIDTH),
        "hg_gain": hg_out_norm.reshape(depth, 1, HG_DIM),
        "q_gain2": jnp.tile(q_norm, (1, LANES // HEAD_DIM)).reshape(depth, 1, LANES),
        "k_gain2": jnp.tile(k_norm, (1, LANES // HEAD_DIM)).reshape(depth, 1, LANES),
        "score_bound": (1.02 * LOG2E * HEAD_DIM ** 0.5 * jnp.max(jnp.abs(q_norm), axis=1)
                        * jnp.max(jnp.abs(k_norm), axis=1)).reshape(depth, 1, 1),
        "norm_mix": norm_mix.reshape(depth, 1, D_MODEL),
        "norm_ffn": norm_ffn.reshape(depth, 1, D_MODEL),
        "w_route": jnp.concatenate(_split_bf16(jnp.pad(jnp.concatenate([w_group, w_router], axis=-1),
                                                       ((0, 0), (0, 0), (0, pad)))), axis=-1),
        "b_route": jnp.pad(jnp.concatenate([b_group, b_router], axis=-1),
                           ((0, 0), (0, pad))).reshape(depth, 1, ROUTE_LANES),
        "w_e_gate": w_e_gate.astype(BF16).reshape(depth * N_EXPERTS, D_MODEL, D_EXPERT),
        "w_e_up": w_e_up.astype(BF16).reshape(depth * N_EXPERTS, D_MODEL, D_EXPERT),
        "w_e_down": w_e_down.astype(BF16).reshape(depth * N_EXPERTS, D_EXPERT, D_MODEL),
        "final_norm": final_norm.reshape(1, D_MODEL),
    }
    y_prompt = _trunk(x_prompt, modr, 0, nb, params)
    y_sample = _trunk(x_sample, modr, bp, nb, params)
    return (y_prompt, y_sample)
```

```python
import functools

import numpy as np
import jax
import jax.numpy as jnp
from jax import lax
from jax.experimental import pallas as pl
from jax.experimental.pallas import tpu as pltpu

F32 = jnp.float32
BF16 = jnp.bfloat16

D_MODEL = 1024
HG_WIDTH = 512
HG_HEADS = 4
HG_DIM = 128
LB_FLOOR = 1e-30
ATTN_WIDTH = 512
HEAD_DIM = 64
N_Q_HEADS = 8
N_KV_HEADS = 2
GQA_GROUP = 4
KV_WIDTH = 128
GRID_W = 64
ROPE_AXIS_DIM = 32
ROPE_THETA = 10000.0
N_GROUPS = 4
EXPERTS_PER_GROUP = 4
N_EXPERTS = 16
D_EXPERT = 512
N_MOD = 6
NORM_EPS = 1e-6
IN_PROJ_WIDTH = 5 * HG_WIDTH + ATTN_WIDTH + 2 * KV_WIDTH
COL_HQ, COL_HF_FWD, COL_HF_BWD, COL_HI, COL_HG, COL_AQ = 0, 1, 2, 3, 4, 5
COL_AK, COL_AV = 24, 25

LANES = 128
SUBLANES = 8
VMEM_LIMIT = 56 * 1024 * 1024
ROW_TILE = 512
OUTPROJ_TILE = 1024
OUTPROJ_ROWS = 1024
HG_CHUNK = 128
KV_BLOCK = 512
Q_TILE = 256
EXPERT_TILE = 128
FFN_TILES_PER_STEP = 4
COMBINE_TILE = 256
MOE_SEGMENT = 4096
ROUTE_LANES = 128
LOG2E = 1.4426950408889634
SAFE_SCORE_BOUND = 40.0


def _cparams(*sem):
    return pltpu.CompilerParams(dimension_semantics=sem, vmem_limit_bytes=VMEM_LIMIT)


def _dot(a, b):
    return jnp.dot(a, b, preferred_element_type=F32)


def _dot_nt(a, b):
    return lax.dot_general(a, b, (((1,), (1,)), ((), ())), preferred_element_type=F32)


def _split_bf16(x):
    hi = x.astype(BF16)
    lo = (x - hi.astype(F32)).astype(BF16)
    return hi, lo


def _sigmoid(x):
    return 1.0 / (1.0 + jnp.exp(-x))


def _mod_body(c_ref, w_ref, b_ref, o_ref):
    c = c_ref[...]
    cs = c * _sigmoid(c)
    o_ref[0] = _dot(cs.astype(BF16), w_ref[0].astype(BF16)) + b_ref[0]


def _modulation(c_all, w_ada, b_ada):
    nb = c_all.shape[0]
    depth, _, n = w_ada.shape
    tn = 1024
    return pl.pallas_call(
        _mod_body,
        grid=(depth, n // tn),
        in_specs=[pl.BlockSpec((nb, D_MODEL), lambda l, j: (0, 0)),
                  pl.BlockSpec((1, D_MODEL, tn), lambda l, j: (l, 0, j)),
                  pl.BlockSpec((1, 1, tn), lambda l, j: (l, 0, j))],
        out_specs=pl.BlockSpec((1, nb, tn), lambda l, j: (l, 0, j)),
        out_shape=jax.ShapeDtypeStruct((depth, nb, n), F32),
        compiler_params=_cparams("arbitrary", "arbitrary"),
        name="adaln_modulation",
    )(c_all, w_ada, b_ada.reshape(depth, 1, n))


def _lb_body(p_ref, o_ref):
    depth = p_ref.shape[0]
    rows = [p_ref[l:l + 1, :] for l in range(depth)]
    m = functools.reduce(jnp.maximum, rows)
    es = [jnp.exp(r - m) for r in rows]
    tot = functools.reduce(lambda a, b: a + b, es)
    sm = [e / tot for e in es]
    acc = jnp.zeros_like(sm[0])
    for l in range(depth):
        acc = acc + sm[l]
        o_ref[l:l + 1, :] = acc - sm[0]


def _lower_bounds(hg_lb):
    return pl.pallas_call(
        _lb_body,
        out_shape=jax.ShapeDtypeStruct(hg_lb.shape, F32),
        name="hgrn_lower_bounds",
    )(hg_lb)


def _inproj_body(x_ref, mod_ref, gain_ref, w_ref, p_ref, fmin_ref):
    x = x_ref[0]
    ms = jnp.mean(x * x, axis=-1, keepdims=True)
    y = x * lax.rsqrt(ms + NORM_EPS) * gain_ref[0]
    h = y * (1.0 + mod_ref[0, 1:2, :]) + mod_ref[0, 0:1, :]
    proj = _dot(h.astype(BF16), w_ref[0]).astype(BF16)
    p_ref[0] = proj
    lo, hi = COL_HF_FWD * HG_WIDTH, (COL_HF_BWD + 1) * HG_WIDTH
    cur = jnp.min(proj[:, lo:hi].astype(F32), axis=0, keepdims=True)
    first = jnp.logical_and(pl.program_id(0) == 0, pl.program_id(1) == 0)

    @pl.when(first)
    def _():
        fmin_ref[...] = cur

    @pl.when(jnp.logical_not(first))
    def _():
        fmin_ref[...] = jnp.minimum(fmin_ref[...], cur)


def _in_proj(x, modr, mod_row0, gain, w_in_bf, layer):
    b, l, _ = x.shape
    tm = min(ROW_TILE, l)
    return pl.pallas_call(
        _inproj_body,
        grid=(b, l // tm),
        in_specs=[pl.BlockSpec((1, tm, D_MODEL), lambda i, j: (i, j, 0)),
                  pl.BlockSpec((1, N_MOD, D_MODEL), lambda i, j: (mod_row0 + i, 0, 0)),
                  pl.BlockSpec((1, 1, D_MODEL), lambda i, j: (layer, 0, 0)),
                  pl.BlockSpec((1, D_MODEL, IN_PROJ_WIDTH), lambda i, j: (layer, 0, 0))],
        out_specs=[pl.BlockSpec((1, tm, IN_PROJ_WIDTH), lambda i, j: (i, j, 0)),
                   pl.BlockSpec((1, 2 * HG_WIDTH), lambda i, j: (0, 0))],
        out_shape=[jax.ShapeDtypeStruct((b, l, IN_PROJ_WIDTH), BF16),
                   jax.ShapeDtypeStruct((1, 2 * HG_WIDTH), F32)],
        compiler_params=_cparams("arbitrary", "arbitrary"),
        name="in_proj",
    )(x, modr, gain, w_in_bf)


HG_LEVELS = (64, 32, 16, 8, 4, 2, 1)
HG_INBLOCK = 8
HG_INBLOCK_MAX_EXPONENT = 80.0
HG_HEAD_GROUPS = ((0, 1, 2, 3),)
MASK_DIAG = len(HG_LEVELS)
MASK_INBLOCK = len(HG_LEVELS) + 1


def _hg_constants():
    c = HG_CHUNK
    t = np.arange(c)[:, None]
    s = np.arange(c)[None, :]
    masks = np.zeros((2, len(HG_LEVELS) + 2, c, c), np.float32)
    for li, h in enumerate(HG_LEVELS):
        same = (t // (2 * h)) == (s // (2 * h))
        masks[0, li] = same & ((t % (2 * h)) >= h) & ((s % (2 * h)) < h)
        masks[1, li] = same & ((t % (2 * h)) < h) & ((s % (2 * h)) >= h)
    masks[:, MASK_DIAG] = (t == s)
    same = (t // HG_INBLOCK) == (s // HG_INBLOCK)
    masks[0, MASK_INBLOCK] = same & (s <= t)
    masks[1, MASK_INBLOCK] = same & (s >= t)
    cum = np.stack([(s <= t), (s >= t)]).astype(np.float32)
    return jnp.asarray(masks), jnp.asarray(cum, dtype=BF16)


def _row_bcast(a, rows, blk):
    parts = [jnp.broadcast_to(a[r:r + 1, :], (blk, a.shape[1])) for r in rows]
    return parts[0] if len(parts) == 1 else jnp.concatenate(parts, axis=0)


def _level_reference(cum, h, reverse, sub):
    c = cum.shape[0]
    if 2 * h >= SUBLANES:
        rows = [p0 + (h if reverse else h - 1) for p0 in range(0, c, 2 * h)]
        return _row_bcast(cum, rows, 2 * h)
    out = None
    for p0 in reversed(range(0, SUBLANES, 2 * h)):
        r = p0 + (h if reverse else h - 1)
        piece = _row_bcast(cum, [v0 + r for v0 in range(0, c, SUBLANES)], SUBLANES)
        out = piece if out is None else jnp.where(sub < p0 + 2 * h, piece, out)
    return out


def _hg_chunks(chains, lb_ref, cum_ref, masks_ref, st_ref, inblock):
    c = HG_CHUNK
    sub = lax.broadcasted_iota(jnp.int32, (c, HG_DIM), 0) % SUBLANES
    work = []
    for head, d, q_ref, f_ref, i_ref, o_ref in chains:
        sl = slice(head * HG_DIM, (head + 1) * HG_DIM)
        lb = lb_ref[0, :, sl]
        hq = q_ref[0, :, sl].astype(F32)
        sig = _sigmoid(f_ref[0, :, sl].astype(F32))
        one_m_lb = 1.0 - lb
        w = dict(head=head, d=d, sl=sl, o_ref=o_ref, hi=i_ref[0, :, sl], q=hq * _sigmoid(hq),
                 g=jnp.log(jnp.maximum(lb, LB_FLOOR) + one_m_lb * sig),
                 kk=one_m_lb * (1.0 - sig))
        work.append(w)
    for w in work:
        g_hi, g_lo = _split_bf16(w["g"])
        w["cum"] = _dot(cum_ref[w["d"]], g_hi) + _dot(cum_ref[w["d"]], g_lo)
        w["q_bf"] = w["q"].astype(BF16)
        w["kk_bf"] = w["kk"].astype(BF16)
        w["a"] = jnp.zeros((c, c), F32)

    for li, h in enumerate(HG_LEVELS):
        if inblock and 2 * h <= HG_INBLOCK:
            continue
        for w in work:
            ref = _level_reference(w["cum"], h, w["d"] == 1, sub)
            e = jnp.exp2((jnp.abs(w["cum"] - ref) * (-LOG2E)).astype(BF16))
            w["a"] = w["a"] + _dot_nt(w["q_bf"] * e, w["kk_bf"] * e) * masks_ref[w["d"], li]
    for w in work:
        q, kk, cum, d = w["q"], w["kk"], w["cum"], w["d"]
        if inblock:
            first = HG_INBLOCK - 1 if d == 1 else 0
            ref = _row_bcast(cum, [p0 + first for p0 in range(0, c, HG_INBLOCK)], HG_INBLOCK)
            z = (cum - ref) * LOG2E
            al = _dot_nt((q * jnp.exp2(z)).astype(BF16), (kk * jnp.exp2(-z)).astype(BF16))
            w["a"] = w["a"] + al * masks_ref[d, MASK_INBLOCK]
        else:
            w["a"] = w["a"] + jnp.sum(q * kk, axis=-1, keepdims=True) * masks_ref[d, MASK_DIAG]
    for w in work:
        q, kk, cum, d, head = w["q"], w["kk"], w["cum"], w["d"], w["head"]
        hi = w["hi"]
        st = st_ref[d, head]
        tot = cum[0:1, :] if d == 1 else cum[c - 1:c, :]
        o = _dot(w["a"].astype(BF16), hi)
        o = o + _dot_nt((q * jnp.exp(cum)).astype(BF16), st.astype(BF16))
        k_end = (kk * jnp.exp(tot - cum)).astype(BF16)
        st_ref[d, head] = jnp.exp(tot) * st + _dot(hi.astype(F32).T.astype(BF16), k_end)
        w["o_ref"][0, :, w["sl"]] = o.astype(BF16)


def _hgrn_body(qf_ref, ff_ref, if_ref, qb_ref, fb_ref, ib_ref, lb_ref, cum_ref, masks_ref,
               of_ref, ob_ref, st_ref, *, inblock):
    @pl.when(pl.program_id(1) == 0)
    def _():
        st_ref[...] = jnp.zeros_like(st_ref)

    for heads in HG_HEAD_GROUPS:
        chains = [(head, d) + refs for head in heads
                  for d, refs in enumerate(((qf_ref, ff_ref, if_ref, of_ref),
                                            (qb_ref, fb_ref, ib_ref, ob_ref)))]
        _hg_chunks(chains, lb_ref, cum_ref, masks_ref, st_ref, inblock)


def _hgrn(proj, lbs, layer, fx_min):
    b, l, _ = proj.shape
    c = HG_CHUNK
    nc = l // c
    masks, cum = _hg_constants()
    fwd = lambda col: pl.BlockSpec((1, c, HG_WIDTH), lambda i, j: (i, j, col))
    bwd = lambda col: pl.BlockSpec((1, c, HG_WIDTH), lambda i, j: (i, nc - 1 - j, col))
    out_sd = jax.ShapeDtypeStruct((b, l, HG_WIDTH), BF16)

    def call(inblock):
        return pl.pallas_call(
            functools.partial(_hgrn_body, inblock=inblock),
            grid=(b, nc),
            in_specs=[fwd(COL_HQ), fwd(COL_HF_FWD), fwd(COL_HI),
                      bwd(COL_HQ), bwd(COL_HF_BWD), bwd(COL_HI),
                      pl.BlockSpec((1, 1, HG_WIDTH), lambda i, j: (layer, 0, 0)),
                      pl.BlockSpec(cum.shape, lambda i, j: (0, 0, 0)),
                      pl.BlockSpec(masks.shape, lambda i, j: (0, 0, 0, 0))],
            out_specs=[pl.BlockSpec((1, c, HG_WIDTH), lambda i, j: (i, j, 0)),
                       pl.BlockSpec((1, c, HG_WIDTH), lambda i, j: (i, nc - 1 - j, 0))],
            out_shape=[out_sd, out_sd],
            scratch_shapes=[pltpu.VMEM((2, HG_HEADS, HG_DIM, HG_DIM), F32)],
            compiler_params=_cparams("arbitrary", "arbitrary"),
            name="hgrn2_recurrence",
        )(proj, proj, proj, proj, proj, proj, lbs, cum, masks)

    lb = jnp.tile(lbs[layer], (1, 2))
    worst = jnp.log(jnp.maximum(lb, LB_FLOOR) + (1.0 - lb) * _sigmoid(fx_min))
    safe = (HG_INBLOCK - 1) * jnp.max(-worst) <= HG_INBLOCK_MAX_EXPONENT
    return lax.cond(safe, lambda: call(True), lambda: call(False))


def _rope_tables(l):
    lane = np.arange(LANES)
    dd = lane % HEAD_DIM
    axis = dd // ROPE_AXIS_DIM
    first_half = (dd % ROPE_AXIS_DIM) < (ROPE_AXIS_DIM // 2)
    freq_idx = dd % (ROPE_AXIS_DIM // 2)
    inv_freq = ROPE_THETA ** (-jnp.arange(0, ROPE_AXIS_DIM, 2, dtype=F32) / ROPE_AXIS_DIM)
    t = jnp.arange(l)
    pos = jnp.where(jnp.asarray(axis)[None, :] == 0, (t // GRID_W)[:, None], (t % GRID_W)[:, None])
    ang = pos.astype(F32) * inv_freq[jnp.asarray(freq_idx)][None, :]
    sign = jnp.where(jnp.asarray(first_half), -1.0, 1.0)[None, :]
    return jnp.cos(ang), jnp.sin(ang) * sign


def _norm_rope(x, gain, cos_t, sin_t, bd, first_half):
    x2_hi, x2_lo = _split_bf16(x * x)
    ss = _dot(x2_hi, bd) + _dot(x2_lo, bd)
    xn = x * lax.rsqrt(ss * (1.0 / HEAD_DIM) + NORM_EPS) * gain
    half = ROPE_AXIS_DIM // 2
    partner = jnp.where(first_half, pltpu.roll(xn, LANES - half, 1), pltpu.roll(xn, half, 1))
    return xn * cos_t + partner * sin_t


def _qkprep_body(q_ref, k_ref, v_ref, qg_ref, kg_ref, cos_ref, sin_ref, bd_ref, qt_ref, ko_ref, vt_ref):
    tq = qt_ref.shape[4] // GQA_GROUP
    cos_t = cos_ref[...]
    sin_t = sin_ref[...]
    bd = bd_ref[...]
    lane = lax.broadcasted_iota(jnp.int32, cos_t.shape, 1)
    first_half = (lane % ROPE_AXIS_DIM) < (ROPE_AXIS_DIM // 2)
    zeros = jnp.zeros((HEAD_DIM, tq), BF16)
    for j in range(ATTN_WIDTH // LANES):
        sl = slice(j * LANES, (j + 1) * LANES)
        qr = _norm_rope(q_ref[0, :, sl].astype(F32), qg_ref[0], cos_t, sin_t, bd, first_half)
        qr = qr * (HEAD_DIM ** -0.5 * LOG2E)
        h = (2 * j) // GQA_GROUP
        for u in range(qt_ref.shape[2]):
            t = qr[u * tq:(u + 1) * tq, :].T.astype(BF16)
            for e in range(2):
                piece = t[e * HEAD_DIM:(e + 1) * HEAD_DIM, :]
                blockcol = jnp.concatenate([piece, zeros] if h == 0 else [zeros, piece], axis=0)
                g = (2 * j + e) % GQA_GROUP
                qt_ref[0, h, u, :, g * tq:(g + 1) * tq] = blockcol
    kr = _norm_rope(k_ref[0].astype(F32), kg_ref[0], cos_t, sin_t, bd, first_half)
    ko_ref[0] = kr.astype(BF16)
    vt_ref[0, 0] = v_ref[0].astype(F32).T.astype(BF16)


def _qk_prep(proj, q_gain2, k_gain2, layer):
    b, l, _ = proj.shape
    tk = min(KV_BLOCK, l)
    tq = min(Q_TILE, l)
    cos_t, sin_t = _rope_tables(l)
    blk = np.arange(LANES) // HEAD_DIM
    bd = jnp.asarray((blk[:, None] == blk[None, :]).astype(np.float32), dtype=BF16)
    return pl.pallas_call(
        _qkprep_body,
        grid=(b, l // tk),
        in_specs=[pl.BlockSpec((1, tk, ATTN_WIDTH), lambda i, j: (i, j, COL_AQ)),
                  pl.BlockSpec((1, tk, KV_WIDTH), lambda i, j: (i, j, COL_AK)),
                  pl.BlockSpec((1, tk, KV_WIDTH), lambda i, j: (i, j, COL_AV)),
                  pl.BlockSpec((1, 1, LANES), lambda i, j: (layer, 0, 0)),
                  pl.BlockSpec((1, 1, LANES), lambda i, j: (layer, 0, 0)),
                  pl.BlockSpec((tk, LANES), lambda i, j: (j, 0)),
                  pl.BlockSpec((tk, LANES), lambda i, j: (j, 0)),
                  pl.BlockSpec((LANES, LANES), lambda i, j: (0, 0))],
        out_specs=[pl.BlockSpec((1, N_KV_HEADS, tk // tq, KV_WIDTH, GQA_GROUP * tq),
                                lambda i, j: (i, 0, j, 0, 0)),
                   pl.BlockSpec((1, tk, KV_WIDTH), lambda i, j: (i, j, 0)),
                   pl.BlockSpec((1, 1, KV_WIDTH, tk), lambda i, j: (i, j, 0, 0))],
        out_shape=[jax.ShapeDtypeStruct((b, N_KV_HEADS, l // tq, KV_WIDTH, GQA_GROUP * tq), BF16),
                   jax.ShapeDtypeStruct((b, l, KV_WIDTH), BF16),
                   jax.ShapeDtypeStruct((b, l // tk, KV_WIDTH, tk), BF16)],
        compiler_params=_cparams("arbitrary", "arbitrary"),
        name="qk_norm_rope",
    )(proj, proj, proj, q_gain2, k_gain2, cos_t, sin_t, bd)


def _attn_finish(acc_scr, l_fin, o_ref):
    tq = o_ref.shape[1]
    outs = []
    for h in range(N_KV_HEADS):
        o = (acc_scr[h] / l_fin[h]).T
        oh = o[:, h * HEAD_DIM:(h + 1) * HEAD_DIM]
        outs += [oh[g * tq:(g + 1) * tq, :] for g in range(GQA_GROUP)]
    o_ref[0] = jnp.concatenate(outs, axis=1).astype(BF16)


def _attn_bounded_body(bound_ref, qt_ref, k_ref, vt_ref, o_ref, acc_scr):
    nblk = vt_ref.shape[1]
    tk = vt_ref.shape[3]
    cols = qt_ref.shape[4]
    shift = bound_ref[0, 0]
    acc_scr[...] = jnp.zeros_like(acc_scr)

    def step(i, sums):
        kb = k_ref[0, pl.ds(pl.multiple_of(i * tk, tk), tk), :]
        vt = vt_ref[0, i]
        scores = [_dot(kb, qt_ref[0, h, 0]) for h in range(N_KV_HEADS)]
        new = []
        for h in range(N_KV_HEADS):
            p = jnp.exp2(scores[h] - shift)
            new.append(sums[h] + jnp.sum(p, axis=0, keepdims=True))
            acc_scr[h] = acc_scr[h] + _dot(vt, p.astype(BF16))
        return tuple(new)

    init = tuple(jnp.zeros((1, cols), F32) for _ in range(N_KV_HEADS))
    _attn_finish(acc_scr, lax.fori_loop(0, nblk, step, init, unroll=4), o_ref)


def _attn_online_body(bound_ref, qt_ref, k_ref, vt_ref, o_ref, acc_scr):
    nblk = vt_ref.shape[1]
    tk = vt_ref.shape[3]
    cols = qt_ref.shape[4]
    acc_scr[...] = jnp.zeros_like(acc_scr)
    l_fin = []
    for h in range(N_KV_HEADS):
        qt = qt_ref[0, h, 0]

        def step(i, carry):
            m_prev, l_prev = carry
            kb = k_ref[0, pl.ds(pl.multiple_of(i * tk, tk), tk), :]
            s = _dot(kb, qt)
            m_new = jnp.maximum(m_prev, jnp.max(s, axis=0, keepdims=True))
            alpha = jnp.exp2(m_prev - m_new)
            p = jnp.exp2(s - m_new)
            l_new = alpha * l_prev + jnp.sum(p, axis=0, keepdims=True)
            acc_scr[h] = alpha * acc_scr[h] + _dot(vt_ref[0, i], p.astype(BF16))
            return m_new, l_new

        init = (jnp.full((1, cols), -jnp.inf, F32), jnp.zeros((1, cols), F32))
        l_fin.append(lax.fori_loop(0, nblk, step, init)[1])
    _attn_finish(acc_scr, l_fin, o_ref)


def _attention(qt, k_rot, vt, bound):
    b, l, _ = k_rot.shape
    tq = qt.shape[4] // GQA_GROUP
    nblk, tk = vt.shape[1], vt.shape[3]

    def call(body):
        return pl.pallas_call(
            body,
            grid=(b, l // tq),
            in_specs=[pl.BlockSpec(memory_space=pltpu.SMEM),
                      pl.BlockSpec((1, N_KV_HEADS, 1, KV_WIDTH, GQA_GROUP * tq), lambda i, j: (i, 0, j, 0, 0)),
                      pl.BlockSpec((1, l, KV_WIDTH), lambda i, j: (i, 0, 0)),
                      pl.BlockSpec((1, nblk, KV_WIDTH, tk), lambda i, j: (i, 0, 0, 0))],
            out_specs=pl.BlockSpec((1, tq, ATTN_WIDTH), lambda i, j: (i, j, 0)),
            out_shape=jax.ShapeDtypeStruct((b, l, ATTN_WIDTH), BF16),
            scratch_shapes=[pltpu.VMEM((N_KV_HEADS, KV_WIDTH, GQA_GROUP * tq), F32)],
            compiler_params=_cparams("arbitrary", "arbitrary"),
            name="gqa_attention",
        )(bound, qt, k_rot, vt)

    return lax.cond(bound[0, 0] <= SAFE_SCORE_BOUND,
                    lambda: call(_attn_bounded_body), lambda: call(_attn_online_body))


def _route(logits):
    lane = lax.broadcasted_iota(jnp.int32, logits.shape, 1).astype(F32)
    neg = jnp.float32(-jnp.inf)
    lanemin = lambda cond: jnp.min(jnp.where(cond, lane, float(ROUTE_LANES)), axis=-1, keepdims=True)
    gl = jnp.where(lane < N_GROUPS, logits, neg)
    gmax = jnp.max(gl, axis=-1, keepdims=True)
    g_idx = lanemin(gl == gmax)
    pg_top = 1.0 / jnp.sum(jnp.exp(gl - gmax), axis=-1, keepdims=True)
    first = N_GROUPS + g_idx * EXPERTS_PER_GROUP
    in_group = (lane >= first) & (lane < first + EXPERTS_PER_GROUP)
    el = jnp.where(in_group, logits, neg)
    a_max = jnp.max(el, axis=-1, keepdims=True)
    a_idx = lanemin(el == a_max)
    el2 = jnp.where(lane == a_idx, neg, el)
    b_max = jnp.max(el2, axis=-1, keepdims=True)
    b_idx = lanemin(el2 == b_max)
    r = jnp.exp(b_max - a_max)
    w_a = pg_top / (1.0 + r)
    w_b = pg_top * r / (1.0 + r)
    return a_idx - N_GROUPS, b_idx - N_GROUPS, w_a, w_b


def _outproj_body(of_ref, ob_ref, hg_ref, at_ref, x_ref, mod_ref, hgain_ref, wout_ref, fgain_ref,
                  wr_ref, br_ref, tri_ref, xo_ref, h_ref, rt_ref, rtt_ref, cnt_ref, cnt_scr, *, seg):
    tm = x_ref.shape[1]
    tile = pl.program_id(0) * pl.num_programs(1) + pl.program_id(1)

    @pl.when(tile == 0)
    def _():
        cnt_scr[...] = jnp.zeros_like(cnt_scr)

    nblk = max(1, tm // OUTPROJ_ROWS)
    blocks = [slice(k * (tm // nblk), (k + 1) * (tm // nblk)) for k in range(nblk)]
    o_hg = []
    for rows in blocks:
        parts = []
        for head in range(HG_HEADS):
            sl = slice(head * HG_DIM, (head + 1) * HG_DIM)
            o = of_ref[0, rows, sl].astype(F32) + ob_ref[0, rows, sl].astype(F32)
            o = o * lax.rsqrt(jnp.mean(o * o, axis=-1, keepdims=True) + NORM_EPS) * hgain_ref[0]
            gt = hg_ref[0, rows, sl].astype(F32)
            parts.append((o * (gt * _sigmoid(gt))).astype(BF16))
        o_hg.append(jnp.concatenate(parts, axis=1))
    mix = [_dot(o_hg[k], wout_ref[0, :HG_WIDTH, :]) + _dot(at_ref[0, rows, :], wout_ref[0, HG_WIDTH:, :])
           for k, rows in enumerate(blocks)]
    hs = []
    for k, rows in enumerate(blocks):
        x = x_ref[0, rows, :] + mod_ref[0, 2:3, :] * mix[k]
        xo_ref[0, rows, :] = x
        y = x * lax.rsqrt(jnp.mean(x * x, axis=-1, keepdims=True) + NORM_EPS) * fgain_ref[0]
        h = y * (1.0 + mod_ref[0, 4:5, :]) + mod_ref[0, 3:4, :]
        h_ref[0, rows, :] = h
        hs.append(_split_bf16(h))
    routed = []
    for h_hi, h_lo in hs:
        both = _dot(h_hi, wr_ref[0])
        logits = (both[:, :ROUTE_LANES] + both[:, ROUTE_LANES:]
                  + _dot(h_lo, wr_ref[0, :, :ROUTE_LANES]) + br_ref[0])
        routed.append(_route(logits))
    e_a, e_b, w_a, w_b = (jnp.concatenate([r[c] for r in routed], axis=0) for c in range(4))

    lane = lax.broadcasted_iota(jnp.int32, (tm, ROUTE_LANES), 1).astype(F32)
    group0 = (((tile * tm) // seg) * N_EXPERTS).astype(F32)
    hot_a = jnp.where(lane == e_a + group0, 1.0, 0.0)
    hot_b = jnp.where(lane == e_b + group0, 1.0, 0.0)
    hot = hot_a + hot_b
    before = cnt_scr[...] + _dot(tri_ref[...], hot.astype(BF16))
    rank_a = jnp.sum(hot_a * before, axis=-1, keepdims=True)
    rank_b = jnp.sum(hot_b * before, axis=-1, keepdims=True)
    cnt_scr[...] = cnt_scr[...] + jnp.sum(hot, axis=0, keepdims=True)
    cnt_ref[...] = cnt_scr[...]
    cols = (e_a, e_b, w_a, w_b, rank_a, rank_b)
    out = jnp.zeros_like(lane)
    for c, v in enumerate(cols):
        out = jnp.where(lane == float(c), v, out)
    rt_ref[0] = out
    rtt_ref[...] = out.T[:SUBLANES, :]


def _out_proj(o_f, o_b, proj, o_at, x, modr, mod_row0, hgain, w_out_bf, fgain, w_route, b_route, layer,
              seg):
    b, l, _ = x.shape
    tm = min(OUTPROJ_TILE, l)
    tok = lambda w, col=0: pl.BlockSpec((1, tm, w), lambda i, j: (i, j, col))
    lay = lambda *s: pl.BlockSpec((1,) + s, lambda i, j: (layer,) + (0,) * len(s))
    tri =jnp.asarray(np.tril(np.ones((tm, tm), np.float32), -1), dtype=BF16)
    return pl.pallas_call(
        functools.partial(_outproj_body, seg=seg),
        grid=(b, l // tm),
        in_specs=[tok(HG_WIDTH), tok(HG_WIDTH), tok(HG_WIDTH, COL_HG), tok(ATTN_WIDTH), tok(D_MODEL),
                  pl.BlockSpec((1, N_MOD, D_MODEL), lambda i, j: (mod_row0 + i, 0, 0)),
                  lay(1, HG_DIM), lay(D_MODEL, D_MODEL), lay(1, D_MODEL),
                  lay(D_MODEL, 2 * ROUTE_LANES), lay(1, ROUTE_LANES),
                  pl.BlockSpec((tm, tm), lambda i, j: (0, 0))],
        out_specs=[tok(D_MODEL), tok(D_MODEL), tok(ROUTE_LANES),
                   pl.BlockSpec((SUBLANES, tm), lambda i, j: (0, i * (l // tm) + j)),
                   pl.BlockSpec((1, ROUTE_LANES), lambda i, j: (0, 0))],
        out_shape=[jax.ShapeDtypeStruct((b, l, D_MODEL), F32),
                   jax.ShapeDtypeStruct((b, l, D_MODEL), F32),
                   jax.ShapeDtypeStruct((b, l, ROUTE_LANES), F32),
                   jax.ShapeDtypeStruct((SUBLANES, b * l), F32),
                   jax.ShapeDtypeStruct((1, ROUTE_LANES), F32)],
        scratch_shapes=[pltpu.VMEM((1, ROUTE_LANES), F32)],
        compiler_params=_cparams("arbitrary", "arbitrary"),
        name="out_proj_router",
    )(o_f, o_b, proj, o_at, x, modr, hgain, w_out_bf, fgain, w_route, b_route, tri)


def _moe_body(te_ref, tv_ref, tu_ref, tok_ref, h_ref, *refs, steps_per_seg):
    nu = FFN_TILES_PER_STEP
    weights = [refs[3 * u:3 * u + 3] for u in range(nu)]
    o_ref, xs_scr = refs[3 * nu:]
    step = pl.program_id(0)
    tr = xs_scr.shape[2]
    cur = step % 2
    tile_of = lambda s, u: _ffn_tile_index(s, u, steps_per_seg)

    def ffn(u, xs):
        wg_ref, wu_ref, wd_ref = weights[u]
        gt = _dot(xs, wg_ref[0])
        up = _dot(xs, wu_ref[0])
        hid = (gt * _sigmoid(gt)) * up
        o_ref[0, u, 0] = _dot(hid.astype(BF16), wd_ref[0])

    any_valid = tv_ref[tile_of(step, 0)] != 0

    @pl.when(jnp.logical_and(any_valid, step % steps_per_seg == 0))
    def _():
        for u in range(nu):
            first = tu_ref[tile_of(step, u)]

            def row(r, carry):
                xs_scr[cur, u, pl.ds(r, 1), :] = h_ref[0, pl.ds(tok_ref[first + r], 1), :]
                return carry

            lax.fori_loop(0, tr, row, 0, unroll=8)

    @pl.when(any_valid)
    def _():
        nxt = jnp.minimum(step + 1, pl.num_programs(0) - 1)
        xs = [xs_scr[cur, u].astype(BF16) for u in range(nu)]
        for u in range(nu):
            first = tu_ref[tile_of(nxt, u)]
            for r in range(tr):
                xs_scr[1 - cur, u, r:r + 1, :] = h_ref[0, pl.ds(tok_ref[first + r], 1), :]
        for u in range(nu):
            ffn(u, xs[u])

    @pl.when(jnp.logical_not(any_valid))
    def _():
        o_ref[...] = jnp.zeros_like(o_ref)


def _moe_ffn(h_seg, sorted_tok, tile_expert, tile_valid, tile_first, wg, wu, wd, layer):
    nseg, seg, d = h_seg.shape
    n_tiles = tile_expert.shape[0]
    tr = EXPERT_TILE
    nu = FFN_TILES_PER_STEP
    steps_per_seg = n_tiles // nseg // nu
    wspec = lambda k, n, u: pl.BlockSpec(
        (1, k, n), lambda i, te, *_: (layer * N_EXPERTS + te[_ffn_tile_index(i, u, steps_per_seg)], 0, 0))
    wspecs = [wspec(D_MODEL, D_EXPERT, u) if j < 2 else wspec(D_EXPERT, D_MODEL, u)
              for u in range(nu) for j in range(3)]
    rows = pl.pallas_call(
        functools.partial(_moe_body, steps_per_seg=steps_per_seg),
        grid_spec=pltpu.PrefetchScalarGridSpec(
            num_scalar_prefetch=4,
            grid=(n_tiles // nu,),
            in_specs=[pl.BlockSpec((1, seg, d), lambda i, *_: (i // steps_per_seg, 0, 0),
                                   pipeline_mode=pl.Buffered(1))] + wspecs,
            out_specs=pl.BlockSpec((1, nu, 1, tr, d),
                                   lambda i, *_: (i // steps_per_seg, 0, i % steps_per_seg, 0, 0)),
            scratch_shapes=[pltpu.VMEM((2, nu, tr, d), F32)]),
        out_shape=jax.ShapeDtypeStruct((nseg, nu, steps_per_seg, tr, d), F32),
        compiler_params=_cparams("arbitrary"),
        name="routed_expert_ffn",
    )(tile_expert, tile_valid, tile_first, sorted_tok, h_seg, *([wg, wu, wd] * nu))
    return rows.reshape(n_tiles * tr, d)


def _ffn_tile_index(step, slot, steps_per_seg):
    seg = step // steps_per_seg
    return (seg * FFN_TILES_PER_STEP + slot) * steps_per_seg + step % steps_per_seg


def _combine_body(pa_ref, pb_ref, rows_ref, rt_ref, x_ref, mod_ref, fin_ref, o_ref, ab_scr, *,
                  final, seg_tiles):
    tm = o_ref.shape[1]
    tile = pl.program_id(0) * pl.num_programs(1) + pl.program_id(1)
    cur = tile % 2
    pos = (pa_ref, pb_ref)

    @pl.when(tile % seg_tiles == 0)
    def _():
        def gather(r, carry):
            for slot in range(2):
                ab_scr[cur, slot, pl.ds(r, 1), :] = rows_ref[0, pl.ds(pos[slot][tile * tm + r], 1), :]
            return carry

        lax.fori_loop(0, tm, gather, 0, unroll=8)

    rt = rt_ref[0]
    moe = rt[:, 2:3] * ab_scr[cur, 0] + rt[:, 3:4] * ab_scr[cur, 1]
    nxt = jnp.minimum(tile + 1, pl.num_programs(0) * pl.num_programs(1) - 1) * tm
    for r in range(tm):
        for slot in range(2):
            ab_scr[1 - cur, slot, r:r + 1, :] = rows_ref[0, pl.ds(pos[slot][nxt + r], 1), :]
    x = x_ref[0] + mod_ref[0, 5:6, :] * moe
    if final:
        x = x * lax.rsqrt(jnp.mean(x * x, axis=-1, keepdims=True) + NORM_EPS) * fin_ref[...]
    o_ref[0] = x


def _combine(pos_a, pos_b, rows_seg, route, x, modr, mod_row0, final_gain, final):
    b, l, _ = x.shape
    nseg, rseg, d = rows_seg.shape
    tm = min(COMBINE_TILE, l)
    tiles_l = l // tm
    seg_tiles = (b * tiles_l) // nseg
    tok = lambda w: pl.BlockSpec((1, tm, w), lambda i, j, *_: (i, j, 0))
    return pl.pallas_call(
        functools.partial(_combine_body, final=final, seg_tiles=seg_tiles),
        grid_spec=pltpu.PrefetchScalarGridSpec(
            num_scalar_prefetch=2,
            grid=(b, tiles_l),
            in_specs=[pl.BlockSpec((1, rseg, d), lambda i, j, *_: ((i * tiles_l + j) // seg_tiles, 0, 0),
                                   pipeline_mode=pl.Buffered(1)),
                      tok(ROUTE_LANES), tok(D_MODEL),
                      pl.BlockSpec((1, N_MOD, D_MODEL), lambda i, j, *_: (mod_row0 + i, 0, 0)),
                      pl.BlockSpec((1, D_MODEL), lambda i, j, *_: (0, 0))],
            out_specs=tok(D_MODEL),
            scratch_shapes=[pltpu.VMEM((2, 2, tm, d), F32)]),
        out_shape=jax.ShapeDtypeStruct((b, l, D_MODEL), F32),
        compiler_params=_cparams("arbitrary", "arbitrary"),
        name="moe_combine",
    )(pos_a, pos_b, rows_seg, route, x, modr, final_gain)


def _routing_tables(route_t, counts, nseg, seg):
    t = route_t.shape[1]
    tr = EXPERT_TILE
    rseg = 2 * seg + N_EXPERTS * tr
    ngrp = nseg * N_EXPERTS
    cnt = counts[0, :ngrp].astype(jnp.int32)
    pad2 = (((cnt + tr - 1) // tr) * tr).reshape(nseg, N_EXPERTS)
    end_local = jnp.cumsum(pad2, axis=1)
    start_local = (end_local - pad2).reshape(-1)
    start_unp = jnp.cumsum(cnt) - cnt
    ids = route_t[0:2].astype(jnp.int32)
    rank = route_t[4:6].astype(jnp.int32)
    key = (jnp.arange(t, dtype=jnp.int32) // seg)[None, :] * N_EXPERTS + ids
    hot = key[:, :, None] == jnp.arange(ngrp, dtype=jnp.int32)[None, None, :]
    lookup = lambda table: jnp.sum(jnp.where(hot, table[None, None, :], 0), axis=-1)
    pos = lookup(start_local) + rank
    order = lookup(start_unp) + rank
    tok = jnp.tile(jnp.arange(t, dtype=jnp.int32) % seg, 2)
    _, sorted_tok = lax.sort((order.reshape(-1), tok), num_keys=1)
    sorted_tok = jnp.concatenate([sorted_tok, jnp.zeros((tr,), jnp.int32)])
    tiles_per_seg = rseg // tr
    local_row = jnp.arange(tiles_per_seg, dtype=jnp.int32) * tr
    tile_grp = jnp.sum(local_row[None, :, None] >= end_local[:, None, :], axis=-1)
    tile_valid = (tile_grp < N_EXPERTS).astype(jnp.int32)
    tile_expert = jnp.minimum(tile_grp, N_EXPERTS - 1).astype(jnp.int32)
    grp = tile_expert + (jnp.arange(nseg, dtype=jnp.int32) * N_EXPERTS)[:, None]
    hot_t = grp[:, :, None] == jnp.arange(ngrp, dtype=jnp.int32)[None, None, :]
    skipped_pad = jnp.sum(jnp.where(hot_t, (start_local - start_unp)[None, None, :], 0), axis=-1)
    tile_first = (local_row[None, :] - skipped_pad) * tile_valid
    return (sorted_tok, pos[0], pos[1], tile_expert.reshape(-1), tile_valid.reshape(-1),
            tile_first.reshape(-1), rseg)


def _trunk(x, modr, mod_row0, nb, params):
    b, l, _ = x.shape
    t = b * l
    seg = min(MOE_SEGMENT, t)
    nseg = t // seg
    depth = params["w_in"].shape[0]
    for layer in range(depth):
        row0 = layer * nb + mod_row0
        proj, fx_min = _in_proj(x, modr, row0, params["norm_mix"], params["w_in"], layer)
        o_f, o_b = _hgrn(proj, params["lbs"], layer, fx_min)
        qt, k_rot, vt = _qk_prep(proj, params["q_gain2"], params["k_gain2"], layer)
        o_at = _attention(qt, k_rot, vt, params["score_bound"][layer])
        x_mid, h, route, route_t, counts = _out_proj(
            o_f, o_b, proj, o_at, x, modr, row0, params["hg_gain"], params["w_out"], params["norm_ffn"],
            params["w_route"], params["b_route"], layer, seg)
        sorted_tok, pos_a, pos_b, tile_e, tile_v, tile_f, rseg = _routing_tables(
            route_t, counts, nseg, seg)
        rows = _moe_ffn(h.reshape(nseg, seg, D_MODEL), sorted_tok, tile_e, tile_v, tile_f,
                        params["w_e_gate"], params["w_e_up"], params["w_e_down"], layer)
        x = _combine(pos_a, pos_b, rows.reshape(nseg, rseg, D_MODEL), route, x_mid, modr, row0,
                     params["final_norm"], final=(layer == depth - 1))
    return x


def kernel(x_prompt, x_sample, c_prompt, c_sample, w_in, w_out, hg_lb, hg_out_norm, q_norm, k_norm,
           norm_mix, norm_ffn, w_ada, b_ada, w_group, b_group, w_router, b_router, w_e_gate, w_e_up,
           w_e_down, final_norm):
    depth = w_in.shape[0]
    bp, bs = c_prompt.shape[0], c_sample.shape[0]
    nb = -(-(bp + bs) // SUBLANES) * SUBLANES
    c_all = jnp.zeros((nb, D_MODEL), F32).at[:bp].set(c_prompt).at[bp:bp + bs].set(c_sample)
    mod = _modulation(c_all, w_ada, b_ada)
    modr = mod.reshape(depth * nb, N_MOD, D_MODEL)
    pad = ROUTE_LANES - N_GROUPS - N_EXPERTS
    params = {
        "w_in": w_in.astype(BF16),
        "w_out": w_out.astype(BF16),
        "lbs": _lower_bounds(hg_lb).reshape(depth, 1, HG_WIDTH),
        "hg_gain": hg_out_norm.reshape(depth, 1, HG_DIM),
        "q_gain2": jnp.tile(q_norm, (1, LANES // HEAD_DIM)).reshape(depth, 1, LANES),
        "k_gain2": jnp.tile(k_norm, (1, LANES // HEAD_DIM)).reshape(depth, 1, LANES),
        "score_bound": (1.02 * LOG2E * HEAD_DIM ** 0.5 * jnp.max(jnp.abs(q_norm), axis=1)
                        * jnp.max(jnp.abs(k_norm), axis=1)).reshape(depth, 1, 1),
        "norm_mix": norm_mix.reshape(depth, 1, D_MODEL),
        "norm_ffn": norm_ffn.reshape(depth, 1, D_MODEL),
        "w_route": jnp.concatenate(_split_bf16(jnp.pad(jnp.concatenate([w_group, w_router], axis=-1),
                                                       ((0, 0), (0, 0), (0, pad)))), axis=-1),
        "b_route": jnp.pad(jnp.concatenate([b_group, b_router], axis=-1),
                           ((0, 0), (0, pad))).reshape(depth, 1, ROUTE_LANES),
        "w_e_gate": w_e_gate.astype(BF16).reshape(depth * N_EXPERTS, D_MODEL, D_EXPERT),
        "w_e_up": w_e_up.astype(BF16).reshape(depth * N_EXPERTS, D_MODEL, D_EXPERT),
        "w_e_down": w_e_down.astype(BF16).reshape(depth * N_EXPERTS, D_EXPERT, D_MODEL),
        "final_norm": final_norm.reshape(1, D_MODEL),
    }
    y_prompt = _trunk(x_prompt, modr, 0, nb, params)
    y_sample = _trunk(x_sample, modr, bp, nb, params)
    return (y_prompt, y_sample)
```

```python
import functools

import numpy as np
import jax
import jax.numpy as jnp
from jax import lax
from jax.experimental import pallas as pl
from jax.experimental.pallas import tpu as pltpu

F32 = jnp.float32
BF16 = jnp.bfloat16

D_MODEL = 1024
HG_WIDTH = 512
HG_HEADS = 4
HG_DIM = 128
LB_FLOOR = 1e-30
ATTN_WIDTH = 512
HEAD_DIM = 64
N_Q_HEADS = 8
N_KV_HEADS = 2
GQA_GROUP = 4
KV_WIDTH = 128
GRID_W = 64
ROPE_AXIS_DIM = 32
ROPE_THETA = 10000.0
N_GROUPS = 4
EXPERTS_PER_GROUP = 4
N_EXPERTS = 16
D_EXPERT = 512
N_MOD = 6
NORM_EPS = 1e-6
IN_PROJ_WIDTH = 5 * HG_WIDTH + ATTN_WIDTH + 2 * KV_WIDTH
COL_HQ, COL_HF_FWD, COL_HF_BWD, COL_HI, COL_HG, COL_AQ = 0, 1, 2, 3, 4, 5
COL_AK, COL_AV = 24, 25

LANES = 128
SUBLANES = 8
VMEM_LIMIT = 56 * 1024 * 1024
ROW_TILE = 512
OUTPROJ_TILE = 1024
HG_CHUNK = 128
KV_BLOCK = 512
Q_TILE = 256
EXPERT_TILE = 128
FFN_TILES_PER_STEP = 4
COMBINE_TILE = 256
MOE_SEGMENT = 4096
ROUTE_LANES = 128
LOG2E = 1.4426950408889634
SAFE_SCORE_BOUND = 40.0


def _cparams(*sem):
    return pltpu.CompilerParams(dimension_semantics=sem, vmem_limit_bytes=VMEM_LIMIT)


def _dot(a, b):
    return jnp.dot(a, b, preferred_element_type=F32)


def _dot_nt(a, b):
    return lax.dot_general(a, b, (((1,), (1,)), ((), ())), preferred_element_type=F32)


def _split_bf16(x):
    hi = x.astype(BF16)
    lo = (x - hi.astype(F32)).astype(BF16)
    return hi, lo


def _sigmoid(x):
    return 1.0 / (1.0 + jnp.exp(-x))


def _mod_body(c_ref, w_ref, b_ref, o_ref):
    c = c_ref[...]
    cs = c * _sigmoid(c)
    o_ref[0] = _dot(cs.astype(BF16), w_ref[0].astype(BF16)) + b_ref[0]


def _modulation(c_all, w_ada, b_ada):
    nb = c_all.shape[0]
    depth, _, n = w_ada.shape
    tn = 1024
    return pl.pallas_call(
        _mod_body,
        grid=(depth, n // tn),
        in_specs=[pl.BlockSpec((nb, D_MODEL), lambda l, j: (0, 0)),
                  pl.BlockSpec((1, D_MODEL, tn), lambda l, j: (l, 0, j)),
                  pl.BlockSpec((1, 1, tn), lambda l, j: (l, 0, j))],
        out_specs=pl.BlockSpec((1, nb, tn), lambda l, j: (l, 0, j)),
        out_shape=jax.ShapeDtypeStruct((depth, nb, n), F32),
        compiler_params=_cparams("arbitrary", "arbitrary"),
        name="adaln_modulation",
    )(c_all, w_ada, b_ada.reshape(depth, 1, n))


def _lb_body(p_ref, o_ref):
    depth = p_ref.shape[0]
    rows = [p_ref[l:l + 1, :] for l in range(depth)]
    m = functools.reduce(jnp.maximum, rows)
    es = [jnp.exp(r - m) for r in rows]
    tot = functools.reduce(lambda a, b: a + b, es)
    sm = [e / tot for e in es]
    acc = jnp.zeros_like(sm[0])
    for l in range(depth):
        acc = acc + sm[l]
        o_ref[l:l + 1, :] = acc - sm[0]


def _lower_bounds(hg_lb):
    return pl.pallas_call(
        _lb_body,
        out_shape=jax.ShapeDtypeStruct(hg_lb.shape, F32),
        name="hgrn_lower_bounds",
    )(hg_lb)


def _inproj_body(x_ref, mod_ref, gain_ref, w_ref, p_ref, fmin_ref):
    x = x_ref[0]
    ms = jnp.mean(x * x, axis=-1, keepdims=True)
    y = x * lax.rsqrt(ms + NORM_EPS) * gain_ref[0]
    h = y * (1.0 + mod_ref[0, 1:2, :]) + mod_ref[0, 0:1, :]
    proj = _dot(h.astype(BF16), w_ref[0]).astype(BF16)
    p_ref[0] = proj
    lo, hi = COL_HF_FWD * HG_WIDTH, (COL_HF_BWD + 1) * HG_WIDTH
    cur = jnp.min(proj[:, lo:hi].astype(F32), axis=0, keepdims=True)
    first = jnp.logical_and(pl.program_id(0) == 0, pl.program_id(1) == 0)

    @pl.when(first)
    def _():
        fmin_ref[...] = cur

    @pl.when(jnp.logical_not(first))
    def _():
        fmin_ref[...] = jnp.minimum(fmin_ref[...], cur)


def _in_proj(x, modr, mod_row0, gain, w_in_bf, layer):
    b, l, _ = x.shape
    tm = min(ROW_TILE, l)
    return pl.pallas_call(
        _inproj_body,
        grid=(b, l // tm),
        in_specs=[pl.BlockSpec((1, tm, D_MODEL), lambda i, j: (i, j, 0)),
                  pl.BlockSpec((1, N_MOD, D_MODEL), lambda i, j: (mod_row0 + i, 0, 0)),
                  pl.BlockSpec((1, 1, D_MODEL), lambda i, j: (layer, 0, 0)),
                  pl.BlockSpec((1, D_MODEL, IN_PROJ_WIDTH), lambda i, j: (layer, 0, 0))],
        out_specs=[pl.BlockSpec((1, tm, IN_PROJ_WIDTH), lambda i, j: (i, j, 0)),
                   pl.BlockSpec((1, 2 * HG_WIDTH), lambda i, j: (0, 0))],
        out_shape=[jax.ShapeDtypeStruct((b, l, IN_PROJ_WIDTH), BF16),
                   jax.ShapeDtypeStruct((1, 2 * HG_WIDTH), F32)],
        compiler_params=_cparams("arbitrary", "arbitrary"),
        name="in_proj",
    )(x, modr, gain, w_in_bf)


HG_LEVELS = (64, 32, 16, 8, 4, 2, 1)
HG_INBLOCK = 8
HG_INBLOCK_MAX_EXPONENT = 80.0
HG_HEAD_GROUPS = ((0, 1, 2, 3),)
MASK_DIAG = len(HG_LEVELS)
MASK_INBLOCK = len(HG_LEVELS) + 1


def _hg_constants():
    c = HG_CHUNK
    t = np.arange(c)[:, None]
    s = np.arange(c)[None, :]
    masks = np.zeros((2, len(HG_LEVELS) + 2, c, c), np.float32)
    for li, h in enumerate(HG_LEVELS):
        same = (t // (2 * h)) == (s // (2 * h))
        masks[0, li] = same & ((t % (2 * h)) >= h) & ((s % (2 * h)) < h)
        masks[1, li] = same & ((t % (2 * h)) < h) & ((s % (2 * h)) >= h)
    masks[:, MASK_DIAG] = (t == s)
    same = (t // HG_INBLOCK) == (s // HG_INBLOCK)
    masks[0, MASK_INBLOCK] = same & (s <= t)
    masks[1, MASK_INBLOCK] = same & (s >= t)
    cum = np.stack([(s <= t), (s >= t)]).astype(np.float32)
    return jnp.asarray(masks), jnp.asarray(cum, dtype=BF16)


def _row_bcast(a, rows, blk):
    parts = [jnp.broadcast_to(a[r:r + 1, :], (blk, a.shape[1])) for r in rows]
    return parts[0] if len(parts) == 1 else jnp.concatenate(parts, axis=0)


def _level_reference(cum, h, reverse, sub):
    c = cum.shape[0]
    if 2 * h >= SUBLANES:
        rows = [p0 + (h if reverse else h - 1) for p0 in range(0, c, 2 * h)]
        return _row_bcast(cum, rows, 2 * h)
    out = None
    for p0 in reversed(range(0, SUBLANES, 2 * h)):
        r = p0 + (h if reverse else h - 1)
        piece = _row_bcast(cum, [v0 + r for v0 in range(0, c, SUBLANES)], SUBLANES)
        out = piece if out is None else jnp.where(sub < p0 + 2 * h, piece, out)
    return out


def _hg_chunks(chains, lb_ref, cum_ref, masks_ref, st_ref, inblock):
    c = HG_CHUNK
    sub = lax.broadcasted_iota(jnp.int32, (c, HG_DIM), 0) % SUBLANES
    work = []
    for head, d, q_ref, f_ref, i_ref, o_ref in chains:
        sl = slice(head * HG_DIM, (head + 1) * HG_DIM)
        lb = lb_ref[0, :, sl]
        hq = q_ref[0, :, sl].astype(F32)
        sig = _sigmoid(f_ref[0, :, sl].astype(F32))
        one_m_lb = 1.0 - lb
        w = dict(head=head, d=d, sl=sl, o_ref=o_ref, hi=i_ref[0, :, sl], q=hq * _sigmoid(hq),
                 g=jnp.log(jnp.maximum(lb, LB_FLOOR) + one_m_lb * sig),
                 kk=one_m_lb * (1.0 - sig))
        work.append(w)
    for w in work:
        g_hi, g_lo = _split_bf16(w["g"])
        w["cum"] = _dot(cum_ref[w["d"]], g_hi) + _dot(cum_ref[w["d"]], g_lo)
        w["q_bf"] = w["q"].astype(BF16)
        w["kk_bf"] = w["kk"].astype(BF16)
        w["a"] = jnp.zeros((c, c), F32)

    for li, h in enumerate(HG_LEVELS):
        if inblock and 2 * h <= HG_INBLOCK:
            continue
        for w in work:
            ref = _level_reference(w["cum"], h, w["d"] == 1, sub)
            e = jnp.exp2((jnp.abs(w["cum"] - ref) * (-LOG2E)).astype(BF16))
            w["a"] = w["a"] + _dot_nt(w["q_bf"] * e, w["kk_bf"] * e) * masks_ref[w["d"], li]
    for w in work:
        q, kk, cum, d = w["q"], w["kk"], w["cum"], w["d"]
        if inblock:
            first = HG_INBLOCK - 1 if d == 1 else 0
            ref = _row_bcast(cum, [p0 + first for p0 in range(0, c, HG_INBLOCK)], HG_INBLOCK)
            z = (cum - ref) * LOG2E
            al = _dot_nt((q * jnp.exp2(z)).astype(BF16), (kk * jnp.exp2(-z)).astype(BF16))
            w["a"] = w["a"] + al * masks_ref[d, MASK_INBLOCK]
        else:
            w["a"] = w["a"] + jnp.sum(q * kk, axis=-1, keepdims=True) * masks_ref[d, MASK_DIAG]
    for w in work:
        q, kk, cum, d, head = w["q"], w["kk"], w["cum"], w["d"], w["head"]
        hi = w["hi"]
        st = st_ref[d, head]
        tot = cum[0:1, :] if d == 1 else cum[c - 1:c, :]
        o = _dot(w["a"].astype(BF16), hi)
        o = o + _dot_nt((q * jnp.exp(cum)).astype(BF16), st.astype(BF16))
        k_end = (kk * jnp.exp(tot - cum)).astype(BF16)
        st_ref[d, head] = jnp.exp(tot) * st + _dot(hi.astype(F32).T.astype(BF16), k_end)
        w["o_ref"][0, :, w["sl"]] = o.astype(BF16)


def _hgrn_body(qf_ref, ff_ref, if_ref, qb_ref, fb_ref, ib_ref, lb_ref, cum_ref, masks_ref,
               of_ref, ob_ref, st_ref, *, inblock):
    @pl.when(pl.program_id(1) == 0)
    def _():
        st_ref[...] = jnp.zeros_like(st_ref)

    for heads in HG_HEAD_GROUPS:
        chains = [(head, d) + refs for head in heads
                  for d, refs in enumerate(((qf_ref, ff_ref, if_ref, of_ref),
                                            (qb_ref, fb_ref, ib_ref, ob_ref)))]
        _hg_chunks(chains, lb_ref, cum_ref, masks_ref, st_ref, inblock)


def _hgrn(proj, lbs, layer, fx_min):
    b, l, _ = proj.shape
    c = HG_CHUNK
    nc = l // c
    masks, cum = _hg_constants()
    fwd = lambda col: pl.BlockSpec((1, c, HG_WIDTH), lambda i, j: (i, j, col))
    bwd = lambda col: pl.BlockSpec((1, c, HG_WIDTH), lambda i, j: (i, nc - 1 - j, col))
    out_sd = jax.ShapeDtypeStruct((b, l, HG_WIDTH), BF16)

    def call(inblock):
        return pl.pallas_call(
            functools.partial(_hgrn_body, inblock=inblock),
            grid=(b, nc),
            in_specs=[fwd(COL_HQ), fwd(COL_HF_FWD), fwd(COL_HI),
                      bwd(COL_HQ), bwd(COL_HF_BWD), bwd(COL_HI),
                      pl.BlockSpec((1, 1, HG_WIDTH), lambda i, j: (layer, 0, 0)),
                      pl.BlockSpec(cum.shape, lambda i, j: (0, 0, 0)),
                      pl.BlockSpec(masks.shape, lambda i, j: (0, 0, 0, 0))],
            out_specs=[pl.BlockSpec((1, c, HG_WIDTH), lambda i, j: (i, j, 0)),
                       pl.BlockSpec((1, c, HG_WIDTH), lambda i, j: (i, nc - 1 - j, 0))],
            out_shape=[out_sd, out_sd],
            scratch_shapes=[pltpu.VMEM((2, HG_HEADS, HG_DIM, HG_DIM), F32)],
            compiler_params=_cparams("arbitrary", "arbitrary"),
            name="hgrn2_recurrence",
        )(proj, proj, proj, proj, proj, proj, lbs, cum, masks)

    lb = jnp.tile(lbs[layer], (1, 2))
    worst = jnp.log(jnp.maximum(lb, LB_FLOOR) + (1.0 - lb) * _sigmoid(fx_min))
    safe = (HG_INBLOCK - 1) * jnp.max(-worst) <= HG_INBLOCK_MAX_EXPONENT
    return lax.cond(safe, lambda: call(True), lambda: call(False))


def _rope_tables(l):
    lane = np.arange(LANES)
    dd = lane % HEAD_DIM
    axis = dd // ROPE_AXIS_DIM
    first_half = (dd % ROPE_AXIS_DIM) < (ROPE_AXIS_DIM // 2)
    freq_idx = dd % (ROPE_AXIS_DIM // 2)
    inv_freq = ROPE_THETA ** (-jnp.arange(0, ROPE_AXIS_DIM, 2, dtype=F32) / ROPE_AXIS_DIM)
    t = jnp.arange(l)
    pos = jnp.where(jnp.asarray(axis)[None, :] == 0, (t // GRID_W)[:, None], (t % GRID_W)[:, None])
    ang = pos.astype(F32) * inv_freq[jnp.asarray(freq_idx)][None, :]
    sign = jnp.where(jnp.asarray(first_half), -1.0, 1.0)[None, :]
    return jnp.cos(ang), jnp.sin(ang) * sign


def _norm_rope(x, gain, cos_t, sin_t, bd, first_half):
    x2_hi, x2_lo = _split_bf16(x * x)
    ss = _dot(x2_hi, bd) + _dot(x2_lo, bd)
    xn = x * lax.rsqrt(ss * (1.0 / HEAD_DIM) + NORM_EPS) * gain
    half = ROPE_AXIS_DIM // 2
    partner = jnp.where(first_half, pltpu.roll(xn, LANES - half, 1), pltpu.roll(xn, half, 1))
    return xn * cos_t + partner * sin_t


def _qkprep_body(q_ref, k_ref, v_ref, qg_ref, kg_ref, cos_ref, sin_ref, bd_ref, qt_ref, ko_ref, vt_ref):
    tq = qt_ref.shape[4] // GQA_GROUP
    cos_t = cos_ref[...]
    sin_t = sin_ref[...]
    bd = bd_ref[...]
    lane = lax.broadcasted_iota(jnp.int32, cos_t.shape, 1)
    first_half = (lane % ROPE_AXIS_DIM) < (ROPE_AXIS_DIM // 2)
    zeros = jnp.zeros((HEAD_DIM, tq), BF16)
    for j in range(ATTN_WIDTH // LANES):
        sl = slice(j * LANES, (j + 1) * LANES)
        qr = _norm_rope(q_ref[0, :, sl].astype(F32), qg_ref[0], cos_t, sin_t, bd, first_half)
        qr = qr * (HEAD_DIM ** -0.5 * LOG2E)
        h = (2 * j) // GQA_GROUP
        for u in range(qt_ref.shape[2]):
            t = qr[u * tq:(u + 1) * tq, :].T.astype(BF16)
            for e in range(2):
                piece = t[e * HEAD_DIM:(e + 1) * HEAD_DIM, :]
                blockcol = jnp.concatenate([piece, zeros] if h == 0 else [zeros, piece], axis=0)
                g = (2 * j + e) % GQA_GROUP
                qt_ref[0, h, u, :, g * tq:(g + 1) * tq] = blockcol
    kr = _norm_rope(k_ref[0].astype(F32), kg_ref[0], cos_t, sin_t, bd, first_half)
    ko_ref[0] = kr.astype(BF16)
    vt_ref[0, 0] = v_ref[0].astype(F32).T.astype(BF16)


def _qk_prep(proj, q_gain2, k_gain2, layer):
    b, l, _ = proj.shape
    tk = min(KV_BLOCK, l)
    tq = min(Q_TILE, l)
    cos_t, sin_t = _rope_tables(l)
    blk = np.arange(LANES) // HEAD_DIM
    bd = jnp.asarray((blk[:, None] == blk[None, :]).astype(np.float32), dtype=BF16)
    return pl.pallas_call(
        _qkprep_body,
        grid=(b, l // tk),
        in_specs=[pl.BlockSpec((1, tk, ATTN_WIDTH), lambda i, j: (i, j, COL_AQ)),
                  pl.BlockSpec((1, tk, KV_WIDTH), lambda i, j: (i, j, COL_AK)),
                  pl.BlockSpec((1, tk, KV_WIDTH), lambda i, j: (i, j, COL_AV)),
                  pl.BlockSpec((1, 1, LANES), lambda i, j: (layer, 0, 0)),
                  pl.BlockSpec((1, 1, LANES), lambda i, j: (layer, 0, 0)),
                  pl.BlockSpec((tk, LANES), lambda i, j: (j, 0)),
                  pl.BlockSpec((tk, LANES), lambda i, j: (j, 0)),
                  pl.BlockSpec((LANES, LANES), lambda i, j: (0, 0))],
        out_specs=[pl.BlockSpec((1, N_KV_HEADS, tk // tq, KV_WIDTH, GQA_GROUP * tq),
                                lambda i, j: (i, 0, j, 0, 0)),
                   pl.BlockSpec((1, tk, KV_WIDTH), lambda i, j: (i, j, 0)),
                   pl.BlockSpec((1, 1, KV_WIDTH, tk), lambda i, j: (i, j, 0, 0))],
        out_shape=[jax.ShapeDtypeStruct((b, N_KV_HEADS, l // tq, KV_WIDTH, GQA_GROUP * tq), BF16),
                   jax.ShapeDtypeStruct((b, l, KV_WIDTH), BF16),
                   jax.ShapeDtypeStruct((b, l // tk, KV_WIDTH, tk), BF16)],
        compiler_params=_cparams("arbitrary", "arbitrary"),
        name="qk_norm_rope",
    )(proj, proj, proj, q_gain2, k_gain2, cos_t, sin_t, bd)


def _attn_finish(acc_scr, l_fin, o_ref):
    tq = o_ref.shape[1]
    outs = []
    for h in range(N_KV_HEADS):
        o = (acc_scr[h] / l_fin[h]).T
        oh = o[:, h * HEAD_DIM:(h + 1) * HEAD_DIM]
        outs += [oh[g * tq:(g + 1) * tq, :] for g in range(GQA_GROUP)]
    o_ref[0] = jnp.concatenate(outs, axis=1).astype(BF16)


def _attn_bounded_body(bound_ref, qt_ref, k_ref, vt_ref, o_ref, acc_scr):
    nblk = vt_ref.shape[1]
    tk = vt_ref.shape[3]
    cols = qt_ref.shape[4]
    shift = bound_ref[0, 0]
    acc_scr[...] = jnp.zeros_like(acc_scr)

    def step(i, sums):
        kb = k_ref[0, pl.ds(pl.multiple_of(i * tk, tk), tk), :]
        vt = vt_ref[0, i]
        scores = [_dot(kb, qt_ref[0, h, 0]) for h in range(N_KV_HEADS)]
        new = []
        for h in range(N_KV_HEADS):
            p = jnp.exp2(scores[h] - shift)
            new.append(sums[h] + jnp.sum(p, axis=0, keepdims=True))
            acc_scr[h] = acc_scr[h] + _dot(vt, p.astype(BF16))
        return tuple(new)

    init = tuple(jnp.zeros((1, cols), F32) for _ in range(N_KV_HEADS))
    _attn_finish(acc_scr, lax.fori_loop(0, nblk, step, init, unroll=8), o_ref)


def _attn_online_body(bound_ref, qt_ref, k_ref, vt_ref, o_ref, acc_scr):
    nblk = vt_ref.shape[1]
    tk = vt_ref.shape[3]
    cols = qt_ref.shape[4]
    acc_scr[...] = jnp.zeros_like(acc_scr)
    l_fin = []
    for h in range(N_KV_HEADS):
        qt = qt_ref[0, h, 0]

        def step(i, carry):
            m_prev, l_prev = carry
            kb = k_ref[0, pl.ds(pl.multiple_of(i * tk, tk), tk), :]
            s = _dot(kb, qt)
            m_new = jnp.maximum(m_prev, jnp.max(s, axis=0, keepdims=True))
            alpha = jnp.exp2(m_prev - m_new)
            p = jnp.exp2(s - m_new)
            l_new = alpha * l_prev + jnp.sum(p, axis=0, keepdims=True)
            acc_scr[h] = alpha * acc_scr[h] + _dot(vt_ref[0, i], p.astype(BF16))
            return m_new, l_new

        init = (jnp.full((1, cols), -jnp.inf, F32), jnp.zeros((1, cols), F32))
        l_fin.append(lax.fori_loop(0, nblk, step, init)[1])
    _attn_finish(acc_scr, l_fin, o_ref)


def _attention(qt, k_rot, vt, bound):
    b, l, _ = k_rot.shape
    tq = qt.shape[4] // GQA_GROUP
    nblk, tk = vt.shape[1], vt.shape[3]

    def call(body):
        return pl.pallas_call(
            body,
            grid=(b, l // tq),
            in_specs=[pl.BlockSpec(memory_space=pltpu.SMEM),
                      pl.BlockSpec((1, N_KV_HEADS, 1, KV_WIDTH, GQA_GROUP * tq), lambda i, j: (i, 0, j, 0, 0)),
                      pl.BlockSpec((1, l, KV_WIDTH), lambda i, j: (i, 0, 0)),
                      pl.BlockSpec((1, nblk, KV_WIDTH, tk), lambda i, j: (i, 0, 0, 0))],
            out_specs=pl.BlockSpec((1, tq, ATTN_WIDTH), lambda i, j: (i, j, 0)),
            out_shape=jax.ShapeDtypeStruct((b, l, ATTN_WIDTH), BF16),
            scratch_shapes=[pltpu.VMEM((N_KV_HEADS, KV_WIDTH, GQA_GROUP * tq), F32)],
            compiler_params=_cparams("arbitrary", "arbitrary"),
            name="gqa_attention",
        )(bound, qt, k_rot, vt)

    return lax.cond(bound[0, 0] <= SAFE_SCORE_BOUND,
                    lambda: call(_attn_bounded_body), lambda: call(_attn_online_body))


def _route(logits):
    lane = lax.broadcasted_iota(jnp.int32, logits.shape, 1).astype(F32)
    neg = jnp.float32(-jnp.inf)
    lanemin = lambda cond: jnp.min(jnp.where(cond, lane, float(ROUTE_LANES)), axis=-1, keepdims=True)
    gl = jnp.where(lane < N_GROUPS, logits, neg)
    gmax = jnp.max(gl, axis=-1, keepdims=True)
    g_idx = lanemin(gl == gmax)
    pg_top = 1.0 / jnp.sum(jnp.exp(gl - gmax), axis=-1, keepdims=True)
    first = N_GROUPS + g_idx * EXPERTS_PER_GROUP
    in_group = (lane >= first) & (lane < first + EXPERTS_PER_GROUP)
    el = jnp.where(in_group, logits, neg)
    a_max = jnp.max(el, axis=-1, keepdims=True)
    a_idx = lanemin(el == a_max)
    el2 = jnp.where(lane == a_idx, neg, el)
    b_max = jnp.max(el2, axis=-1, keepdims=True)
    b_idx = lanemin(el2 == b_max)
    r = jnp.exp(b_max - a_max)
    w_a = pg_top / (1.0 + r)
    w_b = pg_top * r / (1.0 + r)
    return a_idx - N_GROUPS, b_idx - N_GROUPS, w_a, w_b


def _outproj_body(of_ref, ob_ref, hg_ref, at_ref, x_ref, mod_ref, hgain_ref, wout_ref, fgain_ref,
                  wr_ref, br_ref, tri_ref, xo_ref, h_ref, rt_ref, rtt_ref, cnt_ref, cnt_scr, *, seg):
    tm = x_ref.shape[1]
    tile = pl.program_id(0) * pl.num_programs(1) + pl.program_id(1)

    @pl.when(tile == 0)
    def _():
        cnt_scr[...] = jnp.zeros_like(cnt_scr)

    parts = []
    for head in range(HG_HEADS):
        sl = slice(head * HG_DIM, (head + 1) * HG_DIM)
        o = of_ref[0, :, sl].astype(F32) + ob_ref[0, :, sl].astype(F32)
        o = o * lax.rsqrt(jnp.mean(o * o, axis=-1, keepdims=True) + NORM_EPS) * hgain_ref[0]
        gt = hg_ref[0, :, sl].astype(F32)
        parts.append((o * (gt * _sigmoid(gt))).astype(BF16))
    o_hg = jnp.concatenate(parts, axis=1)
    mix = _dot(o_hg, wout_ref[0, :HG_WIDTH, :]) + _dot(at_ref[0], wout_ref[0, HG_WIDTH:, :])
    x = x_ref[0] + mod_ref[0, 2:3, :] * mix
    xo_ref[0] = x
    y = x * lax.rsqrt(jnp.mean(x * x, axis=-1, keepdims=True) + NORM_EPS) * fgain_ref[0]
    h = y * (1.0 + mod_ref[0, 4:5, :]) + mod_ref[0, 3:4, :]
    h_ref[0] = h
    h_hi, h_lo = _split_bf16(h)
    both = _dot(h_hi, wr_ref[0])
    logits = (both[:, :ROUTE_LANES] + both[:, ROUTE_LANES:] + _dot(h_lo, wr_ref[0, :, :ROUTE_LANES])
              + br_ref[0])
    e_a, e_b, w_a, w_b = _route(logits)

    lane = lax.broadcasted_iota(jnp.int32, (tm, ROUTE_LANES), 1).astype(F32)
    group0 = (((tile * tm) // seg) * N_EXPERTS).astype(F32)
    hot_a = jnp.where(lane == e_a + group0, 1.0, 0.0)
    hot_b = jnp.where(lane == e_b + group0, 1.0, 0.0)
    hot = hot_a + hot_b
    before = cnt_scr[...] + _dot(tri_ref[...], hot.astype(BF16))
    rank_a = jnp.sum(hot_a * before, axis=-1, keepdims=True)
    rank_b = jnp.sum(hot_b * before, axis=-1, keepdims=True)
    cnt_scr[...] = cnt_scr[...] + jnp.sum(hot, axis=0, keepdims=True)
    cnt_ref[...] = cnt_scr[...]
    cols = (e_a, e_b, w_a, w_b, rank_a, rank_b)
    out = jnp.zeros_like(lane)
    for c, v in enumerate(cols):
        out = jnp.where(lane == float(c), v, out)
    rt_ref[0] = out
    rtt_ref[...] = out.T[:SUBLANES, :]


def _out_proj(o_f, o_b, proj, o_at, x, modr, mod_row0, hgain, w_out_bf, fgain, w_route, b_route, layer,
              seg):
    b, l, _ = x.shape
    tm = min(OUTPROJ_TILE, l)
    tok = lambda w, col=0: pl.BlockSpec((1, tm, w), lambda i, j: (i, j, col))
    lay = lambda *s: pl.BlockSpec((1,) + s, lambda i, j: (layer,) + (0,) * len(s))
    tri =jnp.asarray(np.tril(np.ones((tm, tm), np.float32), -1), dtype=BF16)
    return pl.pallas_call(
        functools.partial(_outproj_body, seg=seg),
        grid=(b, l // tm),
        in_specs=[tok(HG_WIDTH), tok(HG_WIDTH), tok(HG_WIDTH, COL_HG), tok(ATTN_WIDTH), tok(D_MODEL),
                  pl.BlockSpec((1, N_MOD, D_MODEL), lambda i, j: (mod_row0 + i, 0, 0)),
                  lay(1, HG_DIM), lay(D_MODEL, D_MODEL), lay(1, D_MODEL),
                  lay(D_MODEL, 2 * ROUTE_LANES), lay(1, ROUTE_LANES),
                  pl.BlockSpec((tm, tm), lambda i, j: (0, 0))],
        out_specs=[tok(D_MODEL), tok(D_MODEL), tok(ROUTE_LANES),
                   pl.BlockSpec((SUBLANES, tm), lambda i, j: (0, i * (l // tm) + j)),
                   pl.BlockSpec((1, ROUTE_LANES), lambda i, j: (0, 0))],
        out_shape=[jax.ShapeDtypeStruct((b, l, D_MODEL), F32),
                   jax.ShapeDtypeStruct((b, l, D_MODEL), F32),
                   jax.ShapeDtypeStruct((b, l, ROUTE_LANES), F32),
                   jax.ShapeDtypeStruct((SUBLANES, b * l), F32),
                   jax.ShapeDtypeStruct((1, ROUTE_LANES), F32)],
        scratch_shapes=[pltpu.VMEM((1, ROUTE_LANES), F32)],
        compiler_params=_cparams("arbitrary", "arbitrary"),
        name="out_proj_router",
    )(o_f, o_b, proj, o_at, x, modr, hgain, w_out_bf, fgain, w_route, b_route, tri)


def _moe_body(te_ref, tv_ref, tu_ref, tok_ref, h_ref, *refs, steps_per_seg):
    nu = FFN_TILES_PER_STEP
    weights = [refs[2 * u:2 * u + 2] for u in range(nu)]
    o_ref, xs_scr = refs[2 * nu:]
    step = pl.program_id(0)
    tr = xs_scr.shape[2]
    cur = step % 2
    tile_of = lambda s, u: _ffn_tile_index(s, u, steps_per_seg)

    def ffn(u, xs):
        wgu_ref, wd_ref = weights[u]
        gu = _dot(xs, wgu_ref[0])
        gt = gu[:, :D_EXPERT]
        hid = (gt * _sigmoid(gt)) * gu[:, D_EXPERT:]
        o_ref[0, u, 0] = _dot(hid.astype(BF16), wd_ref[0])

    any_valid = tv_ref[tile_of(step, 0)] != 0

    @pl.when(jnp.logical_and(any_valid, step % steps_per_seg == 0))
    def _():
        for u in range(nu):
            first = tu_ref[tile_of(step, u)]

            def row(r, carry):
                xs_scr[cur, u, pl.ds(r, 1), :] = h_ref[0, pl.ds(tok_ref[first + r], 1), :]
                return carry

            lax.fori_loop(0, tr, row, 0, unroll=8)

    @pl.when(any_valid)
    def _():
        nxt = jnp.minimum(step + 1, pl.num_programs(0) - 1)
        xs = [xs_scr[cur, u].astype(BF16) for u in range(nu)]
        for u in range(nu):
            first = tu_ref[tile_of(nxt, u)]
            for r in range(tr):
                xs_scr[1 - cur, u, r:r + 1, :] = h_ref[0, pl.ds(tok_ref[first + r], 1), :]
        for u in range(nu):
            ffn(u, xs[u])

    @pl.when(jnp.logical_not(any_valid))
    def _():
        o_ref[...] = jnp.zeros_like(o_ref)


def _moe_ffn(h_seg, sorted_tok, tile_expert, tile_valid, tile_first, wgu, wd, layer):
    nseg, seg, d = h_seg.shape
    n_tiles = tile_expert.shape[0]
    tr = EXPERT_TILE
    nu = FFN_TILES_PER_STEP
    steps_per_seg = n_tiles // nseg // nu
    wspec = lambda k, n, u: pl.BlockSpec(
        (1, k, n), lambda i, te, *_: (layer * N_EXPERTS + te[_ffn_tile_index(i, u, steps_per_seg)], 0, 0))
    wspecs = [spec for u in range(nu)
              for spec in (wspec(D_MODEL, 2 * D_EXPERT, u), wspec(D_EXPERT, D_MODEL, u))]
    rows = pl.pallas_call(
        functools.partial(_moe_body, steps_per_seg=steps_per_seg),
        grid_spec=pltpu.PrefetchScalarGridSpec(
            num_scalar_prefetch=4,
            grid=(n_tiles // nu,),
            in_specs=[pl.BlockSpec((1, seg, d), lambda i, *_: (i // steps_per_seg, 0, 0),
                                   pipeline_mode=pl.Buffered(1))] + wspecs,
            out_specs=pl.BlockSpec((1, nu, 1, tr, d),
                                   lambda i, *_: (i // steps_per_seg, 0, i % steps_per_seg, 0, 0)),
            scratch_shapes=[pltpu.VMEM((2, nu, tr, d), F32)]),
        out_shape=jax.ShapeDtypeStruct((nseg, nu, steps_per_seg, tr, d), F32),
        compiler_params=_cparams("arbitrary"),
        name="routed_expert_ffn",
    )(tile_expert, tile_valid, tile_first, sorted_tok, h_seg, *([wgu, wd] * nu))
    return rows.reshape(n_tiles * tr, d)


def _ffn_tile_index(step, slot, steps_per_seg):
    seg = step // steps_per_seg
    return (seg * FFN_TILES_PER_STEP + slot) * steps_per_seg + step % steps_per_seg


def _combine_body(pa_ref, pb_ref, rows_ref, rt_ref, x_ref, mod_ref, fin_ref, o_ref, ab_scr, *,
                  final, seg_tiles):
    tm = o_ref.shape[1]
    tile = pl.program_id(0) * pl.num_programs(1) + pl.program_id(1)
    cur = tile % 2
    pos = (pa_ref, pb_ref)

    @pl.when(tile % seg_tiles == 0)
    def _():
        def gather(r, carry):
            for slot in range(2):
                ab_scr[cur, slot, pl.ds(r, 1), :] = rows_ref[0, pl.ds(pos[slot][tile * tm + r], 1), :]
            return carry

        lax.fori_loop(0, tm, gather, 0, unroll=8)

    rt = rt_ref[0]
    moe = rt[:, 2:3] * ab_scr[cur, 0] + rt[:, 3:4] * ab_scr[cur, 1]
    nxt = jnp.minimum(tile + 1, pl.num_programs(0) * pl.num_programs(1) - 1) * tm
    for r in range(tm):
        for slot in range(2):
            ab_scr[1 - cur, slot, r:r + 1, :] = rows_ref[0, pl.ds(pos[slot][nxt + r], 1), :]
    x = x_ref[0] + mod_ref[0, 5:6, :] * moe
    if final:
        x = x * lax.rsqrt(jnp.mean(x * x, axis=-1, keepdims=True) + NORM_EPS) * fin_ref[...]
    o_ref[0] = x


def _combine(pos_a, pos_b, rows_seg, route, x, modr, mod_row0, final_gain, final):
    b, l, _ = x.shape
    nseg, rseg, d = rows_seg.shape
    tm = min(COMBINE_TILE, l)
    tiles_l = l // tm
    seg_tiles = (b * tiles_l) // nseg
    tok = lambda w: pl.BlockSpec((1, tm, w), lambda i, j, *_: (i, j, 0))
    return pl.pallas_call(
        functools.partial(_combine_body, final=final, seg_tiles=seg_tiles),
        grid_spec=pltpu.PrefetchScalarGridSpec(
            num_scalar_prefetch=2,
            grid=(b, tiles_l),
            in_specs=[pl.BlockSpec((1, rseg, d), lambda i, j, *_: ((i * tiles_l + j) // seg_tiles, 0, 0),
                                   pipeline_mode=pl.Buffered(1)),
                      tok(ROUTE_LANES), tok(D_MODEL),
                      pl.BlockSpec((1, N_MOD, D_MODEL), lambda i, j, *_: (mod_row0 + i, 0, 0)),
                      pl.BlockSpec((1, D_MODEL), lambda i, j, *_: (0, 0))],
            out_specs=tok(D_MODEL),
            scratch_shapes=[pltpu.VMEM((2, 2, tm, d), F32)]),
        out_shape=jax.ShapeDtypeStruct((b, l, D_MODEL), F32),
        compiler_params=_cparams("arbitrary", "arbitrary"),
        name="moe_combine",
    )(pos_a, pos_b, rows_seg, route, x, modr, final_gain)


def _routing_tables(route_t, counts, nseg, seg):
    t = route_t.shape[1]
    tr = EXPERT_TILE
    rseg = 2 * seg + N_EXPERTS * tr
    ngrp = nseg * N_EXPERTS
    cnt = counts[0, :ngrp].astype(jnp.int32)
    pad2 = (((cnt + tr - 1) // tr) * tr).reshape(nseg, N_EXPERTS)
    end_local = jnp.cumsum(pad2, axis=1)
    start_local = (end_local - pad2).reshape(-1)
    start_unp = jnp.cumsum(cnt) - cnt
    ids = route_t[0:2].astype(jnp.int32)
    rank = route_t[4:6].astype(jnp.int32)
    key = (jnp.arange(t, dtype=jnp.int32) // seg)[None, :] * N_EXPERTS + ids
    hot = key[:, :, None] == jnp.arange(ngrp, dtype=jnp.int32)[None, None, :]
    lookup = lambda table: jnp.sum(jnp.where(hot, table[None, None, :], 0), axis=-1)
    pos = lookup(start_local) + rank
    order = lookup(start_unp) + rank
    tok = jnp.tile(jnp.arange(t, dtype=jnp.int32) % seg, 2)
    _, sorted_tok = lax.sort((order.reshape(-1), tok), num_keys=1)
    sorted_tok = jnp.concatenate([sorted_tok, jnp.zeros((tr,), jnp.int32)])
    tiles_per_seg = rseg // tr
    local_row = jnp.arange(tiles_per_seg, dtype=jnp.int32) * tr
    tile_grp = jnp.sum(local_row[None, :, None] >= end_local[:, None, :], axis=-1)
    tile_valid = (tile_grp < N_EXPERTS).astype(jnp.int32)
    tile_expert = jnp.minimum(tile_grp, N_EXPERTS - 1).astype(jnp.int32)
    grp = tile_expert + (jnp.arange(nseg, dtype=jnp.int32) * N_EXPERTS)[:, None]
    hot_t = grp[:, :, None] == jnp.arange(ngrp, dtype=jnp.int32)[None, None, :]
    skipped_pad = jnp.sum(jnp.where(hot_t, (start_local - start_unp)[None, None, :], 0), axis=-1)
    tile_first = (local_row[None, :] - skipped_pad) * tile_valid
    return (sorted_tok, pos[0], pos[1], tile_expert.reshape(-1), tile_valid.reshape(-1),
            tile_first.reshape(-1), rseg)


def _trunk(x, modr, mod_row0, nb, params):
    b, l, _ = x.shape
    t = b * l
    seg = min(MOE_SEGMENT, t)
    nseg = t // seg
    depth = params["w_in"].shape[0]
    for layer in range(depth):
        row0 = layer * nb + mod_row0
        proj, fx_min = _in_proj(x, modr, row0, params["norm_mix"], params["w_in"], layer)
        o_f, o_b = _hgrn(proj, params["lbs"], layer, fx_min)
        qt, k_rot, vt = _qk_prep(proj, params["q_gain2"], params["k_gain2"], layer)
        o_at = _attention(qt, k_rot, vt, params["score_bound"][layer])
        x_mid, h, route, route_t, counts = _out_proj(
            o_f, o_b, proj, o_at, x, modr, row0, params["hg_gain"], params["w_out"], params["norm_ffn"],
            params["w_route"], params["b_route"], layer, seg)
        sorted_tok, pos_a, pos_b, tile_e, tile_v, tile_f, rseg = _routing_tables(
            route_t, counts, nseg, seg)
        rows = _moe_ffn(h.reshape(nseg, seg, D_MODEL), sorted_tok, tile_e, tile_v, tile_f,
                        params["w_e_gate_up"], params["w_e_down"], layer)
        x = _combine(pos_a, pos_b, rows.reshape(nseg, rseg, D_MODEL), route, x_mid, modr, row0,
                     params["final_norm"], final=(layer == depth - 1))
    return x


def kernel(x_prompt, x_sample, c_prompt, c_sample, w_in, w_out, hg_lb, hg_out_norm, q_norm, k_norm,
           norm_mix, norm_ffn, w_ada, b_ada, w_group, b_group, w_router, b_router, w_e_gate, w_e_up,
           w_e_down, final_norm):
    depth = w_in.shape[0]
    bp, bs = c_prompt.shape[0], c_sample.shape[0]
    nb = -(-(bp + bs) // SUBLANES) * SUBLANES
    c_all = jnp.zeros((nb, D_MODEL), F32).at[:bp].set(c_prompt).at[bp:bp + bs].set(c_sample)
    mod = _modulation(c_all, w_ada, b_ada)
    modr = mod.reshape(depth * nb, N_MOD, D_MODEL)
    pad = ROUTE_LANES - N_GROUPS - N_EXPERTS
    params = {
        "w_in": w_in.astype(BF16),
        "w_out": w_out.astype(BF16),
        "lbs": _lower_bounds(hg_lb).reshape(depth, 1, HG_WIDTH),
        "hg_gain": hg_out_norm.reshape(depth, 1, HG_DIM),
        "q_gain2": jnp.tile(q_norm, (1, LANES // HEAD_DIM)).reshape(depth, 1, LANES),
        "k_gain2": jnp.tile(k_norm, (1, LANES // HEAD_DIM)).reshape(depth, 1, LANES),
        "score_bound": (1.02 * LOG2E * HEAD_DIM ** 0.5 * jnp.max(jnp.abs(q_norm), axis=1)
                        * jnp.max(jnp.abs(k_norm), axis=1)).reshape(depth, 1, 1),
        "norm_mix": norm_mix.reshape(depth, 1, D_MODEL),
        "norm_ffn": norm_ffn.reshape(depth, 1, D_MODEL),
        "w_route": jnp.concatenate(_split_bf16(jnp.pad(jnp.concatenate([w_group, w_router], axis=-1),
                                                       ((0, 0), (0, 0), (0, pad)))), axis=-1),
        "b_route": jnp.pad(jnp.concatenate([b_group, b_router], axis=-1),
                           ((0, 0), (0, pad))).reshape(depth, 1, ROUTE_LANES),
        "w_e_gate_up": jnp.concatenate([w_e_gate.astype(BF16), w_e_up.astype(BF16)], axis=-1).reshape(
            depth * N_EXPERTS, D_MODEL, 2 * D_EXPERT),
        "w_e_down": w_e_down.astype(BF16).reshape(depth * N_EXPERTS, D_EXPERT, D_MODEL),
        "final_norm": final_norm.reshape(1, D_MODEL),
    }
    y_prompt = _trunk(x_prompt, modr, 0, nb, params)
    y_sample = _trunk(x_sample, modr, bp, nb, params)
    return (y_prompt, y_sample)
```

```python
import functools

import numpy as np
import jax
import jax.numpy as jnp
from jax import lax
from jax.experimental import pallas as pl
from jax.experimental.pallas import tpu as pltpu

F32 = jnp.float32
BF16 = jnp.bfloat16

D_MODEL = 1024
HG_WIDTH = 512
HG_HEADS = 4
HG_DIM = 128
LB_FLOOR = 1e-30
ATTN_WIDTH = 512
HEAD_DIM = 64
N_Q_HEADS = 8
N_KV_HEADS = 2
GQA_GROUP = 4
KV_WIDTH = 128
GRID_W = 64
ROPE_AXIS_DIM = 32
ROPE_THETA = 10000.0
N_GROUPS = 4
EXPERTS_PER_GROUP = 4
N_EXPERTS = 16
D_EXPERT = 512
N_MOD = 6
NORM_EPS = 1e-6
IN_PROJ_WIDTH = 5 * HG_WIDTH + ATTN_WIDTH + 2 * KV_WIDTH
COL_HQ, COL_HF_FWD, COL_HF_BWD, COL_HI, COL_HG, COL_AQ = 0, 1, 2, 3, 4, 5
COL_AK, COL_AV = 24, 25

LANES = 128
SUBLANES = 8
VMEM_LIMIT = 56 * 1024 * 1024
ROW_TILE = 512
OUTPROJ_TILE = 1024
HG_CHUNK = 128
KV_BLOCK = 1024
Q_TILE = 256
EXPERT_TILE = 128
FFN_TILES_PER_STEP = 4
COMBINE_TILE = 256
MOE_SEGMENT = 4096
ROUTE_LANES = 128
LOG2E = 1.4426950408889634
SAFE_SCORE_BOUND = 40.0


def _cparams(*sem):
    return pltpu.CompilerParams(dimension_semantics=sem, vmem_limit_bytes=VMEM_LIMIT)


def _dot(a, b):
    return jnp.dot(a, b, preferred_element_type=F32)


def _dot_nt(a, b):
    return lax.dot_general(a, b, (((1,), (1,)), ((), ())), preferred_element_type=F32)


def _split_bf16(x):
    hi = x.astype(BF16)
    lo = (x - hi.astype(F32)).astype(BF16)
    return hi, lo


def _sigmoid(x):
    return 1.0 / (1.0 + jnp.exp(-x))


def _mod_body(c_ref, w_ref, b_ref, o_ref):
    c = c_ref[...]
    cs = c * _sigmoid(c)
    o_ref[0] = _dot(cs.astype(BF16), w_ref[0].astype(BF16)) + b_ref[0]


def _modulation(c_all, w_ada, b_ada):
    nb = c_all.shape[0]
    depth, _, n = w_ada.shape
    tn = 1024
    return pl.pallas_call(
        _mod_body,
        grid=(depth, n // tn),
        in_specs=[pl.BlockSpec((nb, D_MODEL), lambda l, j: (0, 0)),
                  pl.BlockSpec((1, D_MODEL, tn), lambda l, j: (l, 0, j)),
                  pl.BlockSpec((1, 1, tn), lambda l, j: (l, 0, j))],
        out_specs=pl.BlockSpec((1, nb, tn), lambda l, j: (l, 0, j)),
        out_shape=jax.ShapeDtypeStruct((depth, nb, n), F32),
        compiler_params=_cparams("arbitrary", "arbitrary"),
        name="adaln_modulation",
    )(c_all, w_ada, b_ada.reshape(depth, 1, n))


def _lb_body(p_ref, o_ref):
    depth = p_ref.shape[0]
    rows = [p_ref[l:l + 1, :] for l in range(depth)]
    m = functools.reduce(jnp.maximum, rows)
    es = [jnp.exp(r - m) for r in rows]
    tot = functools.reduce(lambda a, b: a + b, es)
    sm = [e / tot for e in es]
    acc = jnp.zeros_like(sm[0])
    for l in range(depth):
        acc = acc + sm[l]
        o_ref[l:l + 1, :] = acc - sm[0]


def _lower_bounds(hg_lb):
    return pl.pallas_call(
        _lb_body,
        out_shape=jax.ShapeDtypeStruct(hg_lb.shape, F32),
        name="hgrn_lower_bounds",
    )(hg_lb)


def _inproj_body(x_ref, mod_ref, gain_ref, w_ref, p_ref, fmin_ref):
    x = x_ref[0]
    ms = jnp.mean(x * x, axis=-1, keepdims=True)
    y = x * lax.rsqrt(ms + NORM_EPS) * gain_ref[0]
    h = y * (1.0 + mod_ref[0, 1:2, :]) + mod_ref[0, 0:1, :]
    proj = _dot(h.astype(BF16), w_ref[0]).astype(BF16)
    p_ref[0] = proj
    lo, hi = COL_HF_FWD * HG_WIDTH, (COL_HF_BWD + 1) * HG_WIDTH
    cur = jnp.min(proj[:, lo:hi].astype(F32), axis=0, keepdims=True)
    first = jnp.logical_and(pl.program_id(0) == 0, pl.program_id(1) == 0)

    @pl.when(first)
    def _():
        fmin_ref[...] = cur

    @pl.when(jnp.logical_not(first))
    def _():
        fmin_ref[...] = jnp.minimum(fmin_ref[...], cur)


def _in_proj(x, modr, mod_row0, gain, w_in_bf, layer):
    b, l, _ = x.shape
    tm = min(ROW_TILE, l)
    return pl.pallas_call(
        _inproj_body,
        grid=(b, l // tm),
        in_specs=[pl.BlockSpec((1, tm, D_MODEL), lambda i, j: (i, j, 0)),
                  pl.BlockSpec((1, N_MOD, D_MODEL), lambda i, j: (mod_row0 + i, 0, 0)),
                  pl.BlockSpec((1, 1, D_MODEL), lambda i, j: (layer, 0, 0)),
                  pl.BlockSpec((1, D_MODEL, IN_PROJ_WIDTH), lambda i, j: (layer, 0, 0))],
        out_specs=[pl.BlockSpec((1, tm, IN_PROJ_WIDTH), lambda i, j: (i, j, 0)),
                   pl.BlockSpec((1, 2 * HG_WIDTH), lambda i, j: (0, 0))],
        out_shape=[jax.ShapeDtypeStruct((b, l, IN_PROJ_WIDTH), BF16),
                   jax.ShapeDtypeStruct((1, 2 * HG_WIDTH), F32)],
        compiler_params=_cparams("arbitrary", "arbitrary"),
        name="in_proj",
    )(x, modr, gain, w_in_bf)


HG_LEVELS = (64, 32, 16, 8, 4, 2, 1)
HG_INBLOCK = 8
HG_INBLOCK_MAX_EXPONENT = 80.0
HG_HEAD_GROUPS = ((0, 1, 2, 3),)
MASK_DIAG = len(HG_LEVELS)
MASK_INBLOCK = len(HG_LEVELS) + 1


def _hg_constants():
    c = HG_CHUNK
    t = np.arange(c)[:, None]
    s = np.arange(c)[None, :]
    masks = np.zeros((2, len(HG_LEVELS) + 2, c, c), np.float32)
    for li, h in enumerate(HG_LEVELS):
        same = (t // (2 * h)) == (s // (2 * h))
        masks[0, li] = same & ((t % (2 * h)) >= h) & ((s % (2 * h)) < h)
        masks[1, li] = same & ((t % (2 * h)) < h) & ((s % (2 * h)) >= h)
    masks[:, MASK_DIAG] = (t == s)
    same = (t // HG_INBLOCK) == (s // HG_INBLOCK)
    masks[0, MASK_INBLOCK] = same & (s <= t)
    masks[1, MASK_INBLOCK] = same & (s >= t)
    cum = np.stack([(s <= t), (s >= t)]).astype(np.float32)
    return jnp.asarray(masks), jnp.asarray(cum, dtype=BF16)


def _row_bcast(a, rows, blk):
    parts = [jnp.broadcast_to(a[r:r + 1, :], (blk, a.shape[1])) for r in rows]
    return parts[0] if len(parts) == 1 else jnp.concatenate(parts, axis=0)


def _level_reference(cum, h, reverse, sub):
    c = cum.shape[0]
    if 2 * h >= SUBLANES:
        rows = [p0 + (h if reverse else h - 1) for p0 in range(0, c, 2 * h)]
        return _row_bcast(cum, rows, 2 * h)
    out = None
    for p0 in reversed(range(0, SUBLANES, 2 * h)):
        r = p0 + (h if reverse else h - 1)
        piece = _row_bcast(cum, [v0 + r for v0 in range(0, c, SUBLANES)], SUBLANES)
        out = piece if out is None else jnp.where(sub < p0 + 2 * h, piece, out)
    return out


def _hg_chunks(chains, lb_ref, cum_ref, masks_ref, st_ref, inblock):
    c = HG_CHUNK
    sub = lax.broadcasted_iota(jnp.int32, (c, HG_DIM), 0) % SUBLANES
    work = []
    for head, d, q_ref, f_ref, i_ref, o_ref in chains:
        sl = slice(head * HG_DIM, (head + 1) * HG_DIM)
        lb = lb_ref[0, :, sl]
        hq = q_ref[0, :, sl].astype(F32)
        sig = _sigmoid(f_ref[0, :, sl].astype(F32))
        one_m_lb = 1.0 - lb
        w = dict(head=head, d=d, sl=sl, o_ref=o_ref, hi=i_ref[0, :, sl], q=hq * _sigmoid(hq),
                 g=jnp.log(jnp.maximum(lb, LB_FLOOR) + one_m_lb * sig),
                 kk=one_m_lb * (1.0 - sig))
        work.append(w)
    for w in work:
        g_hi, g_lo = _split_bf16(w["g"])
        w["cum"] = _dot(cum_ref[w["d"]], g_hi) + _dot(cum_ref[w["d"]], g_lo)
        w["q_bf"] = w["q"].astype(BF16)
        w["kk_bf"] = w["kk"].astype(BF16)
        w["a"] = jnp.zeros((c, c), F32)

    for li, h in enumerate(HG_LEVELS):
        if inblock and 2 * h <= HG_INBLOCK:
            continue
        for w in work:
            ref = _level_reference(w["cum"], h, w["d"] == 1, sub)
            e = jnp.exp2((jnp.abs(w["cum"] - ref) * (-LOG2E)).astype(BF16))
            w["a"] = w["a"] + _dot_nt(w["q_bf"] * e, w["kk_bf"] * e) * masks_ref[w["d"], li]
    for w in work:
        q, kk, cum, d = w["q"], w["kk"], w["cum"], w["d"]
        if inblock:
            first = HG_INBLOCK - 1 if d == 1 else 0
            ref = _row_bcast(cum, [p0 + first for p0 in range(0, c, HG_INBLOCK)], HG_INBLOCK)
            z = (cum - ref) * LOG2E
            al = _dot_nt((q * jnp.exp2(z)).astype(BF16), (kk * jnp.exp2(-z)).astype(BF16))
            w["a"] = w["a"] + al * masks_ref[d, MASK_INBLOCK]
        else:
            w["a"] = w["a"] + jnp.sum(q * kk, axis=-1, keepdims=True) * masks_ref[d, MASK_DIAG]
    for w in work:
        q, kk, cum, d, head = w["q"], w["kk"], w["cum"], w["d"], w["head"]
        hi = w["hi"]
        st = st_ref[d, head]
        tot = cum[0:1, :] if d == 1 else cum[c - 1:c, :]
        o = _dot(w["a"].astype(BF16), hi)
        o = o + _dot_nt((q * jnp.exp(cum)).astype(BF16), st.astype(BF16))
        k_end = (kk * jnp.exp(tot - cum)).astype(BF16)
        st_ref[d, head] = jnp.exp(tot) * st + _dot(hi.astype(F32).T.astype(BF16), k_end)
        w["o_ref"][0, :, w["sl"]] = o.astype(BF16)


def _hgrn_body(qf_ref, ff_ref, if_ref, qb_ref, fb_ref, ib_ref, lb_ref, cum_ref, masks_ref,
               of_ref, ob_ref, st_ref, *, inblock):
    @pl.when(pl.program_id(1) == 0)
    def _():
        st_ref[...] = jnp.zeros_like(st_ref)

    for heads in HG_HEAD_GROUPS:
        chains = [(head, d) + refs for head in heads
                  for d, refs in enumerate(((qf_ref, ff_ref, if_ref, of_ref),
                                            (qb_ref, fb_ref, ib_ref, ob_ref)))]
        _hg_chunks(chains, lb_ref, cum_ref, masks_ref, st_ref, inblock)


def _hgrn(proj, lbs, layer, fx_min):
    b, l, _ = proj.shape
    c = HG_CHUNK
    nc = l // c
    masks, cum = _hg_constants()
    fwd = lambda col: pl.BlockSpec((1, c, HG_WIDTH), lambda i, j: (i, j, col))
    bwd = lambda col: pl.BlockSpec((1, c, HG_WIDTH), lambda i, j: (i, nc - 1 - j, col))
    out_sd = jax.ShapeDtypeStruct((b, l, HG_WIDTH), BF16)

    def call(inblock):
        return pl.pallas_call(
            functools.partial(_hgrn_body, inblock=inblock),
            grid=(b, nc),
            in_specs=[fwd(COL_HQ), fwd(COL_HF_FWD), fwd(COL_HI),
                      bwd(COL_HQ), bwd(COL_HF_BWD), bwd(COL_HI),
                      pl.BlockSpec((1, 1, HG_WIDTH), lambda i, j: (layer, 0, 0)),
                      pl.BlockSpec(cum.shape, lambda i, j: (0, 0, 0)),
                      pl.BlockSpec(masks.shape, lambda i, j: (0, 0, 0, 0))],
            out_specs=[pl.BlockSpec((1, c, HG_WIDTH), lambda i, j: (i, j, 0)),
                       pl.BlockSpec((1, c, HG_WIDTH), lambda i, j: (i, nc - 1 - j, 0))],
            out_shape=[out_sd, out_sd],
            scratch_shapes=[pltpu.VMEM((2, HG_HEADS, HG_DIM, HG_DIM), F32)],
            compiler_params=_cparams("arbitrary", "arbitrary"),
            name="hgrn2_recurrence",
        )(proj, proj, proj, proj, proj, proj, lbs, cum, masks)

    lb = jnp.tile(lbs[layer], (1, 2))
    worst = jnp.log(jnp.maximum(lb, LB_FLOOR) + (1.0 - lb) * _sigmoid(fx_min))
    safe = (HG_INBLOCK - 1) * jnp.max(-worst) <= HG_INBLOCK_MAX_EXPONENT
    return lax.cond(safe, lambda: call(True), lambda: call(False))


def _rope_tables(l):
    lane = np.arange(LANES)
    dd = lane % HEAD_DIM
    axis = dd // ROPE_AXIS_DIM
    first_half = (dd % ROPE_AXIS_DIM) < (ROPE_AXIS_DIM // 2)
    freq_idx = dd % (ROPE_AXIS_DIM // 2)
    inv_freq = ROPE_THETA ** (-jnp.arange(0, ROPE_AXIS_DIM, 2, dtype=F32) / ROPE_AXIS_DIM)
    t = jnp.arange(l)
    pos = jnp.where(jnp.asarray(axis)[None, :] == 0, (t // GRID_W)[:, None], (t % GRID_W)[:, None])
    ang = pos.astype(F32) * inv_freq[jnp.asarray(freq_idx)][None, :]
    sign = jnp.where(jnp.asarray(first_half), -1.0, 1.0)[None, :]
    return jnp.cos(ang), jnp.sin(ang) * sign


def _norm_rope(x, gain, cos_t, sin_t, bd, first_half):
    x2_hi, x2_lo = _split_bf16(x * x)
    ss = _dot(x2_hi, bd) + _dot(x2_lo, bd)
    xn = x * lax.rsqrt(ss * (1.0 / HEAD_DIM) + NORM_EPS) * gain
    half = ROPE_AXIS_DIM // 2
    partner = jnp.where(first_half, pltpu.roll(xn, LANES - half, 1), pltpu.roll(xn, half, 1))
    return xn * cos_t + partner * sin_t


def _qkprep_body(q_ref, k_ref, v_ref, qg_ref, kg_ref, cos_ref, sin_ref, bd_ref, qt_ref, ko_ref, vt_ref):
    tq = qt_ref.shape[4] // GQA_GROUP
    cos_t = cos_ref[...]
    sin_t = sin_ref[...]
    bd = bd_ref[...]
    lane = lax.broadcasted_iota(jnp.int32, cos_t.shape, 1)
    first_half = (lane % ROPE_AXIS_DIM) < (ROPE_AXIS_DIM // 2)
    zeros = jnp.zeros((HEAD_DIM, tq), BF16)
    for j in range(ATTN_WIDTH // LANES):
        sl = slice(j * LANES, (j + 1) * LANES)
        qr = _norm_rope(q_ref[0, :, sl].astype(F32), qg_ref[0], cos_t, sin_t, bd, first_half)
        qr = qr * (HEAD_DIM ** -0.5 * LOG2E)
        h = (2 * j) // GQA_GROUP
        for u in range(qt_ref.shape[2]):
            t = qr[u * tq:(u + 1) * tq, :].T.astype(BF16)
            for e in range(2):
                piece = t[e * HEAD_DIM:(e + 1) * HEAD_DIM, :]
                blockcol = jnp.concatenate([piece, zeros] if h == 0 else [zeros, piece], axis=0)
                g = (2 * j + e) % GQA_GROUP
                qt_ref[0, h, u, :, g * tq:(g + 1) * tq] = blockcol
    kr = _norm_rope(k_ref[0].astype(F32), kg_ref[0], cos_t, sin_t, bd, first_half)
    ko_ref[0] = kr.astype(BF16)
    vt_ref[0, 0] = v_ref[0].astype(F32).T.astype(BF16)


def _qk_prep(proj, q_gain2, k_gain2, layer):
    b, l, _ = proj.shape
    tk = min(KV_BLOCK, l)
    tq = min(Q_TILE, l)
    cos_t, sin_t = _rope_tables(l)
    blk = np.arange(LANES) // HEAD_DIM
    bd = jnp.asarray((blk[:, None] == blk[None, :]).astype(np.float32), dtype=BF16)
    return pl.pallas_call(
        _qkprep_body,
        grid=(b, l // tk),
        in_specs=[pl.BlockSpec((1, tk, ATTN_WIDTH), lambda i, j: (i, j, COL_AQ)),
                  pl.BlockSpec((1, tk, KV_WIDTH), lambda i, j: (i, j, COL_AK)),
                  pl.BlockSpec((1, tk, KV_WIDTH), lambda i, j: (i, j, COL_AV)),
                  pl.BlockSpec((1, 1, LANES), lambda i, j: (layer, 0, 0)),
                  pl.BlockSpec((1, 1, LANES), lambda i, j: (layer, 0, 0)),
                  pl.BlockSpec((tk, LANES), lambda i, j: (j, 0)),
                  pl.BlockSpec((tk, LANES), lambda i, j: (j, 0)),
                  pl.BlockSpec((LANES, LANES), lambda i, j: (0, 0))],
        out_specs=[pl.BlockSpec((1, N_KV_HEADS, tk // tq, KV_WIDTH, GQA_GROUP * tq),
                                lambda i, j: (i, 0, j, 0, 0)),
                   pl.BlockSpec((1, tk, KV_WIDTH), lambda i, j: (i, j, 0)),
                   pl.BlockSpec((1, 1, KV_WIDTH, tk), lambda i, j: (i, j, 0, 0))],
        out_shape=[jax.ShapeDtypeStruct((b, N_KV_HEADS, l // tq, KV_WIDTH, GQA_GROUP * tq), BF16),
                   jax.ShapeDtypeStruct((b, l, KV_WIDTH), BF16),
                   jax.ShapeDtypeStruct((b, l // tk, KV_WIDTH, tk), BF16)],
        compiler_params=_cparams("arbitrary", "arbitrary"),
        name="qk_norm_rope",
    )(proj, proj, proj, q_gain2, k_gain2, cos_t, sin_t, bd)


def _attn_finish(acc_scr, l_fin, o_ref):
    tq = o_ref.shape[1]
    outs = []
    for h in range(N_KV_HEADS):
        o = (acc_scr[h] / l_fin[h]).T
        oh = o[:, h * HEAD_DIM:(h + 1) * HEAD_DIM]
        outs += [oh[g * tq:(g + 1) * tq, :] for g in range(GQA_GROUP)]
    o_ref[0] = jnp.concatenate(outs, axis=1).astype(BF16)


def _attn_bounded_body(bound_ref, qt_ref, k_ref, vt_ref, o_ref, acc_scr):
    nblk = vt_ref.shape[1]
    tk = vt_ref.shape[3]
    cols = qt_ref.shape[4]
    shift = bound_ref[0, 0]
    acc_scr[...] = jnp.zeros_like(acc_scr)

    def step(i, sums):
        kb = k_ref[0, pl.ds(pl.multiple_of(i * tk, tk), tk), :]
        vt = vt_ref[0, i]
        scores = [_dot(kb, qt_ref[0, h, 0]) for h in range(N_KV_HEADS)]
        new = []
        for h in range(N_KV_HEADS):
            p = jnp.exp2(scores[h] - shift)
            new.append(sums[h] + jnp.sum(p, axis=0, keepdims=True))
            acc_scr[h] = acc_scr[h] + _dot(vt, p.astype(BF16))
        return tuple(new)

    init = tuple(jnp.zeros((1, cols), F32) for _ in range(N_KV_HEADS))
    _attn_finish(acc_scr, lax.fori_loop(0, nblk, step, init, unroll=8), o_ref)


def _attn_online_body(bound_ref, qt_ref, k_ref, vt_ref, o_ref, acc_scr):
    nblk = vt_ref.shape[1]
    tk = vt_ref.shape[3]
    cols = qt_ref.shape[4]
    acc_scr[...] = jnp.zeros_like(acc_scr)
    l_fin = []
    for h in range(N_KV_HEADS):
        qt = qt_ref[0, h, 0]

        def step(i, carry):
            m_prev, l_prev = carry
            kb = k_ref[0, pl.ds(pl.multiple_of(i * tk, tk), tk), :]
            s = _dot(kb, qt)
            m_new = jnp.maximum(m_prev, jnp.max(s, axis=0, keepdims=True))
            alpha = jnp.exp2(m_prev - m_new)
            p = jnp.exp2(s - m_new)
            l_new = alpha * l_prev + jnp.sum(p, axis=0, keepdims=True)
            acc_scr[h] = alpha * acc_scr[h] + _dot(vt_ref[0, i], p.astype(BF16))
            return m_new, l_new

        init = (jnp.full((1, cols), -jnp.inf, F32), jnp.zeros((1, cols), F32))
        l_fin.append(lax.fori_loop(0, nblk, step, init)[1])
    _attn_finish(acc_scr, l_fin, o_ref)


def _attention(qt, k_rot, vt, bound):
    b, l, _ = k_rot.shape
    tq = qt.shape[4] // GQA_GROUP
    nblk, tk = vt.shape[1], vt.shape[3]

    def call(body):
        return pl.pallas_call(
            body,
            grid=(b, l // tq),
            in_specs=[pl.BlockSpec(memory_space=pltpu.SMEM),
                      pl.BlockSpec((1, N_KV_HEADS, 1, KV_WIDTH, GQA_GROUP * tq), lambda i, j: (i, 0, j, 0, 0)),
                      pl.BlockSpec((1, l, KV_WIDTH), lambda i, j: (i, 0, 0)),
                      pl.BlockSpec((1, nblk, KV_WIDTH, tk), lambda i, j: (i, 0, 0, 0))],
            out_specs=pl.BlockSpec((1, tq, ATTN_WIDTH), lambda i, j: (i, j, 0)),
            out_shape=jax.ShapeDtypeStruct((b, l, ATTN_WIDTH), BF16),
            scratch_shapes=[pltpu.VMEM((N_KV_HEADS, KV_WIDTH, GQA_GROUP * tq), F32)],
            compiler_params=_cparams("arbitrary", "arbitrary"),
            name="gqa_attention",
        )(bound, qt, k_rot, vt)

    return lax.cond(bound[0, 0] <= SAFE_SCORE_BOUND,
                    lambda: call(_attn_bounded_body), lambda: call(_attn_online_body))


def _route(logits):
    lane = lax.broadcasted_iota(jnp.int32, logits.shape, 1).astype(F32)
    neg = jnp.float32(-jnp.inf)
    lanemin = lambda cond: jnp.min(jnp.where(cond, lane, float(ROUTE_LANES)), axis=-1, keepdims=True)
    gl = jnp.where(lane < N_GROUPS, logits, neg)
    gmax = jnp.max(gl, axis=-1, keepdims=True)
    g_idx = lanemin(gl == gmax)
    pg_top = 1.0 / jnp.sum(jnp.exp(gl - gmax), axis=-1, keepdims=True)
    first = N_GROUPS + g_idx * EXPERTS_PER_GROUP
    in_group = (lane >= first) & (lane < first + EXPERTS_PER_GROUP)
    el = jnp.where(in_group, logits, neg)
    a_max = jnp.max(el, axis=-1, keepdims=True)
    a_idx = lanemin(el == a_max)
    el2 = jnp.where(lane == a_idx, neg, el)
    b_max = jnp.max(el2, axis=-1, keepdims=True)
    b_idx = lanemin(el2 == b_max)
    r = jnp.exp(b_max - a_max)
    w_a = pg_top / (1.0 + r)
    w_b = pg_top * r / (1.0 + r)
    return a_idx - N_GROUPS, b_idx - N_GROUPS, w_a, w_b


def _outproj_body(of_ref, ob_ref, hg_ref, at_ref, x_ref, mod_ref, hgain_ref, wout_ref, fgain_ref,
                  wr_ref, br_ref, tri_ref, xo_ref, h_ref, rt_ref, rtt_ref, cnt_ref, cnt_scr, *, seg):
    tm = x_ref.shape[1]
    tile = pl.program_id(0) * pl.num_programs(1) + pl.program_id(1)

    @pl.when(tile == 0)
    def _():
        cnt_scr[...] = jnp.zeros_like(cnt_scr)

    parts = []
    for head in range(HG_HEADS):
        sl = slice(head * HG_DIM, (head + 1) * HG_DIM)
        o = of_ref[0, :, sl].astype(F32) + ob_ref[0, :, sl].astype(F32)
        o = o * lax.rsqrt(jnp.mean(o * o, axis=-1, keepdims=True) + NORM_EPS) * hgain_ref[0]
        gt = hg_ref[0, :, sl].astype(F32)
        parts.append((o * (gt * _sigmoid(gt))).astype(BF16))
    o_hg = jnp.concatenate(parts, axis=1)
    mix = _dot(o_hg, wout_ref[0, :HG_WIDTH, :]) + _dot(at_ref[0], wout_ref[0, HG_WIDTH:, :])
    x = x_ref[0] + mod_ref[0, 2:3, :] * mix
    xo_ref[0] = x
    y = x * lax.rsqrt(jnp.mean(x * x, axis=-1, keepdims=True) + NORM_EPS) * fgain_ref[0]
    h = y * (1.0 + mod_ref[0, 4:5, :]) + mod_ref[0, 3:4, :]
    h_ref[0] = h
    h_hi, h_lo = _split_bf16(h)
    both = _dot(h_hi, wr_ref[0])
    logits = (both[:, :ROUTE_LANES] + both[:, ROUTE_LANES:] + _dot(h_lo, wr_ref[0, :, :ROUTE_LANES])
              + br_ref[0])
    e_a, e_b, w_a, w_b = _route(logits)

    lane = lax.broadcasted_iota(jnp.int32, (tm, ROUTE_LANES), 1).astype(F32)
    group0 = (((tile * tm) // seg) * N_EXPERTS).astype(F32)
    hot_a = jnp.where(lane == e_a + group0, 1.0, 0.0)
    hot_b = jnp.where(lane == e_b + group0, 1.0, 0.0)
    hot = hot_a + hot_b
    before = cnt_scr[...] + _dot(tri_ref[...], hot.astype(BF16))
    rank_a = jnp.sum(hot_a * before, axis=-1, keepdims=True)
    rank_b = jnp.sum(hot_b * before, axis=-1, keepdims=True)
    cnt_scr[...] = cnt_scr[...] + jnp.sum(hot, axis=0, keepdims=True)
    cnt_ref[...] = cnt_scr[...]
    cols = (e_a, e_b, w_a, w_b, rank_a, rank_b)
    out = jnp.zeros_like(lane)
    for c, v in enumerate(cols):
        out = jnp.where(lane == float(c), v, out)
    rt_ref[0] = out
    rtt_ref[...] = out.T[:SUBLANES, :]


def _out_proj(o_f, o_b, proj, o_at, x, modr, mod_row0, hgain, w_out_bf, fgain, w_route, b_route, layer,
              seg):
    b, l, _ = x.shape
    tm = min(OUTPROJ_TILE, l)
    tok = lambda w, col=0: pl.BlockSpec((1, tm, w), lambda i, j: (i, j, col))
    lay = lambda *s: pl.BlockSpec((1,) + s, lambda i, j: (layer,) + (0,) * len(s))
    tri =jnp.asarray(np.tril(np.ones((tm, tm), np.float32), -1), dtype=BF16)
    return pl.pallas_call(
        functools.partial(_outproj_body, seg=seg),
        grid=(b, l // tm),
        in_specs=[tok(HG_WIDTH), tok(HG_WIDTH), tok(HG_WIDTH, COL_HG), tok(ATTN_WIDTH), tok(D_MODEL),
                  pl.BlockSpec((1, N_MOD, D_MODEL), lambda i, j: (mod_row0 + i, 0, 0)),
                  lay(1, HG_DIM), lay(D_MODEL, D_MODEL), lay(1, D_MODEL),
                  lay(D_MODEL, 2 * ROUTE_LANES), lay(1, ROUTE_LANES),
                  pl.BlockSpec((tm, tm), lambda i, j: (0, 0))],
        out_specs=[tok(D_MODEL), tok(D_MODEL), tok(ROUTE_LANES),
                   pl.BlockSpec((SUBLANES, tm), lambda i, j: (0, i * (l // tm) + j)),
                   pl.BlockSpec((1, ROUTE_LANES), lambda i, j: (0, 0))],
        out_shape=[jax.ShapeDtypeStruct((b, l, D_MODEL), F32),
                   jax.ShapeDtypeStruct((b, l, D_MODEL), F32),
                   jax.ShapeDtypeStruct((b, l, ROUTE_LANES), F32),
                   jax.ShapeDtypeStruct((SUBLANES, b * l), F32),
                   jax.ShapeDtypeStruct((1, ROUTE_LANES), F32)],
        scratch_shapes=[pltpu.VMEM((1, ROUTE_LANES), F32)],
        compiler_params=_cparams("arbitrary", "arbitrary"),
        name="out_proj_router",
    )(o_f, o_b, proj, o_at, x, modr, hgain, w_out_bf, fgain, w_route, b_route, tri)


def _moe_body(te_ref, tv_ref, tu_ref, tok_ref, h_ref, *refs, steps_per_seg):
    nu = FFN_TILES_PER_STEP
    weights = [refs[2 * u:2 * u + 2] for u in range(nu)]
    o_ref, xs_scr = refs[2 * nu:]
    step = pl.program_id(0)
    tr = xs_scr.shape[2]
    cur = step % 2
    tile_of = lambda s, u: _ffn_tile_index(s, u, steps_per_seg)

    def ffn(u, xs):
        wgu_ref, wd_ref = weights[u]
        gu = _dot(xs, wgu_ref[0])
        gt = gu[:, :D_EXPERT]
        hid = (gt * _sigmoid(gt)) * gu[:, D_EXPERT:]
        o_ref[0, u, 0] = _dot(hid.astype(BF16), wd_ref[0])

    any_valid = tv_ref[tile_of(step, 0)] != 0

    @pl.when(jnp.logical_and(any_valid, step % steps_per_seg == 0))
    def _():
        for u in range(nu):
            first = tu_ref[tile_of(step, u)]

            def row(r, carry):
                xs_scr[cur, u, pl.ds(r, 1), :] = h_ref[0, pl.ds(tok_ref[first + r], 1), :]
                return carry

            lax.fori_loop(0, tr, row, 0, unroll=8)

    @pl.when(any_valid)
    def _():
        nxt = jnp.minimum(step + 1, pl.num_programs(0) - 1)
        xs = [xs_scr[cur, u].astype(BF16) for u in range(nu)]
        for u in range(nu):
            first = tu_ref[tile_of(nxt, u)]
            for r in range(tr):
                xs_scr[1 - cur, u, r:r + 1, :] = h_ref[0, pl.ds(tok_ref[first + r], 1), :]
        for u in range(nu):
            ffn(u, xs[u])

    @pl.when(jnp.logical_not(any_valid))
    def _():
        o_ref[...] = jnp.zeros_like(o_ref)


def _moe_ffn(h_seg, sorted_tok, tile_expert, tile_valid, tile_first, wgu, wd, layer):
    nseg, seg, d = h_seg.shape
    n_tiles = tile_expert.shape[0]
    tr = EXPERT_TILE
    nu = FFN_TILES_PER_STEP
    steps_per_seg = n_tiles // nseg // nu
    wspec = lambda k, n, u: pl.BlockSpec(
        (1, k, n), lambda i, te, *_: (layer * N_EXPERTS + te[_ffn_tile_index(i, u, steps_per_seg)], 0, 0))
    wspecs = [spec for u in range(nu)
              for spec in (wspec(D_MODEL, 2 * D_EXPERT, u), wspec(D_EXPERT, D_MODEL, u))]
    rows = pl.pallas_call(
        functools.partial(_moe_body, steps_per_seg=steps_per_seg),
        grid_spec=pltpu.PrefetchScalarGridSpec(
            num_scalar_prefetch=4,
            grid=(n_tiles // nu,),
            in_specs=[pl.BlockSpec((1, seg, d), lambda i, *_: (i // steps_per_seg, 0, 0),
                                   pipeline_mode=pl.Buffered(1))] + wspecs,
            out_specs=pl.BlockSpec((1, nu, 1, tr, d),
                                   lambda i, *_: (i // steps_per_seg, 0, i % steps_per_seg, 0, 0)),
            scratch_shapes=[pltpu.VMEM((2, nu, tr, d), F32)]),
        out_shape=jax.ShapeDtypeStruct((nseg, nu, steps_per_seg, tr, d), F32),
        compiler_params=_cparams("arbitrary"),
        name="routed_expert_ffn",
    )(tile_expert, tile_valid, tile_first, sorted_tok, h_seg, *([wgu, wd] * nu))
    return rows.reshape(n_tiles * tr, d)


def _ffn_tile_index(step, slot, steps_per_seg):
    seg = step // steps_per_seg
    return (seg * FFN_TILES_PER_STEP + slot) * steps_per_seg + step % steps_per_seg


def _combine_body(pa_ref, pb_ref, rows_ref, rt_ref, x_ref, mod_ref, fin_ref, o_ref, ab_scr, *,
                  final, seg_tiles):
    tm = o_ref.shape[1]
    tile = pl.program_id(0) * pl.num_programs(1) + pl.program_id(1)
    cur = tile % 2
    pos = (pa_ref, pb_ref)

    @pl.when(tile % seg_tiles == 0)
    def _():
        def gather(r, carry):
            for slot in range(2):
                ab_scr[cur, slot, pl.ds(r, 1), :] = rows_ref[0, pl.ds(pos[slot][tile * tm + r], 1), :]
            return carry

        lax.fori_loop(0, tm, gather, 0, unroll=8)

    rt = rt_ref[0]
    moe = rt[:, 2:3] * ab_scr[cur, 0] + rt[:, 3:4] * ab_scr[cur, 1]
    nxt = jnp.minimum(tile + 1, pl.num_programs(0) * pl.num_programs(1) - 1) * tm
    for r in range(tm):
        for slot in range(2):
            ab_scr[1 - cur, slot, r:r + 1, :] = rows_ref[0, pl.ds(pos[slot][nxt + r], 1), :]
    x = x_ref[0] + mod_ref[0, 5:6, :] * moe
    if final:
        x = x * lax.rsqrt(jnp.mean(x * x, axis=-1, keepdims=True) + NORM_EPS) * fin_ref[...]
    o_ref[0] = x


def _combine(pos_a, pos_b, rows_seg, route, x, modr, mod_row0, final_gain, final):
    b, l, _ = x.shape
    nseg, rseg, d = rows_seg.shape
    tm = min(COMBINE_TILE, l)
    tiles_l = l // tm
    seg_tiles = (b * tiles_l) // nseg
    tok = lambda w: pl.BlockSpec((1, tm, w), lambda i, j, *_: (i, j, 0))
    return pl.pallas_call(
        functools.partial(_combine_body, final=final, seg_tiles=seg_tiles),
        grid_spec=pltpu.PrefetchScalarGridSpec(
            num_scalar_prefetch=2,
            grid=(b, tiles_l),
            in_specs=[pl.BlockSpec((1, rseg, d), lambda i, j, *_: ((i * tiles_l + j) // seg_tiles, 0, 0),
                                   pipeline_mode=pl.Buffered(1)),
                      tok(ROUTE_LANES), tok(D_MODEL),
                      pl.BlockSpec((1, N_MOD, D_MODEL), lambda i, j, *_: (mod_row0 + i, 0, 0)),
                      pl.BlockSpec((1, D_MODEL), lambda i, j, *_: (0, 0))],
            out_specs=tok(D_MODEL),
            scratch_shapes=[pltpu.VMEM((2, 2, tm, d), F32)]),
        out_shape=jax.ShapeDtypeStruct((b, l, D_MODEL), F32),
        compiler_params=_cparams("arbitrary", "arbitrary"),
        name="moe_combine",
    )(pos_a, pos_b, rows_seg, route, x, modr, final_gain)


def _routing_tables(route_t, counts, nseg, seg):
    t = route_t.shape[1]
    tr = EXPERT_TILE
    rseg = 2 * seg + N_EXPERTS * tr
    ngrp = nseg * N_EXPERTS
    cnt = counts[0, :ngrp].astype(jnp.int32)
    pad2 = (((cnt + tr - 1) // tr) * tr).reshape(nseg, N_EXPERTS)
    end_local = jnp.cumsum(pad2, axis=1)
    start_local = (end_local - pad2).reshape(-1)
    start_unp = jnp.cumsum(cnt) - cnt
    ids = route_t[0:2].astype(jnp.int32)
    rank = route_t[4:6].astype(jnp.int32)
    key = (jnp.arange(t, dtype=jnp.int32) // seg)[None, :] * N_EXPERTS + ids
    hot = key[:, :, None] == jnp.arange(ngrp, dtype=jnp.int32)[None, None, :]
    lookup = lambda table: jnp.sum(jnp.where(hot, table[None, None, :], 0), axis=-1)
    pos = lookup(start_local) + rank
    order = lookup(start_unp) + rank
    tok = jnp.tile(jnp.arange(t, dtype=jnp.int32) % seg, 2)
    _, sorted_tok = lax.sort((order.reshape(-1), tok), num_keys=1)
    sorted_tok = jnp.concatenate([sorted_tok, jnp.zeros((tr,), jnp.int32)])
    tiles_per_seg = rseg // tr
    local_row = jnp.arange(tiles_per_seg, dtype=jnp.int32) * tr
    tile_grp = jnp.sum(local_row[None, :, None] >= end_local[:, None, :], axis=-1)
    tile_valid = (tile_grp < N_EXPERTS).astype(jnp.int32)
    tile_expert = jnp.minimum(tile_grp, N_EXPERTS - 1).astype(jnp.int32)
    grp = tile_expert + (jnp.arange(nseg, dtype=jnp.int32) * N_EXPERTS)[:, None]
    hot_t = grp[:, :, None] == jnp.arange(ngrp, dtype=jnp.int32)[None, None, :]
    skipped_pad = jnp.sum(jnp.where(hot_t, (start_local - start_unp)[None, None, :], 0), axis=-1)
    tile_first = (local_row[None, :] - skipped_pad) * tile_valid
    return (sorted_tok, pos[0], pos[1], tile_expert.reshape(-1), tile_valid.reshape(-1),
            tile_first.reshape(-1), rseg)


def _trunk(x, modr, mod_row0, nb, params):
    b, l, _ = x.shape
    t = b * l
    seg = min(MOE_SEGMENT, t)
    nseg = t // seg
    depth = params["w_in"].shape[0]
    for layer in range(depth):
        row0 = layer * nb + mod_row0
        proj, fx_min = _in_proj(x, modr, row0, params["norm_mix"], params["w_in"], layer)
        o_f, o_b = _hgrn(proj, params["lbs"], layer, fx_min)
        qt, k_rot, vt = _qk_prep(proj, params["q_gain2"], params["k_gain2"], layer)
        o_at = _attention(qt, k_rot, vt, params["score_bound"][layer])
        x_mid, h, route, route_t, counts = _out_proj(
            o_f, o_b, proj, o_at, x, modr, row0, params["hg_gain"], params["w_out"], params["norm_ffn"],
            params["w_route"], params["b_route"], layer, seg)
        sorted_tok, pos_a, pos_b, tile_e, tile_v, tile_f, rseg = _routing_tables(
            route_t, counts, nseg, seg)
        rows = _moe_ffn(h.reshape(nseg, seg, D_MODEL), sorted_tok, tile_e, tile_v, tile_f,
                        params["w_e_gate_up"], params["w_e_down"], layer)
        x = _combine(pos_a, pos_b, rows.reshape(nseg, rseg, D_MODEL), route, x_mid, modr, row0,
                     params["final_norm"], final=(layer == depth - 1))
    return x


def kernel(x_prompt, x_sample, c_prompt, c_sample, w_in, w_out, hg_lb, hg_out_norm, q_norm, k_norm,
           norm_mix, norm_ffn, w_ada, b_ada, w_group, b_group, w_router, b_router, w_e_gate, w_e_up,
           w_e_down, final_norm):
    depth = w_in.shape[0]
    bp, bs = c_prompt.shape[0], c_sample.shape[0]
    nb = -(-(bp + bs) // SUBLANES) * SUBLANES
    c_all = jnp.zeros((nb, D_MODEL), F32).at[:bp].set(c_prompt).at[bp:bp + bs].set(c_sample)
    mod = _modulation(c_all, w_ada, b_ada)
    modr = mod.reshape(depth * nb, N_MOD, D_MODEL)
    pad = ROUTE_LANES - N_GROUPS - N_EXPERTS
    params = {
        "w_in": w_in.astype(BF16),
        "w_out": w_out.astype(BF16),
        "lbs": _lower_bounds(hg_lb).reshape(depth, 1, HG_WIDTH),
        "hg_gain": hg_out_norm.reshape(depth, 1, HG_DIM),
        "q_gain2": jnp.tile(q_norm, (1, LANES // HEAD_DIM)).reshape(depth, 1, LANES),
        "k_gain2": jnp.tile(k_norm, (1, LANES // HEAD_DIM)).reshape(depth, 1, LANES),
        "score_bound": (1.02 * LOG2E * HEAD_DIM ** 0.5 * jnp.max(jnp.abs(q_norm), axis=1)
                        * jnp.max(jnp.abs(k_norm), axis=1)).reshape(depth, 1, 1),
        "norm_mix": norm_mix.reshape(depth, 1, D_MODEL),
        "norm_ffn": norm_ffn.reshape(depth, 1, D_MODEL),
        "w_route": jnp.concatenate(_split_bf16(jnp.pad(jnp.concatenate([w_group, w_router], axis=-1),
                                                       ((0, 0), (0, 0), (0, pad)))), axis=-1),
        "b_route": jnp.pad(jnp.concatenate([b_group, b_router], axis=-1),
                           ((0, 0), (0, pad))).reshape(depth, 1, ROUTE_LANES),
        "w_e_gate_up": jnp.concatenate([w_e_gate.astype(BF16), w_e_up.astype(BF16)], axis=-1).reshape(
            depth * N_EXPERTS, D_MODEL, 2 * D_EXPERT),
        "w_e_down": w_e_down.astype(BF16).reshape(depth * N_EXPERTS, D_EXPERT, D_MODEL),
        "final_norm": final_norm.reshape(1, D_MODEL),
    }
    y_prompt = _trunk(x_prompt, modr, 0, nb, params)
    y_sample = _trunk(x_sample, modr, bp, nb, params)
    return (y_prompt, y_sample)
```
